```python
import jax, jax.numpy as jnp
from jax import lax
import numpy as np

D_MODEL = 2048
BATCH = 4
SEQ = 2048
DEPTH = 1
DEC_BATCH = 8
DEC_SEQ = 8
PAST_LEN = 16384
PAGE_SIZE = 128

DH_A = 128
H_A = (D_MODEL // 2) // DH_A
W_A = H_A * DH_A
H_IDX = 16
D_IDX = 64
TOPK_MAX = 256
Q_BLOCK = 32
DK_B = 128
DV_B = 128
H_B = (D_MODEL // 4) // DV_B
W_B = H_B * DV_B
HGRN_CHUNK = 64
DH_M = 128
H_M = (D_MODEL // 4) // DH_M
W_M = H_M * DH_M
N_MEM = 256
ROPE_THETA = 500000.0
ROPE_FRAC = 4
LN_EPS = 1e-5
RMS_EPS = 1e-6
ALPHA = (2.0 * DEPTH) ** 0.25
BETA = (8.0 * DEPTH) ** -0.25
IN_SPLITS = (W_A, W_A, W_A, W_A, H_IDX * D_IDX, D_IDX, H_IDX,
             H_B * DK_B, H_B * DK_B, W_B, W_B, W_M, W_M)
N_IN = sum(IN_SPLITS)

kernel_name = "hymba_dsa_hgrn2_memxattn_step"


def _rope(x, pos):
    d = x.shape[-1]
    r = d // ROPE_FRAC
    half = r // 2
    inv = ROPE_THETA ** (-jnp.arange(half, dtype=jnp.float32) / half)
    ang = pos.astype(jnp.float32)[:, None] * inv[None, :]
    cos = jnp.cos(ang)[:, None, :]
    sin = jnp.sin(ang)[:, None, :]
    xr = x[..., :r].astype(jnp.float32)
    x1, x2 = xr[..., :half], xr[..., half:]
    rot = jnp.concatenate([x1 * cos - x2 * sin, x2 * cos + x1 * sin], axis=-1)
    return jnp.concatenate([rot.astype(x.dtype), x[..., r:]], axis=-1)


def _layernorm(x, g, b):
    xf = x.astype(jnp.float32)
    xc = xf - jnp.mean(xf, axis=-1, keepdims=True)
    var = jnp.mean(xc * xc, axis=-1, keepdims=True)
    return (xc * lax.rsqrt(var + LN_EPS) * g.astype(jnp.float32) + b.astype(jnp.float32)).astype(x.dtype)


def _in_proj(h, w, pos):
    n_b, n_t, _ = h.shape
    z = jnp.einsum('btd,dn->btn', h, w)
    cuts = [int(c) for c in np.cumsum(IN_SPLITS)[:-1]]
    aq, ak, av, ag, iq, ik, iw, bq, bf, bi, bg, mq, mg = jnp.split(z, cuts, axis=-1)
    aq = _rope(aq.reshape(n_b, n_t, H_A, DH_A), pos)
    ak = _rope(ak.reshape(n_b, n_t, H_A, DH_A), pos)
    av = av.reshape(n_b, n_t, H_A, DH_A)
    iq = _rope(iq.reshape(n_b, n_t, H_IDX, D_IDX), pos)
    ik = _rope(ik.reshape(n_b, n_t, 1, D_IDX), pos)[:, :, 0]
    mq = mq.reshape(n_b, n_t, H_M, DH_M)
    return aq, ak, av, ag, iq, ik, iw, bq, bf, bi, bg, mq, mg


def _indexer_scores(iq, iw, ik):
    dots = jnp.einsum('bthd,bld->bthl', iq.astype(jnp.float32), ik.astype(jnp.float32)) * (D_IDX ** -0.5)
    return jnp.einsum('bth,bthl->btl', iw.astype(jnp.float32) * (H_IDX ** -0.5), jax.nn.relu(dots))


def _sparse_attend(q, k_sel, v_sel, valid):
    s = jnp.einsum('bthd,btkhd->bthk', q.astype(jnp.float32), k_sel.astype(jnp.float32)) * (DH_A ** -0.5)
    s = jnp.where(valid[:, :, None, :], s, -jnp.inf)
    p = jax.nn.softmax(s, axis=-1)
    return jnp.einsum('bthk,btkhd->bthd', p, v_sel.astype(jnp.float32))


def _take_rows(a, idx):
    return jax.vmap(lambda ab, ib: ab[ib])(a, idx)


def _dsa_prompt(q, k, v, iq, iw, ik):
    n_b, n_t = q.shape[:2]
    n_sel = min(TOPK_MAX, n_t // 4)
    n_blk = n_t // Q_BLOCK
    kpos = jnp.arange(n_t)

    def block(i):
        t0 = i * Q_BLOCK
        qb = lax.dynamic_slice_in_dim(q, t0, Q_BLOCK, axis=1)
        iqb = lax.dynamic_slice_in_dim(iq, t0, Q_BLOCK, axis=1)
        iwb = lax.dynamic_slice_in_dim(iw, t0, Q_BLOCK, axis=1)
        qpos = t0 + jnp.arange(Q_BLOCK)
        sc = _indexer_scores(iqb, iwb, ik)
        sc = jnp.where(kpos[None, None, :] <= qpos[None, :, None], sc, -jnp.inf)
        _, sel = lax.top_k(sc, n_sel)
        valid = sel <= qpos[None, :, None]
        return _sparse_attend(qb, _take_rows(k, sel), _take_rows(v, sel), valid)

    out = lax.map(block, jnp.arange(n_blk))
    return jnp.moveaxis(out, 0, 1).reshape(n_b, n_t, H_A, DH_A)


def _dsa_sample(q, k_new, v_new, iq, iw, ik_new, cache_k, cache_v, cache_idx_k, page_table):
    n_b, n_t = q.shape[:2]
    past = page_table.shape[1] * PAGE_SIZE
    n_keys = past + n_t
    n_sel = min(TOPK_MAX, n_keys // 4)
    ik_past = cache_idx_k[page_table].reshape(n_b, past, D_IDX)
    ik_all = jnp.concatenate([ik_past.astype(ik_new.dtype), ik_new], axis=1)
    qpos = past + jnp.arange(n_t)
    kpos = jnp.arange(n_keys)
    sc = _indexer_scores(iq, iw, ik_all)
    sc = jnp.where(kpos[None, None, :] <= qpos[None, :, None], sc, -jnp.inf)
    _, sel = lax.top_k(sc, n_sel)
    valid = sel <= qpos[None, :, None]
    sel_past = jnp.minimum(sel, past - 1)
    phys = (page_table[jnp.arange(n_b)[:, None, None], sel_past // PAGE_SIZE] * PAGE_SIZE
            + sel_past % PAGE_SIZE)
    pool_k = cache_k.reshape(-1, H_A, DH_A)
    pool_v = cache_v.reshape(-1, H_A, DH_A)
    sel_new = jnp.clip(sel - past, 0, n_t - 1)
    is_new = (sel >= past)[..., None, None]
    k_sel = jnp.where(is_new, _take_rows(k_new, sel_new), pool_k[phys].astype(k_new.dtype))
    v_sel = jnp.where(is_new, _take_rows(v_new, sel_new), pool_v[phys].astype(v_new.dtype))
    return _sparse_attend(q, k_sel, v_sel, valid)


def _hgrn2(bq, bf, bi, lb, s0, chunk):
    n_b, n_t, _ = bq.shape
    q = jax.nn.silu(bq.astype(jnp.float32)).reshape(n_b, n_t, H_B, DK_B)
    f = lb + (1.0 - lb) * jax.nn.sigmoid(bf.astype(jnp.float32))
    log_f = jnp.log(f).reshape(n_b, n_t, H_B, DK_B)
    kk = (1.0 - f).reshape(n_b, n_t, H_B, DK_B)
    v = bi.astype(jnp.float32).reshape(n_b, n_t, H_B, DV_B)
    n_c = n_t // chunk
    tri = jnp.tril(jnp.ones((chunk, chunk), dtype=bool))

    def resh(a):
        return jnp.moveaxis(a.reshape(n_b, n_c, chunk, H_B, a.shape[-1]), 1, 0)

    def step(S, inp):
        qc, gc, kc, vc = inp
        G = jnp.cumsum(gc, axis=1)
        diff = G[:, :, None] - G[:, None, :]
        dec = jnp.exp(jnp.where(tri[None, :, :, None, None], diff, -jnp.inf))
        A = jnp.einsum('bthd,btshd->bhts', qc, kc[:, None] * dec)
        o = (jnp.einsum('bhts,bshv->bthv', A, vc)
             + jnp.einsum('bthd,bhdv->bthv', qc * jnp.exp(G), S))
        G_last = G[:, -1]
        k_dec = kc * jnp.exp(G_last[:, None] - G)
        S_new = jnp.exp(G_last)[..., None] * S + jnp.einsum('bshd,bshv->bhdv', k_dec, vc)
        return S_new, o

    S, o = lax.scan(step, s0.astype(jnp.float32), (resh(q), resh(log_f), resh(kk), resh(v)))
    return jnp.moveaxis(o, 0, 1).reshape(n_b, n_t, H_B, DV_B), S


def _mem_attend(q, mk, mv):
    s = jnp.einsum('bthd,bnhd->bhtn', q.astype(jnp.float32), mk.astype(jnp.float32)) * (DH_M ** -0.5)
    p = jax.nn.softmax(s, axis=-1)
    return jnp.einsum('bhtn,bnhd->bthd', p, mv.astype(jnp.float32))


def _merge(h, a_out, ag, b_o, bg, m_out, mg, norm_g, w_out, ln_g, ln_b):
    n_b, n_t, _ = h.shape
    a = a_out.reshape(n_b, n_t, W_A) * jax.nn.silu(ag.astype(jnp.float32))
    bn = b_o * lax.rsqrt(jnp.mean(b_o * b_o, axis=-1, keepdims=True) + RMS_EPS) * norm_g.astype(jnp.float32)
    b = bn.reshape(n_b, n_t, W_B) * jax.nn.silu(bg.astype(jnp.float32))
    m = m_out.reshape(n_b, n_t, W_M) * jax.nn.silu(mg.astype(jnp.float32))
    cat = jnp.concatenate([a, b, m], axis=-1).astype(h.dtype)
    y = jnp.einsum('btn,nd->btd', cat, w_out)
    return _layernorm(ALPHA * h + y, ln_g, ln_b)


def setup_inputs(seed: int = 0) -> dict:
    key = jax.random.key(seed)
    ks = jax.random.split(key, 20)
    n_pages = PAST_LEN // PAGE_SIZE
    n_phys = (DEC_BATCH * n_pages * 5 + 3) // 4
    nrm = jax.random.normal
    x_prompt = nrm(ks[0], (BATCH, SEQ, D_MODEL), jnp.float32)
    x_sample = nrm(ks[1], (DEC_BATCH, DEC_SEQ, D_MODEL), jnp.float32)
    mem_prompt = nrm(ks[2], (BATCH, N_MEM, D_MODEL), jnp.float32)
    cache_k = nrm(ks[3], (DEPTH, n_phys, PAGE_SIZE, H_A, DH_A), jnp.float32)
    cache_v = nrm(ks[4], (DEPTH, n_phys, PAGE_SIZE, H_A, DH_A), jnp.float32)
    cache_idx_k = nrm(ks[5], (DEPTH, n_phys, PAGE_SIZE, D_IDX), jnp.float32)
    state_hgrn = 0.5 * nrm(ks[6], (DEPTH, DEC_BATCH, H_B, DK_B, DV_B), jnp.float32)
    cache_mem_k = nrm(ks[7], (DEPTH, DEC_BATCH, N_MEM, H_M, DH_M), jnp.float32)
    cache_mem_v = nrm(ks[8], (DEPTH, DEC_BATCH, N_MEM, H_M, DH_M), jnp.float32)
    page_table = jax.random.permutation(ks[9], n_phys)[:DEC_BATCH * n_pages].reshape(
        DEC_BATCH, n_pages).astype(jnp.int32)
    off = np.concatenate([[0], np.cumsum(IN_SPLITS)])
    col_scale = np.ones((N_IN,), np.float32)
    col_scale[off[2]:off[3]] = BETA
    col_scale[off[9]:off[10]] = BETA
    w_in = nrm(ks[10], (DEPTH, D_MODEL, N_IN), jnp.float32) * (D_MODEL ** -0.5) * jnp.asarray(col_scale)
    lb_logits = 0.1 * nrm(ks[11], (DEPTH + 1, H_B * DK_B), jnp.float32)
    hgrn_norm_g = 1.0 + 0.02 * nrm(ks[12], (DEPTH, DV_B), jnp.float32)
    w_mem_k = nrm(ks[13], (DEPTH, D_MODEL, W_M), jnp.float32) * (D_MODEL ** -0.5)
    w_mem_v = nrm(ks[14], (DEPTH, D_MODEL, W_M), jnp.float32) * (D_MODEL ** -0.5) * BETA
    w_out = nrm(ks[15], (DEPTH, D_MODEL, D_MODEL), jnp.float32) * (D_MODEL ** -0.5) * BETA
    ln_g = 1.0 + 0.02 * nrm(ks[16], (DEPTH, D_MODEL), jnp.float32)
    ln_b = 0.02 * nrm(ks[17], (DEPTH, D_MODEL), jnp.float32)
    return {"x_prompt": x_prompt, "x_sample": x_sample, "mem_prompt": mem_prompt,
            "cache_k": cache_k, "cache_v": cache_v, "cache_idx_k": cache_idx_k,
            "state_hgrn": state_hgrn, "cache_mem_k": cache_mem_k, "cache_mem_v": cache_mem_v,
            "page_table": page_table, "w_in": w_in, "lb_logits": lb_logits,
            "hgrn_norm_g": hgrn_norm_g, "w_mem_k": w_mem_k, "w_mem_v": w_mem_v,
            "w_out": w_out, "ln_g": ln_g, "ln_b": ln_b}


def reference(x_prompt, x_sample, mem_prompt, cache_k, cache_v, cache_idx_k, state_hgrn,
              cache_mem_k, cache_mem_v, page_table, w_in, lb_logits, hgrn_norm_g,
              w_mem_k, w_mem_v, w_out, ln_g, ln_b):
    lb_all = jnp.cumsum(jax.nn.softmax(lb_logits.astype(jnp.float32), axis=0), axis=0)
    n_bp, t_p, _ = x_prompt.shape
    n_bs, t_s, _ = x_sample.shape
    past = page_table.shape[1] * PAGE_SIZE
    pos_p = jnp.arange(t_p, dtype=jnp.int32)
    pos_s = past + jnp.arange(t_s, dtype=jnp.int32)
    hp, hs = x_prompt, x_sample
    kp_l, vp_l, ikp_l, sp_l, mkp_l, mvp_l = [], [], [], [], [], []
    ks_l, vs_l, iks_l, ss_l = [], [], [], []
    for l in range(DEPTH):
        lb = lb_all[l]
        aq, ak, av, ag, iq, ik, iw, bq, bf, bi, bg, mq, mg = _in_proj(hp, w_in[l], pos_p)
        a_out = _dsa_prompt(aq, ak, av, iq, iw, ik)
        s0 = jnp.zeros((n_bp, H_B, DK_B, DV_B), jnp.float32)
        b_o, s_p = _hgrn2(bq, bf, bi, lb, s0, min(HGRN_CHUNK, t_p))
        mk = jnp.einsum('bnd,dm->bnm', mem_prompt, w_mem_k[l]).reshape(n_bp, N_MEM, H_M, DH_M)
        mv = jnp.einsum('bnd,dm->bnm', mem_prompt, w_mem_v[l]).reshape(n_bp, N_MEM, H_M, DH_M)
        m_out = _mem_attend(mq, mk, mv)
        hp_next = _merge(hp, a_out, ag, b_o, bg, m_out, mg, hgrn_norm_g[l], w_out[l], ln_g[l], ln_b[l])
        kp_l.append(ak); vp_l.append(av); ikp_l.append(ik); sp_l.append(s_p)
        mkp_l.append(mk); mvp_l.append(mv)
        aq, ak, av, ag, iq, ik, iw, bq, bf, bi, bg, mq, mg = _in_proj(hs, w_in[l], pos_s)
        a_out = _dsa_sample(aq, ak, av, iq, iw, ik, cache_k[l], cache_v[l], cache_idx_k[l], page_table)
        b_o, s_s = _hgrn2(bq, bf, bi, lb, state_hgrn[l], t_s)
        m_out = _mem_attend(mq, cache_mem_k[l], cache_mem_v[l])
        hs_next = _merge(hs, a_out, ag, b_o, bg, m_out, mg, hgrn_norm_g[l], w_out[l], ln_g[l], ln_b[l])
        ks_l.append(ak); vs_l.append(av); iks_l.append(ik); ss_l.append(s_s)
        hp, hs = hp_next, hs_next
    return (hp, hs, jnp.stack(kp_l), jnp.stack(vp_l), jnp.stack(ikp_l), jnp.stack(sp_l),
            jnp.stack(mkp_l), jnp.stack(mvp_l), jnp.stack(ks_l), jnp.stack(vs_l),
            jnp.stack(iks_l), jnp.stack(ss_l))
```

```python
import functools

import numpy as np
import jax
import jax.numpy as jnp
from jax import lax
from jax.experimental import pallas as pl
from jax.experimental.pallas import tpu as pltpu

D_MODEL = 2048
DH_A = 128
H_A = 8
W_A = H_A * DH_A
H_IDX = 16
D_IDX = 64
W_IDX = H_IDX * D_IDX
TOPK_MAX = 256
DK_B = 128
DV_B = 128
H_B = 4
W_B = H_B * DV_B
DH_M = 128
H_M = 4
W_M = H_M * DH_M
N_MEM = 256
PAGE_SIZE = 128
ROPE_THETA = 500000.0
ROPE_FRAC = 4
LN_EPS = 1e-5
RMS_EPS = 1e-6
DEPTH = 1
ALPHA = (2.0 * DEPTH) ** 0.25

LANE = 128
VMEM_LIMIT = 48 * 1024 * 1024

C_AQ, C_AK, C_AV, C_AG, C_IQ = 0, 1024, 2048, 3072, 4096
C_BQ, C_BF, C_BI, C_BG, C_MQ, C_MG = 5120, 5632, 6144, 6656, 7168, 7680
N_MAIN = 8192

NEG = -1e30
INT_MIN = -(2 ** 31)
BF16 = jnp.bfloat16
F32 = jnp.float32


def _dot(a, b, **kw):
    return jnp.dot(a, b, preferred_element_type=F32, **kw)


def _dot_nt(a, b):
    return lax.dot_general(a, b, (((1,), (1,)), ((), ())), preferred_element_type=F32)


def _dot_tn(a, b):
    return lax.dot_general(a, b, (((0,), (0,)), ((), ())), preferred_element_type=F32)


def _sigmoid(x):
    return 1.0 / (1.0 + jnp.exp(-x))


def _silu(x):
    return x * _sigmoid(x)


def _cparams(sem):
    return pltpu.CompilerParams(dimension_semantics=sem, vmem_limit_bytes=VMEM_LIMIT)


def _proj_kernel(x_ref, w_ref, ws_ref, o_ref, os_ref, xb_ref):
    j = pl.program_id(1)

    @pl.when(j == 0)
    def _():
        xb = x_ref[...].astype(BF16)
        xb_ref[...] = xb
        os_ref[...] = _dot(xb, ws_ref[...])

    o_ref[...] = _dot(xb_ref[...], w_ref[...])


def _proj(x, w_main, w_small, tm, tn):
    m = x.shape[0]
    n = w_main.shape[1]
    return pl.pallas_call(
        _proj_kernel,
        grid=(m // tm, n // tn),
        in_specs=[pl.BlockSpec((tm, D_MODEL), lambda i, j: (i, 0)),
                  pl.BlockSpec((D_MODEL, tn), lambda i, j: (0, j)),
                  pl.BlockSpec((D_MODEL, LANE), lambda i, j: (0, 0))],
        out_specs=[pl.BlockSpec((tm, tn), lambda i, j: (i, j)),
                   pl.BlockSpec((tm, LANE), lambda i, j: (i, 0))],
        out_shape=[jax.ShapeDtypeStruct((m, n), F32),
                   jax.ShapeDtypeStruct((m, LANE), F32)],
        scratch_shapes=[pltpu.VMEM((tm, D_MODEL), BF16)],
        compiler_params=_cparams(("parallel", "arbitrary")),
        name="in_proj",
    )(x, w_main, w_small)


def _rope(x, c, sa, sb, half):
    return x * c + pltpu.roll(x, LANE - half, 1) * sa + pltpu.roll(x, half, 1) * sb


def _prep_kernel(aq_ref, ak_ref, av_ref, iq_ref, zs_ref,
                 c1_ref, sa1_ref, sb1_ref, c2_ref, sa2_ref, sb2_ref,
                 q_ref, kf_ref, kb_ref, vf_ref, vb_ref, iqw_ref, sgn_ref,
                 ikf_ref, ike_ref, iko_ref):
    c1, sa1, sb1 = c1_ref[...], sa1_ref[...], sb1_ref[...]
    c2, sa2, sb2 = c2_ref[...], sa2_ref[...], sb2_ref[...]
    half_a = DH_A // ROPE_FRAC // 2
    half_i = D_IDX // ROPE_FRAC // 2
    for h in range(H_A):
        sl = slice(h * DH_A, (h + 1) * DH_A)
        q_ref[:, sl] = (_rope(aq_ref[:, sl], c1, sa1, sb1, half_a) * (DH_A ** -0.5)).astype(BF16)
        kr = _rope(ak_ref[:, sl], c1, sa1, sb1, half_a)
        kf_ref[:, sl] = kr
        kb_ref[:, sl] = kr.astype(BF16)
    v = av_ref[...]
    vf_ref[...] = v
    vb_ref[...] = v.astype(BF16)

    zs = zs_ref[...]
    lane = lax.broadcasted_iota(jnp.int32, zs.shape, 1)
    ikr = _rope(zs, c2, sa2, sb2, half_i)
    ikf_ref[...] = ikr[:, :D_IDX]
    ike_ref[...] = jnp.where(lane < D_IDX, ikr, 0.0).astype(BF16)
    iko_ref[...] = jnp.where(lane >= D_IDX, pltpu.roll(ikr, D_IDX, 1), 0.0).astype(BF16)
    sgn_ref[...] = jnp.sign(pltpu.roll(zs, LANE - D_IDX, 1))
    wscale = (D_IDX ** -0.5) * (H_IDX ** -0.5)
    for p in range(H_IDX // 2):
        sl = slice(p * LANE, (p + 1) * LANE)
        wa = jnp.abs(zs[:, D_IDX + 2 * p:D_IDX + 2 * p + 1])
        wb = jnp.abs(zs[:, D_IDX + 2 * p + 1:D_IDX + 2 * p + 2])
        wpair = jnp.where(lane < D_IDX, wa, wb) * wscale
        iqw_ref[:, sl] = (_rope(iq_ref[:, sl], c2, sa2, sb2, half_i) * wpair).astype(BF16)


def _prep(z, zs, tabs, tm):
    m = z.shape[0]
    period = tabs[0].shape[0] // tm
    zblk = lambda c: pl.BlockSpec((tm, 1024), lambda i, c=c: (i, c // 1024))
    tab = pl.BlockSpec((tm, LANE), lambda i: (i % period, 0))
    row = lambda w: pl.BlockSpec((tm, w), lambda i: (i, 0))
    sds = lambda w, dt: jax.ShapeDtypeStruct((m, w), dt)
    return pl.pallas_call(
        _prep_kernel,
        grid=(m // tm,),
        in_specs=[zblk(C_AQ), zblk(C_AK), zblk(C_AV), zblk(C_IQ), row(LANE)] + [tab] * 6,
        out_specs=[row(W_A), row(W_A), row(W_A), row(W_A), row(W_A), row(W_IDX), row(LANE),
                   row(D_IDX), row(LANE), row(LANE)],
        out_shape=[sds(W_A, BF16), sds(W_A, F32), sds(W_A, BF16), sds(W_A, F32), sds(W_A, BF16),
                   sds(W_IDX, BF16), sds(LANE, F32), sds(D_IDX, F32), sds(LANE, BF16), sds(LANE, BF16)],
        compiler_params=_cparams(("parallel",)),
        name="rope_prep",
    )(z, z, z, z, zs, *tabs)


def _rope_tables(pos, d):
    r = d // ROPE_FRAC
    half = r // 2
    n = pos.shape[0]
    inv = ROPE_THETA ** (-jnp.arange(half, dtype=F32) / half)
    ang = pos.astype(F32)[:, None] * inv[None, :]
    cos, sin = jnp.cos(ang), jnp.sin(ang)
    zh = jnp.zeros((n, half), F32)
    c = jnp.concatenate([cos, cos, jnp.ones((n, d - r), F32)], axis=1)
    sa = jnp.concatenate([-sin, zh, jnp.zeros((n, d - r), F32)], axis=1)
    sb = jnp.concatenate([zh, sin, jnp.zeros((n, d - r), F32)], axis=1)
    reps = LANE // d
    return [jnp.tile(t, (1, reps)) for t in (c, sa, sb)]


def _mono_key(score):
    b = lax.bitcast_convert_type(score, jnp.int32)
    return b ^ (lax.shift_right_arithmetic(b, 31) & jnp.int32(0x7FFFFFFF))


def _radix_select(count_ge, rows, total, nsel):
    def body(bi, carry):
        pu, cacc = carry
        bit = jnp.int32(31) - bi
        cand_u = pu | lax.shift_left(jnp.int32(1), bit)
        cnt = count_ge(cand_u ^ jnp.int32(INT_MIN))
        ok = cnt >= nsel
        return jnp.where(ok, cand_u, pu), jnp.where(ok, cnt, cacc)

    pu0 = jnp.zeros((rows, 1), jnp.int32)
    c0 = jnp.full((rows, 1), total, F32)
    pu, cacc = lax.fori_loop(0, 32, body, (pu0, c0))
    return pu ^ jnp.int32(INT_MIN), cacc


def _dsa_prompt_kernel(q_ref, k_ref, v_ref, iq_ref, sgn_ref, ike_ref, iko_ref, o_ref,
                       sgnb_ref, key_ref, bias_ref, m_ref, l_ref, acc_ref, *, tq, nsel):
    tk = tq
    i = pl.program_id(1)
    nch = i + 1
    reps = tk // LANE

    sgn = sgn_ref[...]
    for h in range(H_IDX):
        sgnb_ref[h] = jnp.broadcast_to(sgn[:, h:h + 1], (tq, tk))

    rows = i * tq + lax.broadcasted_iota(jnp.int32, (tq, tk), 0)
    cols0 = lax.broadcasted_iota(jnp.int32, (tq, tk), 1)

    def chunk_slice(c):
        return pl.ds(pl.multiple_of(c * tk, tk), tk)

    def score_body(c, carry):
        ds = chunk_slice(c)
        ke = ike_ref[ds, :]
        ko = iko_ref[ds, :]
        acc = jnp.zeros((tq, tk), F32)
        for h in range(H_IDX):
            lhs = iq_ref[:, (h // 2) * LANE:(h // 2 + 1) * LANE]
            d = _dot_nt(lhs, ke if h % 2 == 0 else ko)
            acc = acc + sgnb_ref[h] * jnp.maximum(d, 0.0)
        key = _mono_key(acc)
        key_ref[:, ds] = jnp.where(cols0 + c * tk <= rows, key, jnp.int32(INT_MIN))
        return carry

    lax.fori_loop(0, nch, score_body, 0)

    def count(cmp_fn):
        def body(c, part):
            kc = key_ref[:, chunk_slice(c)]
            ge = jnp.where(cmp_fn(kc), 1.0, 0.0)
            for r_ in range(reps):
                part = part + ge[:, r_ * LANE:(r_ + 1) * LANE]
            return part
        part = lax.fori_loop(0, nch, body, jnp.zeros((tq, LANE), F32))
        return jnp.sum(part, axis=1, keepdims=True)

    total = (nch * tk).astype(F32)
    tau, c_ge = _radix_select(lambda cand: count(lambda kc: kc >= cand), tq, total, float(nsel))
    tau_eff = jnp.maximum(tau, jnp.int32(INT_MIN + 1))

    def bias_body(c, carry):
        ds = chunk_slice(c)
        bias_ref[:, ds] = jnp.where(key_ref[:, ds] >= tau_eff, 0.0, NEG)
        return carry

    lax.fori_loop(0, nch, bias_body, 0)

    tie_rows = jnp.logical_and(c_ge > float(nsel), tau > jnp.int32(INT_MIN))
    has_tie = jnp.max(jnp.where(tie_rows, 1.0, 0.0)) > 0.5

    @pl.when(has_tie)
    def _():
        c_gt = count(lambda kc: kc > tau)
        need = float(nsel) - c_gt
        upper = jnp.where(lax.broadcasted_iota(jnp.int32, (tk, tk), 0) <= cols0[:tk, :], 1.0, 0.0).astype(BF16)
        real = tau > jnp.int32(INT_MIN)

        def tie_body(c, run):
            ds = chunk_slice(c)
            kc = key_ref[:, ds]
            eq = kc == tau
            eqf = jnp.where(eq, 1.0, 0.0)
            pre = _dot(eqf.astype(BF16), upper) + run
            keep = jnp.where(kc > tau, 1.0, jnp.where(real, eqf * jnp.where(pre <= need, 1.0, 0.0), 0.0))
            bias_ref[:, ds] = jnp.where(keep > 0.5, 0.0, NEG)
            return run + jnp.sum(eqf, axis=1, keepdims=True)

        lax.fori_loop(0, nch, tie_body, jnp.zeros((tq, 1), F32))

    m_ref[...] = jnp.full(m_ref.shape, NEG, F32)
    l_ref[...] = jnp.zeros(l_ref.shape, F32)
    acc_ref[...] = jnp.zeros(acc_ref.shape, F32)

    def attn_body(c, carry):
        ds = chunk_slice(c)
        bias = bias_ref[:, ds]
        for h in range(H_A):
            hs = slice(h * DH_A, (h + 1) * DH_A)
            s = _dot_nt(q_ref[:, hs], k_ref[ds, hs]) + bias
            m_prev = m_ref[h]
            m_new = jnp.maximum(m_prev, jnp.max(s, axis=1, keepdims=True))
            alpha = jnp.exp(m_prev - m_new)
            p = jnp.exp(s - jnp.concatenate([m_new] * reps, axis=1))
            l_ref[h] = alpha * l_ref[h] + jnp.sum(p, axis=1, keepdims=True)
            acc_ref[h] = alpha * acc_ref[h] + _dot(p.astype(BF16), v_ref[ds, hs])
            m_ref[h] = m_new
        return carry

    lax.fori_loop(0, nch, attn_body, 0)
    for h in range(H_A):
        o_ref[:, h * DH_A:(h + 1) * DH_A] = acc_ref[h] / l_ref[h]


def _dsa_prompt(q, k, v, iqw, sgn, ike, iko, n_b, n_t, tq):
    nsel = min(TOPK_MAX, n_t // 4)
    nq = n_t // tq
    qblk = lambda w: pl.BlockSpec((tq, w), lambda b, i: (b * nq + i, 0))
    full = lambda w: pl.BlockSpec((n_t, w), lambda b, i: (b, 0))
    return pl.pallas_call(
        functools.partial(_dsa_prompt_kernel, tq=tq, nsel=nsel),
        grid=(n_b, nq),
        in_specs=[qblk(W_A), full(W_A), full(W_A), qblk(W_IDX), qblk(LANE), full(LANE), full(LANE)],
        out_specs=qblk(W_A),
        out_shape=jax.ShapeDtypeStruct((n_b * n_t, W_A), F32),
        scratch_shapes=[pltpu.VMEM((H_IDX, tq, tq), F32),
                        pltpu.VMEM((tq, n_t), jnp.int32),
                        pltpu.VMEM((tq, n_t), F32),
                        pltpu.VMEM((H_A, tq, LANE), F32),
                        pltpu.VMEM((H_A, tq, LANE), F32),
                        pltpu.VMEM((H_A, tq, DH_A), F32)],
        compiler_params=_cparams(("parallel", "arbitrary")),
        name="dsa_prompt",
    )(q, k, v, iqw, sgn, ike, iko)


def _hgrn_kernel(bq_ref, bf_ref, bi_ref, lb_ref, g_ref, s0_ref, o_ref, s_out_ref, st_ref,
                 *, n_t, chunk, mm_dtype):
    c_ = chunk
    lb = lb_ref[...]
    g = g_ref[...]
    st_ref[...] = s0_ref[0, 0].T
    row = lax.broadcasted_iota(jnp.int32, (c_, c_), 0)
    col = lax.broadcasted_iota(jnp.int32, (c_, c_), 1)
    lower = jnp.where(col <= row, 1.0, 0.0)
    levels = []
    m_ = 8
    while m_ < c_:
        levels.append(m_)
        m_ *= 2

    def chunk_body(ci, carry):
        ds = pl.ds(pl.multiple_of(ci * c_, c_), c_)
        bq = bq_ref[ds, :]
        qs = bq * _sigmoid(bq)
        f = lb + (1.0 - lb) * _sigmoid(bf_ref[ds, :])
        kk = 1.0 - f
        v = bi_ref[ds, :]
        gcum = _dot(lower, jnp.log(f), precision=lax.Precision.HIGHEST)
        a = jnp.zeros((c_, c_), F32)
        for m in levels:
            sh = int(np.log2(2 * m))
            g3 = gcum.reshape(c_ // (2 * m), 2 * m, DK_B)
            ref = jnp.broadcast_to(g3[:, m - 1:m, :], g3.shape).reshape(c_, DK_B)
            qh = qs * jnp.exp(jnp.minimum(gcum - ref, 0.0))
            kh = kk * jnp.exp(jnp.minimum(ref - gcum, 0.0))
            am = _dot_nt(qh.astype(mm_dtype), kh.astype(mm_dtype))
            same = (row >> sh) == (col >> sh)
            lo_hi = jnp.logical_and((row & (2 * m - 1)) >= m, (col & (2 * m - 1)) < m)
            a = a + jnp.where(jnp.logical_and(same, lo_hi), am, 0.0)
        for j in range(8):
            kj = kk if j == 0 else pltpu.roll(kk, j, 0)
            gj = gcum if j == 0 else pltpu.roll(gcum, j, 0)
            e = jnp.exp(jnp.minimum(gcum - gj, 0.0))
            aj = jnp.sum(qs * kj * e, axis=1, keepdims=True)
            on_diag = jnp.logical_and(row - col == j, (row & 7) >= j)
            a = a + jnp.where(on_diag, aj, 0.0)
        st = st_ref[...]
        o = (_dot_nt((qs * jnp.exp(gcum)).astype(mm_dtype), st.astype(mm_dtype))
             + _dot(a.astype(mm_dtype), v.astype(mm_dtype)))
        glast = gcum[c_ - 1:c_, :]
        kd = kk * jnp.exp(glast - gcum)
        st_ref[...] = jnp.exp(glast) * st + _dot_tn(v.astype(mm_dtype), kd.astype(mm_dtype))
        o_ref[ds, :] = o * lax.rsqrt(jnp.mean(o * o, axis=1, keepdims=True) + RMS_EPS) * g
        return carry

    lax.fori_loop(0, n_t // c_, chunk_body, 0)
    s_out_ref[0, 0] = st_ref[...].T


def _hgrn(z, lb, g, s0, n_b, n_t, chunk, mm_dtype):
    zcol = lambda c: pl.BlockSpec((n_t, DK_B), lambda b, h, c=c: (b, c // DK_B + h))
    sblk = pl.BlockSpec((1, 1, DK_B, DV_B), lambda b, h: (b, h, 0, 0))
    return pl.pallas_call(
        functools.partial(_hgrn_kernel, n_t=n_t, chunk=chunk, mm_dtype=mm_dtype),
        grid=(n_b, H_B),
        in_specs=[zcol(C_BQ), zcol(C_BF), zcol(C_BI),
                  pl.BlockSpec((1, DK_B), lambda b, h: (0, h)),
                  pl.BlockSpec((1, DV_B), lambda b, h: (0, 0)),
                  sblk],
        out_specs=[pl.BlockSpec((n_t, DV_B), lambda b, h: (b, h)), sblk],
        out_shape=[jax.ShapeDtypeStruct((n_b * n_t, W_B), F32),
                   jax.ShapeDtypeStruct((n_b, H_B, DK_B, DV_B), F32)],
        scratch_shapes=[pltpu.VMEM((DV_B, DK_B), F32)],
        compiler_params=_cparams(("parallel", "arbitrary")),
        name="hgrn2",
    )(z, z, z, lb, g, s0)


def _mem_kernel(q_ref, mk_ref, mv_ref, o_ref, *, mm_dtype):
    for h in range(H_M):
        hs = slice(h * DH_M, (h + 1) * DH_M)
        q = (q_ref[:, hs] * (DH_M ** -0.5)).astype(mm_dtype)
        s = _dot_nt(q, mk_ref[:, hs].astype(mm_dtype))
        p = jnp.exp(s - jnp.max(s, axis=1, keepdims=True))
        l = jnp.sum(p, axis=1, keepdims=True)
        o_ref[:, hs] = _dot(p.astype(mm_dtype), mv_ref[:, hs].astype(mm_dtype)) / l


def _mem_attend(z, mk, mk_col, mv, mv_col, n_b, n_t, tq, mm_dtype):
    nq = n_t // tq
    return pl.pallas_call(
        functools.partial(_mem_kernel, mm_dtype=mm_dtype),
        grid=(n_b, nq),
        in_specs=[pl.BlockSpec((tq, W_M), lambda b, i: (b * nq + i, C_MQ // W_M)),
                  pl.BlockSpec((N_MEM, W_M), lambda b, i: (b, mk_col)),
                  pl.BlockSpec((N_MEM, W_M), lambda b, i: (b, mv_col))],
        out_specs=pl.BlockSpec((tq, W_M), lambda b, i: (b * nq + i, 0)),
        out_shape=jax.ShapeDtypeStruct((n_b * n_t, W_M), F32),
        compiler_params=_cparams(("parallel", "parallel")),
        name="mem_attend",
    )(z, mk, mv)


def _merge_kernel(a_ref, ag_ref, bn_ref, bg_ref, mo_ref, mg_ref, h_ref, wo_ref, lg_ref, lbias_ref, o_ref):
    a = (a_ref[...] * _silu(ag_ref[...])).astype(BF16)
    b = (bn_ref[...] * _silu(bg_ref[...])).astype(BF16)
    m = (mo_ref[...] * _silu(mg_ref[...])).astype(BF16)
    y = (_dot(a, wo_ref[0:W_A, :]) + _dot(b, wo_ref[W_A:W_A + W_B, :])
         + _dot(m, wo_ref[W_A + W_B:D_MODEL, :]))
    r = ALPHA * h_ref[...] + y
    xc = r - jnp.mean(r, axis=1, keepdims=True)
    var = jnp.mean(xc * xc, axis=1, keepdims=True)
    o_ref[...] = xc * lax.rsqrt(var + LN_EPS) * lg_ref[...] + lbias_ref[...]


def _merge(a_out, z, bn, m_out, h, w_out, ln_g, ln_b, tm):
    m = h.shape[0]
    row = lambda w: pl.BlockSpec((tm, w), lambda i: (i, 0))
    zcol = lambda c, w: pl.BlockSpec((tm, w), lambda i, c=c, w=w: (i, c // w))
    const = lambda s: pl.BlockSpec(s, lambda i: (0, 0))
    return pl.pallas_call(
        _merge_kernel,
        grid=(m // tm,),
        in_specs=[row(W_A), zcol(C_AG, W_A), row(W_B), zcol(C_BG, W_B), row(W_M), zcol(C_MG, W_M),
                  row(D_MODEL), const((D_MODEL, D_MODEL)), const((1, D_MODEL)), const((1, D_MODEL))],
        out_specs=row(D_MODEL),
        out_shape=jax.ShapeDtypeStruct((m, D_MODEL), F32),
        compiler_params=_cparams(("parallel",)),
        name="merge",
    )(a_out, z, bn, z, m_out, z, h, w_out, ln_g, ln_b)


PAGES_PER_STEP = 8


def _dsa_sample_select_kernel(pt_ref, *refs, n_t, n_keys_pad, steps, nsel):
    pages = refs[:PAGES_PER_STEP]
    lq_ref, sg_ref, iknew_ref, bias_ref, key_ref = refs[PAGES_PER_STEP:]
    j = pl.program_id(1)
    lq = lq_ref[0]
    sg = sg_ref[0]

    def score(ikp):
        d = _dot_nt(lq, ikp.astype(BF16))
        r = jnp.maximum(d, 0.0) * sg
        s = r[0:n_t]
        for h in range(1, H_IDX):
            s = s + r[h * n_t:(h + 1) * n_t]
        return s

    for u in range(PAGES_PER_STEP):
        off = pl.multiple_of((j * PAGES_PER_STEP + u) * PAGE_SIZE, PAGE_SIZE)
        key_ref[:, pl.ds(off, PAGE_SIZE)] = _mono_key(score(pages[u][0]))

    @pl.when(j == steps - 1)
    def _():
        past = n_keys_pad - PAGE_SIZE
        s = score(iknew_ref[0])
        t_i = lax.broadcasted_iota(jnp.int32, (n_t, PAGE_SIZE), 0)
        k_i = lax.broadcasted_iota(jnp.int32, (n_t, PAGE_SIZE), 1)
        key_ref[:, past:n_keys_pad] = jnp.where(k_i <= t_i, _mono_key(s), jnp.int32(INT_MIN))

        keys = key_ref[...]

        def count_ge(cand):
            return jnp.sum(jnp.where(keys >= cand, 1.0, 0.0), axis=1, keepdims=True)

        tau, c_ge = _radix_select(count_ge, n_t, float(n_keys_pad), float(nsel))
        tau_eff = jnp.maximum(tau, jnp.int32(INT_MIN + 1))
        bias_ref[0] = jnp.where(keys >= tau_eff, 0.0, NEG)

        tie_rows = jnp.logical_and(c_ge > float(nsel), tau > jnp.int32(INT_MIN))
        has_tie = jnp.max(jnp.where(tie_rows, 1.0, 0.0)) > 0.5

        @pl.when(has_tie)
        def _():
            c_gt = jnp.sum(jnp.where(keys > tau, 1.0, 0.0), axis=1, keepdims=True)
            need = float(nsel) - c_gt
            ri = lax.broadcasted_iota(jnp.int32, (PAGE_SIZE, PAGE_SIZE), 0)
            ci = lax.broadcasted_iota(jnp.int32, (PAGE_SIZE, PAGE_SIZE), 1)
            upper = jnp.where(ri <= ci, 1.0, 0.0)
            real = tau > jnp.int32(INT_MIN)

            def tie_body(c, run):
                ds = pl.ds(pl.multiple_of(c * PAGE_SIZE, PAGE_SIZE), PAGE_SIZE)
                kc = key_ref[:, ds]
                eqf = jnp.where(kc == tau, 1.0, 0.0)
                pre = _dot(eqf, upper) + run
                keep = jnp.where(kc > tau, 1.0, jnp.where(real, eqf * jnp.where(pre <= need, 1.0, 0.0), 0.0))
                bias_ref[0, :, ds] = jnp.where(keep > 0.5, 0.0, NEG)
                return run + jnp.sum(eqf, axis=1, keepdims=True)

            lax.fori_loop(0, n_keys_pad // PAGE_SIZE, tie_body, jnp.zeros((n_t, 1), F32))


def _dsa_sample_select(page_table, idx_pool, lq, sgb, iknew, n_b, n_t, nsel):
    n_pages = page_table.shape[1]
    steps = n_pages // PAGES_PER_STEP
    n_keys_pad = (n_pages + 1) * PAGE_SIZE
    page_spec = lambda u: pl.BlockSpec(
        (1, PAGE_SIZE, D_IDX), lambda r, j, pt, u=u: (pt[r, j * PAGES_PER_STEP + u], 0, 0))
    per_req = lambda s: pl.BlockSpec((1,) + s, lambda r, j, pt: (r, 0, 0))
    return pl.pallas_call(
        functools.partial(_dsa_sample_select_kernel, n_t=n_t, n_keys_pad=n_keys_pad, steps=steps, nsel=nsel),
        grid_spec=pltpu.PrefetchScalarGridSpec(
            num_scalar_prefetch=1,
            grid=(n_b, steps),
            in_specs=[page_spec(u) for u in range(PAGES_PER_STEP)]
            + [per_req((H_IDX * n_t, D_IDX)), per_req((H_IDX * n_t, PAGE_SIZE)), per_req((PAGE_SIZE, D_IDX))],
            out_specs=per_req((n_t, n_keys_pad)),
            scratch_shapes=[pltpu.VMEM((n_t, n_keys_pad), jnp.int32)]),
        out_shape=jax.ShapeDtypeStruct((n_b, n_t, n_keys_pad), F32),
        compiler_params=_cparams(("parallel", "arbitrary")),
        name="dsa_sample_select",
    )(page_table, *([idx_pool] * PAGES_PER_STEP), lq, sgb, iknew)


def _dsa_sample_attend_kernel(pt_ref, *refs, n_t, steps):
    kp = refs[:PAGES_PER_STEP]
    vp = refs[PAGES_PER_STEP:2 * PAGES_PER_STEP]
    qbd_ref, bias_ref, knew_ref, vnew_ref, o_ref, m_ref, l_ref, acc_ref = refs[2 * PAGES_PER_STEP:]
    j = pl.program_id(1)
    rows = H_A * n_t

    @pl.when(j == 0)
    def _():
        m_ref[...] = jnp.full(m_ref.shape, NEG, F32)
        l_ref[...] = jnp.zeros(l_ref.shape, F32)
        acc_ref[...] = jnp.zeros(acc_ref.shape, F32)

    qbd = qbd_ref[0]

    def update(kblk, vblk, bias):
        n = kblk.shape[0]
        s = _dot_nt(qbd, kblk) + jnp.concatenate([bias] * H_A, axis=0)
        m_prev = m_ref[...]
        m_new = jnp.maximum(m_prev, jnp.max(s, axis=1, keepdims=True))
        alpha = jnp.exp(m_prev - m_new)
        p = jnp.exp(s - jnp.concatenate([m_new] * (n // LANE), axis=1))
        l_ref[...] = alpha * l_ref[...] + jnp.sum(p, axis=1, keepdims=True)
        pv = _dot(p.astype(BF16), vblk)
        diag = jnp.concatenate(
            [pv[h * n_t:(h + 1) * n_t, h * DH_A:(h + 1) * DH_A] for h in range(H_A)], axis=0)
        acc_ref[...] = alpha * acc_ref[...] + diag
        m_ref[...] = m_new

    kblk = jnp.concatenate([kp[u][0].astype(BF16) for u in range(PAGES_PER_STEP)], axis=0)
    vblk = jnp.concatenate([vp[u][0].astype(BF16) for u in range(PAGES_PER_STEP)], axis=0)
    width = PAGES_PER_STEP * PAGE_SIZE
    off = pl.multiple_of(j * width, width)
    update(kblk, vblk, bias_ref[0, :, pl.ds(off, width)])

    @pl.when(j == steps - 1)
    def _():
        past = steps * width
        update(knew_ref[0].astype(BF16), vnew_ref[0].astype(BF16), bias_ref[0, :, past:past + PAGE_SIZE])
        o = acc_ref[...] / l_ref[...]
        o_ref[0] = jnp.concatenate([o[h * n_t:(h + 1) * n_t] for h in range(H_A)], axis=1)


def _dsa_sample_attend(page_table, k_pool, v_pool, qbd, bias, knew, vnew, n_b, n_t):
    n_pages = page_table.shape[1]
    steps = n_pages // PAGES_PER_STEP
    n_keys_pad = (n_pages + 1) * PAGE_SIZE
    page_spec = lambda u: pl.BlockSpec(
        (1, PAGE_SIZE, W_A), lambda r, j, pt, u=u: (pt[r, j * PAGES_PER_STEP + u], 0, 0))
    per_req = lambda s: pl.BlockSpec((1,) + s, lambda r, j, pt: (r, 0, 0))
    rows = H_A * n_t
    return pl.pallas_call(
        functools.partial(_dsa_sample_attend_kernel, n_t=n_t, steps=steps),
        grid_spec=pltpu.PrefetchScalarGridSpec(
            num_scalar_prefetch=1,
            grid=(n_b, steps),
            in_specs=[page_spec(u) for u in range(PAGES_PER_STEP)] * 2
            + [per_req((rows, W_A)), per_req((n_t, n_keys_pad)),
               per_req((PAGE_SIZE, W_A)), per_req((PAGE_SIZE, W_A))],
            out_specs=per_req((n_t, W_A)),
            scratch_shapes=[pltpu.VMEM((rows, LANE), F32), pltpu.VMEM((rows, LANE), F32),
                            pltpu.VMEM((rows, DH_A), F32)]),
        out_shape=jax.ShapeDtypeStruct((n_b, n_t, W_A), F32),
        compiler_params=_cparams(("parallel", "arbitrary")),
        name="dsa_sample_attend",
    )(page_table, *([k_pool] * PAGES_PER_STEP), *([v_pool] * PAGES_PER_STEP), qbd, bias, knew, vnew)


def _reorder_w_in(w):
    o_ik = C_IQ + W_IDX
    o_iw = o_ik + D_IDX
    o_bq = o_iw + H_IDX
    main = jnp.concatenate([w[:, :o_ik], w[:, o_bq:]], axis=1).astype(BF16)
    small = jnp.concatenate(
        [w[:, o_ik:o_bq], jnp.zeros((D_MODEL, LANE - D_IDX - H_IDX), w.dtype)], axis=1).astype(BF16)
    return main, small


def kernel(x_prompt, x_sample, mem_prompt, cache_k, cache_v, cache_idx_k, state_hgrn, cache_mem_k,
           cache_mem_v, page_table, w_in, lb_logits, hgrn_norm_g, w_mem_k, w_mem_v, w_out, ln_g, ln_b):
    n_bp, t_p, _ = x_prompt.shape
    n_bs, t_s, _ = x_sample.shape
    n_pages = page_table.shape[1]
    past = n_pages * PAGE_SIZE
    l = 0

    lb_all = jnp.cumsum(jax.nn.softmax(lb_logits.astype(F32), axis=0), axis=0)
    lb = lb_all[l][None, :]
    g_norm = hgrn_norm_g[l][None, :]
    w_main, w_small = _reorder_w_in(w_in[l])
    w_o = w_out[l].astype(BF16)
    lg, lbias = ln_g[l][None, :], ln_b[l][None, :]

    xp = x_prompt.reshape(n_bp * t_p, D_MODEL)
    z, zs = _proj(xp, w_main, w_small, tm=512, tn=1024)
    pos_p = jnp.arange(t_p, dtype=jnp.int32)
    tabs = _rope_tables(pos_p, DH_A) + _rope_tables(pos_p, D_IDX)
    q, kf, kb, vf, vb, iqw, sgn, ikf, ike, iko = _prep(z, zs, tabs, tm=256)
    a_out = _dsa_prompt(q, kb, vb, iqw, sgn, ike, iko, n_bp, t_p, tq=256)
    s0 = jnp.zeros((n_bp, H_B, DK_B, DV_B), F32)
    bn, s_p = _hgrn(z, lb, g_norm, s0, n_bp, t_p, chunk=256, mm_dtype=BF16)
    w_mem = jnp.concatenate([w_mem_k[l], w_mem_v[l]], axis=1).astype(BF16)
    mkv, _ = _proj(mem_prompt.reshape(n_bp * N_MEM, D_MODEL), w_mem, w_small, tm=256, tn=1024)
    m_out = _mem_attend(z, mkv, 0, mkv, 1, n_bp, t_p, tq=512, mm_dtype=BF16)
    y_p = _merge(a_out, z, bn, m_out, xp, w_o, lg, lbias, tm=256)

    n_s = n_bs * t_s
    xs = x_sample.reshape(n_s, D_MODEL)
    z2, zs2 = _proj(xs, w_main, w_small, tm=n_s, tn=1024)
    pos_s = past + jnp.arange(t_s, dtype=jnp.int32)
    tabs_s = [jnp.tile(t, (n_bs, 1)) for t in _rope_tables(pos_s, DH_A) + _rope_tables(pos_s, D_IDX)]
    q2, kf2, kb2, vf2, vb2, iqw2, sgn2, ikf2, ike2, iko2 = _prep(z2, zs2, tabs_s, tm=n_s)
    nsel_s = min(TOPK_MAX, (past + t_s) // 4)
    lq = iqw2.reshape(n_bs, t_s, H_IDX, D_IDX).transpose(0, 2, 1, 3).reshape(n_bs, H_IDX * t_s, D_IDX)
    sgb = jnp.broadcast_to(
        sgn2[:, :H_IDX].reshape(n_bs, t_s, H_IDX).transpose(0, 2, 1).reshape(n_bs, H_IDX * t_s, 1),
        (n_bs, H_IDX * t_s, PAGE_SIZE))
    pad_rows = lambda a: jnp.pad(a.reshape(n_bs, t_s, -1), ((0, 0), (0, PAGE_SIZE - t_s), (0, 0)))
    bias = _dsa_sample_select(page_table, cache_idx_k[l], lq, sgb, pad_rows(ikf2), n_bs, t_s, nsel_s)
    q4 = q2.reshape(n_bs, t_s, H_A, DH_A)
    eye = jnp.eye(H_A, dtype=q2.dtype)
    qbd = (q4[:, None, :, :, :] * eye[None, :, None, :, None]).reshape(n_bs, H_A * t_s, W_A)
    n_phys = cache_k.shape[1]
    a_out2 = _dsa_sample_attend(page_table, cache_k[l].reshape(n_phys, PAGE_SIZE, W_A),
                                cache_v[l].reshape(n_phys, PAGE_SIZE, W_A), qbd, bias,
                                pad_rows(kf2), pad_rows(vf2), n_bs, t_s)
    bn2, s_s = _hgrn(z2, lb, g_norm, state_hgrn[l], n_bs, t_s, chunk=t_s, mm_dtype=F32)
    m_out2 = _mem_attend(z2, cache_mem_k[l].reshape(n_bs * N_MEM, W_M), 0,
                         cache_mem_v[l].reshape(n_bs * N_MEM, W_M), 0, n_bs, t_s, tq=t_s, mm_dtype=F32)
    y_s = _merge(a_out2.reshape(n_s, W_A), z2, bn2, m_out2, xs, w_o, lg, lbias, tm=n_s)

    st = lambda a: a[None]
    return (y_p.reshape(n_bp, t_p, D_MODEL), y_s.reshape(n_bs, t_s, D_MODEL),
            st(kf.reshape(n_bp, t_p, H_A, DH_A)), st(vf.reshape(n_bp, t_p, H_A, DH_A)),
            st(ikf.reshape(n_bp, t_p, D_IDX)), st(s_p),
            st(mkv[:, :W_M].reshape(n_bp, N_MEM, H_M, DH_M)), st(mkv[:, W_M:].reshape(n_bp, N_MEM, H_M, DH_M)),
            st(kf2.reshape(n_bs, t_s, H_A, DH_A)), st(vf2.reshape(n_bs, t_s, H_A, DH_A)),
            st(ikf2.reshape(n_bs, t_s, D_IDX)), st(s_s))
```

```python
import functools

import numpy as np
import jax
import jax.numpy as jnp
from jax import lax
from jax.experimental import pallas as pl
from jax.experimental.pallas import tpu as pltpu

D_MODEL = 2048
DH_A = 128
H_A = 8
W_A = H_A * DH_A
H_IDX = 16
D_IDX = 64
W_IDX = H_IDX * D_IDX
TOPK_MAX = 256
DK_B = 128
DV_B = 128
H_B = 4
W_B = H_B * DV_B
DH_M = 128
H_M = 4
W_M = H_M * DH_M
N_MEM = 256
PAGE_SIZE = 128
ROPE_THETA = 500000.0
ROPE_FRAC = 4
LN_EPS = 1e-5
RMS_EPS = 1e-6
DEPTH = 1
ALPHA = (2.0 * DEPTH) ** 0.25

LANE = 128
VMEM_LIMIT = 48 * 1024 * 1024

C_AQ, C_AK, C_AV, C_AG, C_IQ = 0, 1024, 2048, 3072, 4096
C_BQ, C_BF, C_BI, C_BG, C_MQ, C_MG = 5120, 5632, 6144, 6656, 7168, 7680
N_MAIN = 8192

NEG = -1e30
INT_MIN = -(2 ** 31)
BF16 = jnp.bfloat16
F32 = jnp.float32


def _dot(a, b, **kw):
    return jnp.dot(a, b, preferred_element_type=F32, **kw)


def _dot_nt(a, b):
    return lax.dot_general(a, b, (((1,), (1,)), ((), ())), preferred_element_type=F32)


def _dot_tn(a, b):
    return lax.dot_general(a, b, (((0,), (0,)), ((), ())), preferred_element_type=F32)


def _sigmoid(x):
    return 1.0 / (1.0 + jnp.exp(-x))


def _silu(x):
    return x * _sigmoid(x)


def _cparams(sem):
    return pltpu.CompilerParams(dimension_semantics=sem, vmem_limit_bytes=VMEM_LIMIT)


def _proj_kernel(x_ref, w_ref, ws_ref, o_ref, os_ref, xb_ref):
    j = pl.program_id(1)

    @pl.when(j == 0)
    def _():
        xb = x_ref[...].astype(BF16)
        xb_ref[...] = xb
        os_ref[...] = _dot(xb, ws_ref[...])

    o_ref[...] = _dot(xb_ref[...], w_ref[...])


def _proj(x, w_main, w_small, tm, tn):
    m = x.shape[0]
    n = w_main.shape[1]
    return pl.pallas_call(
        _proj_kernel,
        grid=(m // tm, n // tn),
        in_specs=[pl.BlockSpec((tm, D_MODEL), lambda i, j: (i, 0)),
                  pl.BlockSpec((D_MODEL, tn), lambda i, j: (0, j)),
                  pl.BlockSpec((D_MODEL, LANE), lambda i, j: (0, 0))],
        out_specs=[pl.BlockSpec((tm, tn), lambda i, j: (i, j)),
                   pl.BlockSpec((tm, LANE), lambda i, j: (i, 0))],
        out_shape=[jax.ShapeDtypeStruct((m, n), F32),
                   jax.ShapeDtypeStruct((m, LANE), F32)],
        scratch_shapes=[pltpu.VMEM((tm, D_MODEL), BF16)],
        compiler_params=_cparams(("parallel", "arbitrary")),
        name="in_proj",
    )(x, w_main, w_small)


def _rope(x, c, sa, sb, half):
    return x * c + pltpu.roll(x, LANE - half, 1) * sa + pltpu.roll(x, half, 1) * sb


def _prep_kernel(aq_ref, ak_ref, av_ref, iq_ref, zs_ref,
                 c1_ref, sa1_ref, sb1_ref, c2_ref, sa2_ref, sb2_ref,
                 q_ref, kf_ref, kb_ref, vf_ref, vb_ref, iqw_ref, sgn_ref,
                 ikf_ref, ike_ref, iko_ref):
    c1, sa1, sb1 = c1_ref[...], sa1_ref[...], sb1_ref[...]
    c2, sa2, sb2 = c2_ref[...], sa2_ref[...], sb2_ref[...]
    half_a = DH_A // ROPE_FRAC // 2
    half_i = D_IDX // ROPE_FRAC // 2
    for h in range(H_A):
        sl = slice(h * DH_A, (h + 1) * DH_A)
        q_ref[:, sl] = (_rope(aq_ref[:, sl], c1, sa1, sb1, half_a) * (DH_A ** -0.5)).astype(BF16)
        kr = _rope(ak_ref[:, sl], c1, sa1, sb1, half_a)
        kf_ref[:, h, :] = kr
        kb_ref[:, sl] = kr.astype(BF16)
        v = av_ref[:, sl]
        vf_ref[:, h, :] = v
        vb_ref[:, sl] = v.astype(BF16)

    zs = zs_ref[...]
    lane = lax.broadcasted_iota(jnp.int32, zs.shape, 1)
    ikr = _rope(zs, c2, sa2, sb2, half_i)
    ikf_ref[...] = ikr[:, :D_IDX]
    ike_ref[...] = jnp.where(lane < D_IDX, ikr, 0.0).astype(BF16)
    iko_ref[...] = jnp.where(lane >= D_IDX, pltpu.roll(ikr, D_IDX, 1), 0.0).astype(BF16)
    sgn_ref[...] = jnp.sign(pltpu.roll(zs, LANE - D_IDX, 1))
    wscale = (D_IDX ** -0.5) * (H_IDX ** -0.5)
    for p in range(H_IDX // 2):
        sl = slice(p * LANE, (p + 1) * LANE)
        wa = jnp.abs(zs[:, D_IDX + 2 * p:D_IDX + 2 * p + 1])
        wb = jnp.abs(zs[:, D_IDX + 2 * p + 1:D_IDX + 2 * p + 2])
        wpair = jnp.where(lane < D_IDX, wa, wb) * wscale
        iqw_ref[:, sl] = (_rope(iq_ref[:, sl], c2, sa2, sb2, half_i) * wpair).astype(BF16)


def _prep(z, zs, tabs, tm):
    m = z.shape[0]
    period = tabs[0].shape[0] // tm
    zblk = lambda c: pl.BlockSpec((tm, 1024), lambda i, c=c: (i, c // 1024))
    tab = pl.BlockSpec((tm, LANE), lambda i: (i % period, 0))
    row = lambda w: pl.BlockSpec((tm, w), lambda i: (i, 0))
    sds = lambda w, dt: jax.ShapeDtypeStruct((m, w), dt)
    heads = pl.BlockSpec((tm, H_A, DH_A), lambda i: (i, 0, 0))
    heads_sds = jax.ShapeDtypeStruct((m, H_A, DH_A), F32)
    return pl.pallas_call(
        _prep_kernel,
        grid=(m // tm,),
        in_specs=[zblk(C_AQ), zblk(C_AK), zblk(C_AV), zblk(C_IQ), row(LANE)] + [tab] * 6,
        out_specs=[row(W_A), heads, row(W_A), heads, row(W_A), row(W_IDX), row(LANE),
                   row(D_IDX), row(LANE), row(LANE)],
        out_shape=[sds(W_A, BF16), heads_sds, sds(W_A, BF16), heads_sds, sds(W_A, BF16),
                   sds(W_IDX, BF16), sds(LANE, F32), sds(D_IDX, F32), sds(LANE, BF16), sds(LANE, BF16)],
        compiler_params=_cparams(("parallel",)),
        name="rope_prep",
    )(z, z, z, z, zs, *tabs)


def _rope_tables(pos, d):
    r = d // ROPE_FRAC
    half = r // 2
    n = pos.shape[0]
    inv = ROPE_THETA ** (-jnp.arange(half, dtype=F32) / half)
    ang = pos.astype(F32)[:, None] * inv[None, :]
    cos, sin = jnp.cos(ang), jnp.sin(ang)
    zh = jnp.zeros((n, half), F32)
    c = jnp.concatenate([cos, cos, jnp.ones((n, d - r), F32)], axis=1)
    sa = jnp.concatenate([-sin, zh, jnp.zeros((n, d - r), F32)], axis=1)
    sb = jnp.concatenate([zh, sin, jnp.zeros((n, d - r), F32)], axis=1)
    reps = LANE // d
    return [jnp.tile(t, (1, reps)) for t in (c, sa, sb)]


def _mono_key(score):
    b = lax.bitcast_convert_type(score, jnp.int32)
    return b ^ (lax.shift_right_arithmetic(b, 31) & jnp.int32(0x7FFFFFFF))


def _lane_total(part):
    return _dot(part.astype(BF16), jnp.ones((LANE, LANE), BF16))


def _radix_select(count_ge, rows, total, nsel):
    def body(bi, carry):
        pu, cacc = carry
        bit = jnp.int32(31) - bi
        cand_u = pu | lax.shift_left(jnp.int32(1), bit)
        cnt = count_ge(cand_u ^ jnp.int32(INT_MIN))
        ok = cnt >= nsel
        return jnp.where(ok, cand_u, pu), jnp.where(ok, cnt, cacc)

    pu0 = jnp.zeros((rows, LANE), jnp.int32)
    c0 = jnp.full((rows, LANE), total, F32)
    pu, cacc = lax.fori_loop(0, 32, body, (pu0, c0))
    return pu ^ jnp.int32(INT_MIN), cacc


def _dsa_prompt_kernel(q_ref, k_ref, v_ref, iq_ref, sgn_ref, ike_ref, iko_ref, o_ref,
                       sgnb_ref, key_ref, bias_ref, m_ref, l_ref, acc_ref, *, tq, nsel):
    tk = tq
    i = pl.program_id(1)
    nch = i + 1
    reps = tk // LANE

    sgn = sgn_ref[...]
    for h in range(H_IDX):
        sgnb_ref[h] = jnp.broadcast_to(sgn[:, h:h + 1], (tq, tk))

    rows = i * tq + lax.broadcasted_iota(jnp.int32, (tq, tk), 0)
    cols0 = lax.broadcasted_iota(jnp.int32, (tq, tk), 1)

    def chunk_slice(c):
        return pl.ds(pl.multiple_of(c * tk, tk), tk)

    def score_body(c, carry):
        ds = chunk_slice(c)
        ke = ike_ref[ds, :]
        ko = iko_ref[ds, :]
        acc = jnp.zeros((tq, tk), F32)
        for h in range(H_IDX):
            lhs = iq_ref[:, (h // 2) * LANE:(h // 2 + 1) * LANE]
            d = _dot_nt(lhs, ke if h % 2 == 0 else ko)
            acc = acc + sgnb_ref[h] * jnp.maximum(d, 0.0)
        key = _mono_key(acc)
        key_ref[:, ds] = jnp.where(cols0 + c * tk <= rows, key, jnp.int32(INT_MIN))
        return carry

    lax.fori_loop(0, nch, score_body, 0)

    def count(cmp_fn):
        def body(c, part):
            kc = key_ref[:, chunk_slice(c)]
            for r_ in range(reps):
                part = part + jnp.where(cmp_fn(kc[:, r_ * LANE:(r_ + 1) * LANE]), 1.0, 0.0)
            return part
        return _lane_total(lax.fori_loop(0, nch, body, jnp.zeros((tq, LANE), F32)))

    total = (nch * tk).astype(F32)
    tau_r, c_ge_r = _radix_select(lambda cand: count(lambda kc: kc >= cand), tq, total, float(nsel))
    tau_eff = jnp.concatenate([jnp.maximum(tau_r, jnp.int32(INT_MIN + 1))] * reps, axis=1)

    def bias_body(c, carry):
        ds = chunk_slice(c)
        bias_ref[:, ds] = jnp.where(key_ref[:, ds] >= tau_eff, 0.0, NEG)
        return carry

    lax.fori_loop(0, nch, bias_body, 0)

    tie_rows = jnp.logical_and(c_ge_r > float(nsel), tau_r > jnp.int32(INT_MIN))
    has_tie = jnp.max(jnp.where(tie_rows, 1.0, 0.0)) > 0.5

    @pl.when(has_tie)
    def _():
        tau = tau_r[:, 0:1]
        c_gt = count(lambda kc: kc > tau_r)[:, 0:1]
        need = float(nsel) - c_gt
        upper = jnp.where(lax.broadcasted_iota(jnp.int32, (tk, tk), 0) <= cols0[:tk, :], 1.0, 0.0).astype(BF16)
        real = tau > jnp.int32(INT_MIN)

        def tie_body(c, run):
            ds = chunk_slice(c)
            kc = key_ref[:, ds]
            eq = kc == tau
            eqf = jnp.where(eq, 1.0, 0.0)
            pre = _dot(eqf.astype(BF16), upper) + run
            keep = jnp.where(kc > tau, 1.0, jnp.where(real, eqf * jnp.where(pre <= need, 1.0, 0.0), 0.0))
            bias_ref[:, ds] = jnp.where(keep > 0.5, 0.0, NEG)
            return run + jnp.sum(eqf, axis=1, keepdims=True)

        lax.fori_loop(0, nch, tie_body, jnp.zeros((tq, 1), F32))

    m_ref[...] = jnp.full(m_ref.shape, NEG, F32)
    l_ref[...] = jnp.zeros(l_ref.shape, F32)
    acc_ref[...] = jnp.zeros(acc_ref.shape, F32)

    def attn_body(c, carry):
        ds = chunk_slice(c)
        bias = bias_ref[:, ds]
        for h in range(H_A):
            hs = slice(h * DH_A, (h + 1) * DH_A)
            s = _dot_nt(q_ref[:, hs], k_ref[ds, hs]) + bias
            m_prev = m_ref[h]
            m_new = jnp.maximum(m_prev, jnp.max(s, axis=1, keepdims=True))
            alpha = jnp.exp(m_prev - m_new)
            p = jnp.exp(s - jnp.concatenate([m_new] * reps, axis=1))
            l_ref[h] = alpha * l_ref[h] + jnp.sum(p, axis=1, keepdims=True)
            acc_ref[h] = alpha * acc_ref[h] + _dot(p.astype(BF16), v_ref[ds, hs])
            m_ref[h] = m_new
        return carry

    lax.fori_loop(0, nch, attn_body, 0)
    for h in range(H_A):
        o_ref[:, h * DH_A:(h + 1) * DH_A] = acc_ref[h] / l_ref[h]


def _dsa_prompt(q, k, v, iqw, sgn, ike, iko, n_b, n_t, tq):
    nsel = min(TOPK_MAX, n_t // 4)
    nq = n_t // tq
    qblk = lambda w: pl.BlockSpec((tq, w), lambda b, i: (b * nq + i, 0))
    full = lambda w: pl.BlockSpec((n_t, w), lambda b, i: (b, 0))
    return pl.pallas_call(
        functools.partial(_dsa_prompt_kernel, tq=tq, nsel=nsel),
        grid=(n_b, nq),
        in_specs=[qblk(W_A), full(W_A), full(W_A), qblk(W_IDX), qblk(LANE), full(LANE), full(LANE)],
        out_specs=qblk(W_A),
        out_shape=jax.ShapeDtypeStruct((n_b * n_t, W_A), F32),
        scratch_shapes=[pltpu.VMEM((H_IDX, tq, tq), F32),
                        pltpu.VMEM((tq, n_t), jnp.int32),
                        pltpu.VMEM((tq, n_t), F32),
                        pltpu.VMEM((H_A, tq, LANE), F32),
                        pltpu.VMEM((H_A, tq, LANE), F32),
                        pltpu.VMEM((H_A, tq, DH_A), F32)],
        compiler_params=_cparams(("parallel", "arbitrary")),
        name="dsa_prompt",
    )(q, k, v, iqw, sgn, ike, iko)


def _hgrn_kernel(bq_ref, bf_ref, bi_ref, lb_ref, g_ref, s0_ref, o_ref, s_out_ref, st_ref,
                 *, n_t, chunk, mm_dtype):
    c_ = chunk
    lb = lb_ref[...]
    g = g_ref[...]
    st_ref[...] = s0_ref[0, 0, 0].T
    row = lax.broadcasted_iota(jnp.int32, (c_, c_), 0)
    col = lax.broadcasted_iota(jnp.int32, (c_, c_), 1)
    lower = jnp.where(col <= row, 1.0, 0.0)
    levels = []
    m_ = 8
    while m_ < c_:
        levels.append(m_)
        m_ *= 2

    def chunk_body(ci, carry):
        ds = pl.ds(pl.multiple_of(ci * c_, c_), c_)
        bq = bq_ref[ds, :]
        qs = bq * _sigmoid(bq)
        f = lb + (1.0 - lb) * _sigmoid(bf_ref[ds, :])
        kk = 1.0 - f
        v = bi_ref[ds, :]
        gcum = _dot(lower, jnp.log(f), precision=lax.Precision.HIGHEST)
        a = jnp.zeros((c_, c_), F32)
        for m in levels:
            sh = int(np.log2(2 * m))
            g3 = gcum.reshape(c_ // (2 * m), 2 * m, DK_B)
            ref = jnp.broadcast_to(g3[:, m - 1:m, :], g3.shape).reshape(c_, DK_B)
            qh = qs * jnp.exp(jnp.minimum(gcum - ref, 0.0))
            kh = kk * jnp.exp(jnp.minimum(ref - gcum, 0.0))
            am = _dot_nt(qh.astype(mm_dtype), kh.astype(mm_dtype))
            same = (row >> sh) == (col >> sh)
            lo_hi = jnp.logical_and((row & (2 * m - 1)) >= m, (col & (2 * m - 1)) < m)
            a = a + jnp.where(jnp.logical_and(same, lo_hi), am, 0.0)
        for j in range(8):
            kj = kk if j == 0 else pltpu.roll(kk, j, 0)
            gj = gcum if j == 0 else pltpu.roll(gcum, j, 0)
            e = jnp.exp(jnp.minimum(gcum - gj, 0.0))
            aj = jnp.sum(qs * kj * e, axis=1, keepdims=True)
            on_diag = jnp.logical_and(row - col == j, (row & 7) >= j)
            a = a + jnp.where(on_diag, aj, 0.0)
        st = st_ref[...]
        o = (_dot_nt((qs * jnp.exp(gcum)).astype(mm_dtype), st.astype(mm_dtype))
             + _dot(a.astype(mm_dtype), v.astype(mm_dtype)))
        glast = gcum[c_ - 1:c_, :]
        kd = kk * jnp.exp(glast - gcum)
        st_ref[...] = jnp.exp(glast) * st + _dot_tn(v.astype(mm_dtype), kd.astype(mm_dtype))
        o_ref[ds, :] = o * lax.rsqrt(jnp.mean(o * o, axis=1, keepdims=True) + RMS_EPS) * g
        return carry

    lax.fori_loop(0, n_t // c_, chunk_body, 0)
    s_out_ref[0, 0, 0] = st_ref[...].T


def _hgrn(z, lb, g, s0, layer, n_b, n_t, chunk, mm_dtype):
    zcol = lambda c: pl.BlockSpec((n_t, DK_B), lambda b, h, c=c: (b, c // DK_B + h))
    sblk = lambda lyr: pl.BlockSpec((1, 1, 1, DK_B, DV_B), lambda b, h: (lyr, b, h, 0, 0))
    return pl.pallas_call(
        functools.partial(_hgrn_kernel, n_t=n_t, chunk=chunk, mm_dtype=mm_dtype),
        grid=(n_b, H_B),
        in_specs=[zcol(C_BQ), zcol(C_BF), zcol(C_BI),
                  pl.BlockSpec((1, DK_B), lambda b, h: (0, h)),
                  pl.BlockSpec((1, DV_B), lambda b, h: (0, 0)),
                  sblk(layer)],
        out_specs=[pl.BlockSpec((n_t, DV_B), lambda b, h: (b, h)), sblk(0)],
        out_shape=[jax.ShapeDtypeStruct((n_b * n_t, W_B), F32),
                   jax.ShapeDtypeStruct((1, n_b, H_B, DK_B, DV_B), F32)],
        scratch_shapes=[pltpu.VMEM((DV_B, DK_B), F32)],
        compiler_params=_cparams(("parallel", "arbitrary")),
        name="hgrn2",
    )(z, z, z, lb, g, s0)


def _mem_kv_kernel(x_ref, wk_ref, wv_ref, mk_ref, mv_ref):
    x = x_ref[0].astype(BF16)
    mk = _dot(x, wk_ref[...])
    mv = _dot(x, wv_ref[...])
    for h in range(H_M):
        hs = slice(h * DH_M, (h + 1) * DH_M)
        mk_ref[0, 0, :, h, :] = mk[:, hs]
        mv_ref[0, 0, :, h, :] = mv[:, hs]


def _mem_kv(mem, wk, wv):
    n_b = mem.shape[0]
    wspec = pl.BlockSpec((D_MODEL, W_M), lambda b: (0, 0))
    ospec = pl.BlockSpec((1, 1, N_MEM, H_M, DH_M), lambda b: (0, b, 0, 0, 0))
    sds = jax.ShapeDtypeStruct((1, n_b, N_MEM, H_M, DH_M), F32)
    return pl.pallas_call(
        _mem_kv_kernel,
        grid=(n_b,),
        in_specs=[pl.BlockSpec((1, N_MEM, D_MODEL), lambda b: (b, 0, 0)), wspec, wspec],
        out_specs=[ospec, ospec],
        out_shape=[sds, sds],
        compiler_params=_cparams(("parallel",)),
        name="mem_kv",
    )(mem, wk, wv)


def _mem_kernel(q_ref, mk_ref, mv_ref, o_ref, *, mm_dtype):
    for h in range(H_M):
        hs = slice(h * DH_M, (h + 1) * DH_M)
        q = (q_ref[:, hs] * (DH_M ** -0.5)).astype(mm_dtype)
        s = _dot_nt(q, mk_ref[0, 0, :, h, :].astype(mm_dtype))
        p = jnp.exp(s - jnp.max(s, axis=1, keepdims=True))
        l = jnp.sum(p, axis=1, keepdims=True)
        o_ref[:, hs] = _dot(p.astype(mm_dtype), mv_ref[0, 0, :, h, :].astype(mm_dtype)) / l


def _mem_attend(z, mk, mv, layer, n_b, n_t, tq, mm_dtype):
    nq = n_t // tq
    kv = pl.BlockSpec((1, 1, N_MEM, H_M, DH_M), lambda b, i: (layer, b, 0, 0, 0))
    return pl.pallas_call(
        functools.partial(_mem_kernel, mm_dtype=mm_dtype),
        grid=(n_b, nq),
        in_specs=[pl.BlockSpec((tq, W_M), lambda b, i: (b * nq + i, C_MQ // W_M)), kv, kv],
        out_specs=pl.BlockSpec((tq, W_M), lambda b, i: (b * nq + i, 0)),
        out_shape=jax.ShapeDtypeStruct((n_b * n_t, W_M), F32),
        compiler_params=_cparams(("parallel", "parallel")),
        name="mem_attend",
    )(z, mk, mv)


def _merge_kernel(a_ref, ag_ref, bn_ref, bg_ref, mo_ref, mg_ref, h_ref, wo_ref, lg_ref, lbias_ref, o_ref):
    a = (a_ref[...] * _silu(ag_ref[...])).astype(BF16)
    b = (bn_ref[...] * _silu(bg_ref[...])).astype(BF16)
    m = (mo_ref[...] * _silu(mg_ref[...])).astype(BF16)
    y = (_dot(a, wo_ref[0:W_A, :]) + _dot(b, wo_ref[W_A:W_A + W_B, :])
         + _dot(m, wo_ref[W_A + W_B:D_MODEL, :]))
    r = ALPHA * h_ref[...] + y
    xc = r - jnp.mean(r, axis=1, keepdims=True)
    var = jnp.mean(xc * xc, axis=1, keepdims=True)
    o_ref[...] = xc * lax.rsqrt(var + LN_EPS) * lg_ref[...] + lbias_ref[...]


def _merge(a_out, z, bn, m_out, h, w_out, ln_g, ln_b, tm):
    m = h.shape[0]
    row = lambda w: pl.BlockSpec((tm, w), lambda i: (i, 0))
    zcol = lambda c, w: pl.BlockSpec((tm, w), lambda i, c=c, w=w: (i, c // w))
    const = lambda s: pl.BlockSpec(s, lambda i: (0, 0))
    return pl.pallas_call(
        _merge_kernel,
        grid=(m // tm,),
        in_specs=[row(W_A), zcol(C_AG, W_A), row(W_B), zcol(C_BG, W_B), row(W_M), zcol(C_MG, W_M),
                  row(D_MODEL), const((D_MODEL, D_MODEL)), const((1, D_MODEL)), const((1, D_MODEL))],
        out_specs=row(D_MODEL),
        out_shape=jax.ShapeDtypeStruct((m, D_MODEL), F32),
        compiler_params=_cparams(("parallel",)),
        name="merge",
    )(a_out, z, bn, z, m_out, z, h, w_out, ln_g, ln_b)


PAGES_PER_STEP = 8


def _dsa_sample_select_kernel(pt_ref, *refs, n_t, n_keys_pad, steps, nsel):
    pages = refs[:PAGES_PER_STEP]
    lq_ref, sg_ref, iknew_ref, bias_ref, key_ref = refs[PAGES_PER_STEP:]
    j = pl.program_id(1)
    lq = lq_ref[0]
    sg = sg_ref[0]

    def score(ikp_t):
        d = _dot(lq, ikp_t.astype(BF16))
        r = jnp.maximum(d, 0.0) * sg
        s = r[0:n_t]
        for h in range(1, H_IDX):
            s = s + r[h * n_t:(h + 1) * n_t]
        return s

    for u in range(PAGES_PER_STEP):
        off = pl.multiple_of((j * PAGES_PER_STEP + u) * PAGE_SIZE, PAGE_SIZE)
        key_ref[:, pl.ds(off, PAGE_SIZE)] = _mono_key(score(pages[u][0, 0]))

    @pl.when(j == steps - 1)
    def _():
        past = n_keys_pad - PAGE_SIZE
        s = score(iknew_ref[0])
        t_i = lax.broadcasted_iota(jnp.int32, (n_t, PAGE_SIZE), 0)
        k_i = lax.broadcasted_iota(jnp.int32, (n_t, PAGE_SIZE), 1)
        key_ref[:, past:n_keys_pad] = jnp.where(k_i <= t_i, _mono_key(s), jnp.int32(INT_MIN))

        keys = key_ref[...]

        n_tiles = n_keys_pad // LANE

        def count_ge(cand):
            part = jnp.zeros((n_t, LANE), F32)
            for c in range(n_tiles):
                part = part + jnp.where(keys[:, c * LANE:(c + 1) * LANE] >= cand, 1.0, 0.0)
            return _lane_total(part)

        tau_r, c_ge_r = _radix_select(count_ge, n_t, float(n_keys_pad), float(nsel))
        tau = tau_r[:, 0:1]
        tau_eff = jnp.maximum(tau, jnp.int32(INT_MIN + 1))
        bias_ref[0] = jnp.where(keys >= tau_eff, 0.0, NEG)

        tie_rows = jnp.logical_and(c_ge_r > float(nsel), tau_r > jnp.int32(INT_MIN))
        has_tie = jnp.max(jnp.where(tie_rows, 1.0, 0.0)) > 0.5

        @pl.when(has_tie)
        def _():
            c_gt = jnp.sum(jnp.where(keys > tau, 1.0, 0.0), axis=1, keepdims=True)
            need = float(nsel) - c_gt
            ri = lax.broadcasted_iota(jnp.int32, (PAGE_SIZE, PAGE_SIZE), 0)
            ci = lax.broadcasted_iota(jnp.int32, (PAGE_SIZE, PAGE_SIZE), 1)
            upper = jnp.where(ri <= ci, 1.0, 0.0)
            real = tau > jnp.int32(INT_MIN)

            def tie_body(c, run):
                ds = pl.ds(pl.multiple_of(c * PAGE_SIZE, PAGE_SIZE), PAGE_SIZE)
                kc = key_ref[:, ds]
                eqf = jnp.where(kc == tau, 1.0, 0.0)
                pre = _dot(eqf, upper) + run
                keep = jnp.where(kc > tau, 1.0, jnp.where(real, eqf * jnp.where(pre <= need, 1.0, 0.0), 0.0))
                bias_ref[0, :, ds] = jnp.where(keep > 0.5, 0.0, NEG)
                return run + jnp.sum(eqf, axis=1, keepdims=True)

            lax.fori_loop(0, n_keys_pad // PAGE_SIZE, tie_body, jnp.zeros((n_t, 1), F32))


def _dsa_sample_select(page_table, idx_pool, layer, lq, sgb, iknew, n_b, n_t, nsel):
    n_pages = page_table.shape[1]
    steps = n_pages // PAGES_PER_STEP
    n_keys_pad = (n_pages + 1) * PAGE_SIZE
    page_spec = lambda u: pl.BlockSpec(
        (1, 1, D_IDX, PAGE_SIZE), lambda r, j, pt, u=u: (layer, pt[r, j * PAGES_PER_STEP + u], 0, 0))
    per_req = lambda s: pl.BlockSpec((1,) + s, lambda r, j, pt: (r, 0, 0))
    return pl.pallas_call(
        functools.partial(_dsa_sample_select_kernel, n_t=n_t, n_keys_pad=n_keys_pad, steps=steps, nsel=nsel),
        grid_spec=pltpu.PrefetchScalarGridSpec(
            num_scalar_prefetch=1,
            grid=(n_b, steps),
            in_specs=[page_spec(u) for u in range(PAGES_PER_STEP)]
            + [per_req((H_IDX * n_t, D_IDX)), per_req((H_IDX * n_t, PAGE_SIZE)), per_req((D_IDX, PAGE_SIZE))],
            out_specs=per_req((n_t, n_keys_pad)),
            scratch_shapes=[pltpu.VMEM((n_t, n_keys_pad), jnp.int32)]),
        out_shape=jax.ShapeDtypeStruct((n_b, n_t, n_keys_pad), F32),
        compiler_params=_cparams(("parallel", "arbitrary")),
        name="dsa_sample_select",
    )(page_table, *([idx_pool] * PAGES_PER_STEP), lq, sgb, iknew)


def _dsa_sample_attend_kernel(pt_ref, *refs, n_t, steps):
    kp = refs[:PAGES_PER_STEP]
    vp = refs[PAGES_PER_STEP:2 * PAGES_PER_STEP]
    qbd_ref, bias_ref, knew_ref, vnew_ref, o_ref, m_ref, l_ref, acc_ref = refs[2 * PAGES_PER_STEP:]
    j = pl.program_id(1)
    rows = H_A * n_t

    @pl.when(j == 0)
    def _():
        m_ref[...] = jnp.full(m_ref.shape, NEG, F32)
        l_ref[...] = jnp.zeros(l_ref.shape, F32)
        acc_ref[...] = jnp.zeros(acc_ref.shape, F32)

    qbd = qbd_ref[0]

    def update(kblk, vblk, bias):
        n = kblk.shape[0]
        s = _dot_nt(qbd, kblk) + jnp.concatenate([bias] * H_A, axis=0)
        m_prev = m_ref[...]
        m_new = jnp.maximum(m_prev, jnp.max(s, axis=1, keepdims=True))
        alpha = jnp.exp(m_prev - m_new)
        p = jnp.exp(s - jnp.concatenate([m_new] * (n // LANE), axis=1))
        l_ref[...] = alpha * l_ref[...] + jnp.sum(p, axis=1, keepdims=True)
        pv = _dot(p.astype(BF16), vblk)
        diag = jnp.concatenate(
            [pv[h * n_t:(h + 1) * n_t, h * DH_A:(h + 1) * DH_A] for h in range(H_A)], axis=0)
        acc_ref[...] = alpha * acc_ref[...] + diag
        m_ref[...] = m_new

    def page2d(ref):
        return jnp.concatenate([ref[0, 0, :, h, :] for h in range(H_A)], axis=1).astype(BF16)

    kblk = jnp.concatenate([page2d(kp[u]) for u in range(PAGES_PER_STEP)], axis=0)
    vblk = jnp.concatenate([page2d(vp[u]) for u in range(PAGES_PER_STEP)], axis=0)
    width = PAGES_PER_STEP * PAGE_SIZE
    off = pl.multiple_of(j * width, width)
    update(kblk, vblk, bias_ref[0, :, pl.ds(off, width)])

    @pl.when(j == steps - 1)
    def _():
        past = steps * width
        update(page2d(knew_ref), page2d(vnew_ref), bias_ref[0, :, past:past + PAGE_SIZE])
        o = acc_ref[...] / l_ref[...]
        o_ref[0] = jnp.concatenate([o[h * n_t:(h + 1) * n_t] for h in range(H_A)], axis=1)


def _dsa_sample_attend(page_table, k_pool, v_pool, layer, qbd, bias, knew, vnew, n_b, n_t):
    n_pages = page_table.shape[1]
    steps = n_pages // PAGES_PER_STEP
    n_keys_pad = (n_pages + 1) * PAGE_SIZE
    page_spec = lambda u: pl.BlockSpec(
        (1, 1, PAGE_SIZE, H_A, DH_A), lambda r, j, pt, u=u: (layer, pt[r, j * PAGES_PER_STEP + u], 0, 0, 0))
    per_req = lambda s: pl.BlockSpec((1,) + s, lambda r, j, pt: (r, 0, 0))
    rows = H_A * n_t
    return pl.pallas_call(
        functools.partial(_dsa_sample_attend_kernel, n_t=n_t, steps=steps),
        grid_spec=pltpu.PrefetchScalarGridSpec(
            num_scalar_prefetch=1,
            grid=(n_b, steps),
            in_specs=[page_spec(u) for u in range(PAGES_PER_STEP)] * 2
            + [per_req((rows, W_A)), per_req((n_t, n_keys_pad))]
            + [pl.BlockSpec((1, 1, PAGE_SIZE, H_A, DH_A), lambda r, j, pt: (r, 0, 0, 0, 0))] * 2,
            out_specs=per_req((n_t, W_A)),
            scratch_shapes=[pltpu.VMEM((rows, LANE), F32), pltpu.VMEM((rows, LANE), F32),
                            pltpu.VMEM((rows, DH_A), F32)]),
        out_shape=jax.ShapeDtypeStruct((n_b, n_t, W_A), F32),
        compiler_params=_cparams(("parallel", "arbitrary")),
        name="dsa_sample_attend",
    )(page_table, *([k_pool] * PAGES_PER_STEP), *([v_pool] * PAGES_PER_STEP), qbd, bias, knew, vnew)


def _reorder_w_in(w):
    o_ik = C_IQ + W_IDX
    o_iw = o_ik + D_IDX
    o_bq = o_iw + H_IDX
    main = jnp.concatenate([w[:, :o_ik], w[:, o_bq:]], axis=1).astype(BF16)
    small = jnp.concatenate(
        [w[:, o_ik:o_bq], jnp.zeros((D_MODEL, LANE - D_IDX - H_IDX), w.dtype)], axis=1).astype(BF16)
    return main, small


def kernel(x_prompt, x_sample, mem_prompt, cache_k, cache_v, cache_idx_k, state_hgrn, cache_mem_k,
           cache_mem_v, page_table, w_in, lb_logits, hgrn_norm_g, w_mem_k, w_mem_v, w_out, ln_g, ln_b):
    n_bp, t_p, _ = x_prompt.shape
    n_bs, t_s, _ = x_sample.shape
    n_pages = page_table.shape[1]
    past = n_pages * PAGE_SIZE
    l = 0

    lb_all = jnp.cumsum(jax.nn.softmax(lb_logits.astype(F32), axis=0), axis=0)
    lb = lb_all[l][None, :]
    g_norm = hgrn_norm_g[l][None, :]
    w_main, w_small = _reorder_w_in(w_in[l])
    w_o = w_out[l].astype(BF16)
    lg, lbias = ln_g[l][None, :], ln_b[l][None, :]

    xp = x_prompt.reshape(n_bp * t_p, D_MODEL)
    z, zs = _proj(xp, w_main, w_small, tm=1024, tn=1024)
    pos_p = jnp.arange(t_p, dtype=jnp.int32)
    tabs = _rope_tables(pos_p, DH_A) + _rope_tables(pos_p, D_IDX)
    q, kf, kb, vf, vb, iqw, sgn, ikf, ike, iko = _prep(z, zs, tabs, tm=256)
    a_out = _dsa_prompt(q, kb, vb, iqw, sgn, ike, iko, n_bp, t_p, tq=256)
    s0 = jnp.zeros((1, n_bp, H_B, DK_B, DV_B), F32)
    bn, s_p = _hgrn(z, lb, g_norm, s0, 0, n_bp, t_p, chunk=256, mm_dtype=BF16)
    mk_p, mv_p = _mem_kv(mem_prompt, w_mem_k[l].astype(BF16), w_mem_v[l].astype(BF16))
    m_out = _mem_attend(z, mk_p, mv_p, 0, n_bp, t_p, tq=512, mm_dtype=BF16)
    y_p = _merge(a_out, z, bn, m_out, xp, w_o, lg, lbias, tm=256)

    n_s = n_bs * t_s
    xs = x_sample.reshape(n_s, D_MODEL)
    z2, zs2 = _proj(xs, w_main, w_small, tm=n_s, tn=1024)
    pos_s = past + jnp.arange(t_s, dtype=jnp.int32)
    tabs_s = [jnp.tile(t, (n_bs, 1)) for t in _rope_tables(pos_s, DH_A) + _rope_tables(pos_s, D_IDX)]
    q2, kf2, kb2, vf2, vb2, iqw2, sgn2, ikf2, ike2, iko2 = _prep(z2, zs2, tabs_s, tm=n_s)
    nsel_s = min(TOPK_MAX, (past + t_s) // 4)
    lq = iqw2.reshape(n_bs, t_s, H_IDX, D_IDX).transpose(0, 2, 1, 3).reshape(n_bs, H_IDX * t_s, D_IDX)
    sgb = jnp.broadcast_to(
        sgn2[:, :H_IDX].reshape(n_bs, t_s, H_IDX).transpose(0, 2, 1).reshape(n_bs, H_IDX * t_s, 1),
        (n_bs, H_IDX * t_s, PAGE_SIZE))
    pad_keys = lambda a: jnp.pad(a.reshape((n_bs, t_s) + a.shape[1:]),
                                 ((0, 0), (0, PAGE_SIZE - t_s)) + ((0, 0),) * (a.ndim - 1))
    idx_pool_t = jnp.swapaxes(cache_idx_k, 2, 3)
    bias = _dsa_sample_select(page_table, idx_pool_t, l, lq, sgb, jnp.swapaxes(pad_keys(ikf2), 1, 2),
                              n_bs, t_s, nsel_s)
    q4 = q2.reshape(n_bs, t_s, H_A, DH_A)
    eye = jnp.eye(H_A, dtype=q2.dtype)
    qbd = (q4[:, None, :, :, :] * eye[None, :, None, :, None]).reshape(n_bs, H_A * t_s, W_A)
    a_out2 = _dsa_sample_attend(page_table, cache_k, cache_v, l, qbd, bias,
                                pad_keys(kf2)[:, None], pad_keys(vf2)[:, None], n_bs, t_s)
    bn2, s_s = _hgrn(z2, lb, g_norm, state_hgrn, l, n_bs, t_s, chunk=t_s, mm_dtype=F32)
    m_out2 = _mem_attend(z2, cache_mem_k, cache_mem_v, l, n_bs, t_s, tq=t_s, mm_dtype=F32)
    y_s = _merge(a_out2.reshape(n_s, W_A), z2, bn2, m_out2, xs, w_o, lg, lbias, tm=n_s)

    return (y_p.reshape(n_bp, t_p, D_MODEL), y_s.reshape(n_bs, t_s, D_MODEL),
            kf.reshape(1, n_bp, t_p, H_A, DH_A), vf.reshape(1, n_bp, t_p, H_A, DH_A),
            ikf.reshape(1, n_bp, t_p, D_IDX), s_p, mk_p, mv_p,
            kf2.reshape(1, n_bs, t_s, H_A, DH_A), vf2.reshape(1, n_bs, t_s, H_A, DH_A),
            ikf2.reshape(1, n_bs, t_s, D_IDX), s_s)
```

```python
import functools

import numpy as np
import jax
import jax.numpy as jnp
from jax import lax
from jax.experimental import pallas as pl
from jax.experimental.pallas import tpu as pltpu

D_MODEL = 2048
DH_A = 128
H_A = 8
W_A = H_A * DH_A
H_IDX = 16
D_IDX = 64
W_IDX = H_IDX * D_IDX
TOPK_MAX = 256
DK_B = 128
DV_B = 128
H_B = 4
W_B = H_B * DV_B
DH_M = 128
H_M = 4
W_M = H_M * DH_M
N_MEM = 256
PAGE_SIZE = 128
ROPE_THETA = 500000.0
ROPE_FRAC = 4
LN_EPS = 1e-5
RMS_EPS = 1e-6
DEPTH = 1
ALPHA = (2.0 * DEPTH) ** 0.25

LANE = 128
VMEM_LIMIT = 48 * 1024 * 1024

C_AQ, C_AK, C_AV, C_AG, C_IQ = 0, 1024, 2048, 3072, 4096
C_BQ, C_BF, C_BI, C_BG, C_MQ, C_MG = 5120, 5632, 6144, 6656, 7168, 7680
N_MAIN = 8192

NEG = -1e30
Q_SCALE = DH_A ** -0.5 * 1.4426950408889634
VT_ROWS = DH_A + 16
INT_MIN = -(2 ** 31)
BF16 = jnp.bfloat16
F32 = jnp.float32


def _dot(a, b, **kw):
    return jnp.dot(a, b, preferred_element_type=F32, **kw)


def _dot_nt(a, b):
    return lax.dot_general(a, b, (((1,), (1,)), ((), ())), preferred_element_type=F32)


def _dot_tn(a, b):
    return lax.dot_general(a, b, (((0,), (0,)), ((), ())), preferred_element_type=F32)


def _sigmoid(x):
    return 1.0 / (1.0 + jnp.exp(-x))


def _silu(x):
    return x * _sigmoid(x)


def _cparams(sem):
    return pltpu.CompilerParams(dimension_semantics=sem, vmem_limit_bytes=VMEM_LIMIT)


def _proj_kernel(x_ref, w_ref, ws_ref, o_ref, os_ref, xb_ref):
    j = pl.program_id(1)

    @pl.when(j == 0)
    def _():
        xb = x_ref[...].astype(BF16)
        xb_ref[...] = xb
        os_ref[...] = _dot(xb, ws_ref[...])

    o_ref[...] = _dot(xb_ref[...], w_ref[...])


def _proj(x, w_main, w_small, tm, tn):
    m = x.shape[0]
    n = w_main.shape[1]
    return pl.pallas_call(
        _proj_kernel,
        grid=(m // tm, n // tn),
        in_specs=[pl.BlockSpec((tm, D_MODEL), lambda i, j: (i, 0)),
                  pl.BlockSpec((D_MODEL, tn), lambda i, j: (0, j)),
                  pl.BlockSpec((D_MODEL, LANE), lambda i, j: (0, 0))],
        out_specs=[pl.BlockSpec((tm, tn), lambda i, j: (i, j)),
                   pl.BlockSpec((tm, LANE), lambda i, j: (i, 0))],
        out_shape=[jax.ShapeDtypeStruct((m, n), F32),
                   jax.ShapeDtypeStruct((m, LANE), F32)],
        scratch_shapes=[pltpu.VMEM((tm, D_MODEL), BF16)],
        compiler_params=_cparams(("parallel", "arbitrary")),
        name="in_proj",
    )(x, w_main, w_small)


def _rope(x, c, sa, sb, half):
    return x * c + pltpu.roll(x, LANE - half, 1) * sa + pltpu.roll(x, half, 1) * sb


def _prep_kernel(aq_ref, ak_ref, av_ref, iq_ref, zs_ref,
                 c1_ref, sa1_ref, sb1_ref, c2_ref, sa2_ref, sb2_ref,
                 q_ref, kf_ref, kb_ref, vf_ref, vt_ref, iqw_ref, sgn_ref,
                 ikf_ref, ike_ref, iko_ref):
    c1, sa1, sb1 = c1_ref[...], sa1_ref[...], sb1_ref[...]
    c2, sa2, sb2 = c2_ref[...], sa2_ref[...], sb2_ref[...]
    half_a = DH_A // ROPE_FRAC // 2
    half_i = D_IDX // ROPE_FRAC // 2
    for h in range(H_A):
        sl = slice(h * DH_A, (h + 1) * DH_A)
        q_ref[:, sl] = (_rope(aq_ref[:, sl], c1, sa1, sb1, half_a) * Q_SCALE).astype(BF16)
        kr = _rope(ak_ref[:, sl], c1, sa1, sb1, half_a)
        kf_ref[:, h, :] = kr
        kb_ref[:, sl] = kr.astype(BF16)
        v = av_ref[:, sl]
        vf_ref[:, h, :] = v
        vt_ref[h * VT_ROWS:h * VT_ROWS + DH_A, :] = v.T.astype(BF16)
        vt_ref[h * VT_ROWS + DH_A:(h + 1) * VT_ROWS, :] = jnp.ones((VT_ROWS - DH_A, v.shape[0]), BF16)

    zs = zs_ref[...]
    lane = lax.broadcasted_iota(jnp.int32, zs.shape, 1)
    ikr = _rope(zs, c2, sa2, sb2, half_i)
    ikf_ref[...] = ikr[:, :D_IDX]
    ike_ref[...] = jnp.where(lane < D_IDX, ikr, 0.0).astype(BF16)
    iko_ref[...] = jnp.where(lane >= D_IDX, pltpu.roll(ikr, D_IDX, 1), 0.0).astype(BF16)
    sgn_t = jnp.sign(pltpu.roll(zs, LANE - D_IDX, 1)).T
    sgn_ref[...] = sgn_t[:H_IDX, :]
    wscale = (D_IDX ** -0.5) * (H_IDX ** -0.5)
    for p in range(H_IDX // 2):
        sl = slice(p * LANE, (p + 1) * LANE)
        wa = jnp.abs(zs[:, D_IDX + 2 * p:D_IDX + 2 * p + 1])
        wb = jnp.abs(zs[:, D_IDX + 2 * p + 1:D_IDX + 2 * p + 2])
        wpair = jnp.where(lane < D_IDX, wa, wb) * wscale
        iqw_ref[:, sl] = (_rope(iq_ref[:, sl], c2, sa2, sb2, half_i) * wpair).astype(BF16)


def _prep(z, zs, tabs, tm):
    m = z.shape[0]
    period = tabs[0].shape[0] // tm
    zblk = lambda c: pl.BlockSpec((tm, 1024), lambda i, c=c: (i, c // 1024))
    tab = pl.BlockSpec((tm, LANE), lambda i: (i % period, 0))
    row = lambda w: pl.BlockSpec((tm, w), lambda i: (i, 0))
    col = lambda w: pl.BlockSpec((w, tm), lambda i: (0, i))
    sds = lambda w, dt: jax.ShapeDtypeStruct((m, w), dt)
    heads =pl.BlockSpec((tm, H_A, DH_A), lambda i: (i, 0, 0))
    heads_sds = jax.ShapeDtypeStruct((m, H_A, DH_A), F32)
    return pl.pallas_call(
        _prep_kernel,
        grid=(m // tm,),
        in_specs=[zblk(C_AQ), zblk(C_AK), zblk(C_AV), zblk(C_IQ), row(LANE)] + [tab] * 6,
        out_specs=[row(W_A), heads, row(W_A), heads, col(H_A * VT_ROWS), row(W_IDX), col(H_IDX),
                   row(D_IDX), row(LANE), row(LANE)],
        out_shape=[sds(W_A, BF16), heads_sds, sds(W_A, BF16), heads_sds,
                   jax.ShapeDtypeStruct((H_A * VT_ROWS, m), BF16), sds(W_IDX, BF16),
                   jax.ShapeDtypeStruct((H_IDX, m), F32),
                   sds(D_IDX, F32), sds(LANE, BF16), sds(LANE, BF16)],
        compiler_params=_cparams(("parallel",)),
        name="rope_prep",
    )(z, z, z, z, zs, *tabs)


def _rope_tables(pos, d):
    r = d // ROPE_FRAC
    half = r // 2
    n = pos.shape[0]
    inv = ROPE_THETA ** (-jnp.arange(half, dtype=F32) / half)
    ang = pos.astype(F32)[:, None] * inv[None, :]
    cos, sin = jnp.cos(ang), jnp.sin(ang)
    zh = jnp.zeros((n, half), F32)
    c = jnp.concatenate([cos, cos, jnp.ones((n, d - r), F32)], axis=1)
    sa = jnp.concatenate([-sin, zh, jnp.zeros((n, d - r), F32)], axis=1)
    sb = jnp.concatenate([zh, sin, jnp.zeros((n, d - r), F32)], axis=1)
    reps = LANE // d
    return [jnp.tile(t, (1, reps)) for t in (c, sa, sb)]


KEY_NEG_INF = -2139095041
F32_LOWEST = -3.4028234663852886e38


def _key_to_float(key):
    key = jnp.maximum(key, jnp.int32(KEY_NEG_INF))
    b = key ^ (lax.shift_right_arithmetic(key, 31) & jnp.int32(0x7FFFFFFF))
    return lax.bitcast_convert_type(b, F32)


def _lane_total(part):
    return _dot(part.astype(BF16), jnp.ones((LANE, LANE), BF16))


def _radix_select(count_ge, shape, total, nsel):
    def body(bi, carry):
        pu, cacc = carry
        bit = jnp.int32(31) - bi
        cand_u = pu | lax.shift_left(jnp.int32(1), bit)
        cnt = count_ge(_key_to_float(cand_u ^ jnp.int32(INT_MIN)))
        ok = cnt >= nsel
        return jnp.where(ok, cand_u, pu), jnp.where(ok, cnt, cacc)

    pu0 = jnp.zeros(shape, jnp.int32)
    c0 = jnp.full(shape, total, F32)
    pu, cacc = lax.fori_loop(0, 32, body, (pu0, c0))
    return _key_to_float(pu ^ jnp.int32(INT_MIN)), cacc


def _dsa_prompt_kernel(q_ref, k_ref, vt_ref, iq_ref, sgn_ref, ike_ref, iko_ref, o_ref,
                       key_ref, bias_ref, m_ref, acc_ref, *, tq, nsel):
    tk = tq
    i = pl.program_id(1)
    nch = i + 1
    sub = 8

    kidx0 = lax.broadcasted_iota(jnp.int32, (tk, tq), 0)
    qidx = i * tq + lax.broadcasted_iota(jnp.int32, (tk, tq), 1)

    def chunk_slice(c):
        return pl.ds(pl.multiple_of(c * tk, tk), tk)

    def score_body(c, carry):
        ds = chunk_slice(c)
        ke = ike_ref[ds, :]
        ko = iko_ref[ds, :]
        acc = jnp.zeros((tk, tq), F32)
        for h in range(H_IDX):
            rhs = iq_ref[:, (h // 2) * LANE:(h // 2 + 1) * LANE]
            d = _dot_nt(ke if h % 2 == 0 else ko, rhs)
            acc = acc + sgn_ref[h:h + 1, :] * jnp.maximum(d, 0.0)
        key_ref[ds, :] = jnp.where(kidx0 + c * tk <= qidx, acc, -jnp.inf)
        return carry

    lax.fori_loop(0, nch, score_body, 0)

    def count(cmp_fn):
        def body(c, part):
            hit = jnp.where(cmp_fn(key_ref[chunk_slice(c), :]), 1.0, 0.0)
            return part + jnp.sum(hit.reshape(tk // sub, sub, tq), axis=0)
        part = lax.fori_loop(0, nch, body, jnp.zeros((sub, tq), F32))
        return jnp.sum(part, axis=0, keepdims=True)

    total = (nch * tk).astype(F32)
    tau, c_ge = _radix_select(lambda cand: count(lambda kc: kc >= cand), (1, tq), total, float(nsel))
    tau_eff = jnp.maximum(tau, F32_LOWEST)

    def bias_body(c, carry):
        ds = chunk_slice(c)
        bias_ref[ds, :] = jnp.where(key_ref[ds, :] >= tau_eff, 0.0, NEG)
        return carry

    lax.fori_loop(0, nch, bias_body, 0)

    tie_q = jnp.logical_and(c_ge > float(nsel), tau > -jnp.inf)
    has_tie = jnp.max(jnp.where(tie_q, 1.0, 0.0)) > 0.5

    @pl.when(has_tie)
    def _():
        need = float(nsel) - count(lambda kc: kc > tau)
        lower = jnp.where(lax.broadcasted_iota(jnp.int32, (tk, tk), 1)
                          <= lax.broadcasted_iota(jnp.int32, (tk, tk), 0), 1.0, 0.0).astype(BF16)
        real = tau > -jnp.inf

        def tie_body(c, run):
            ds = chunk_slice(c)
            kc = key_ref[ds, :]
            eqf = jnp.where(kc == tau, 1.0, 0.0)
            pre = _dot(lower, eqf.astype(BF16)) + run
            keep = jnp.where(kc > tau, 1.0, jnp.where(real, eqf * jnp.where(pre <= need, 1.0, 0.0), 0.0))
            bias_ref[ds, :] = jnp.where(keep > 0.5, 0.0, NEG)
            return run + jnp.sum(eqf, axis=0, keepdims=True)

        lax.fori_loop(0, nch, tie_body, jnp.zeros((1, tq), F32))

    m_ref[...] = jnp.full(m_ref.shape, NEG, F32)
    acc_ref[...] = jnp.zeros(acc_ref.shape, F32)

    def attn_body(c, carry):
        ds = chunk_slice(c)
        bias = bias_ref[ds, :]
        heads = [slice(h * DH_A, (h + 1) * DH_A) for h in range(H_A)]
        ss = [_dot_nt(k_ref[ds, hs], q_ref[:, hs]) for hs in heads]
        ps, alphas = [], []
        for h in range(H_A):
            s = ss[h] + bias
            m_prev = m_ref[h]
            m_new = jnp.maximum(m_prev, jnp.max(s, axis=0, keepdims=True))
            alphas.append(jnp.exp2(m_prev - m_new))
            ps.append(jnp.exp2(s - m_new).astype(BF16))
            m_ref[h] = m_new
        for h in range(H_A):
            acc_ref[h] = alphas[h] * acc_ref[h] + _dot(vt_ref[h * VT_ROWS:(h + 1) * VT_ROWS, ds], ps[h])
        return carry

    lax.fori_loop(0, nch, attn_body, 0)
    for h in range(H_A):
        acc = acc_ref[h]
        o_ref[:, h * DH_A:(h + 1) * DH_A] = (acc[:DH_A] / acc[DH_A:DH_A + 1]).T


def _dsa_prompt(q, k, vt, iqw, sgn_t, ike, iko, n_b, n_t, tq):
    nsel = min(TOPK_MAX, n_t // 4)
    nq = n_t // tq
    qblk = lambda w: pl.BlockSpec((tq, w), lambda b, i: (b * nq + i, 0))
    full = lambda w: pl.BlockSpec((n_t, w), lambda b, i: (b, 0))
    return pl.pallas_call(
        functools.partial(_dsa_prompt_kernel, tq=tq, nsel=nsel),
        grid=(n_b, nq),
        in_specs=[qblk(W_A), full(W_A), pl.BlockSpec((H_A * VT_ROWS, n_t), lambda b, i: (0, b)), qblk(W_IDX),
                  pl.BlockSpec((H_IDX, tq), lambda b, i: (0, b * nq + i)), full(LANE), full(LANE)],
        out_specs=qblk(W_A),
        out_shape=jax.ShapeDtypeStruct((n_b * n_t, W_A), F32),
        scratch_shapes=[pltpu.VMEM((n_t, tq), F32),
                        pltpu.VMEM((n_t, tq), F32),
                        pltpu.VMEM((H_A, 1, tq), F32),
                        pltpu.VMEM((H_A, VT_ROWS, tq), F32)],
        compiler_params=_cparams(("parallel", "arbitrary")),
        name="dsa_prompt",
    )(q, k, vt, iqw, sgn_t, ike, iko)


HGRN_BLOCK = 8


def _hgrn_levels(chunk):
    levels, m = [], HGRN_BLOCK
    while m < chunk:
        levels.append(m)
        m *= 2
    return levels


def _hgrn_masks(chunk):
    row, col = np.indices((chunk, chunk))
    masks = []
    for m in _hgrn_levels(chunk):
        same_pair = (row // (2 * m)) == (col // (2 * m))
        masks.append(same_pair & (row % (2 * m) >= m) & (col % (2 * m) < m))
    for j in range(HGRN_BLOCK):
        masks.append((row - col == j) & (row % HGRN_BLOCK >= j))
    return np.stack(masks).astype(np.float32)


def _hgrn_kernel(bq_ref, bf_ref, bi_ref, lb_ref, g_ref, s0_ref, mask_ref, o_ref, s_out_ref, st_ref,
                 *, n_t, chunk, heads, mm_dtype):
    c_ = chunk
    g = g_ref[...]
    for hh in range(heads):
        st_ref[hh] = s0_ref[0, 0, hh].T
    row = lax.broadcasted_iota(jnp.int32, (c_, c_), 0)
    col = lax.broadcasted_iota(jnp.int32, (c_, c_), 1)
    lower = jnp.where(col <= row, 1.0, 0.0)
    levels = _hgrn_levels(c_)

    def head_chunk(hh, ds):
        hs = slice(hh * DK_B, (hh + 1) * DK_B)
        lb = lb_ref[:, hs]
        bq = bq_ref[ds, hs]
        qs = bq * _sigmoid(bq)
        f = lb + (1.0 - lb) * _sigmoid(bf_ref[ds, hs])
        kk = 1.0 - f
        v = bi_ref[ds, hs]
        gcum = _dot(lower, jnp.log(f), precision=lax.Precision.HIGHEST)
        a = jnp.zeros((c_, c_), F32)
        for li, m in enumerate(levels):
            g3 = gcum.reshape(c_ // (2 * m), 2 * m, DK_B)
            ref = jnp.broadcast_to(g3[:, m - 1:m, :], g3.shape).reshape(c_, DK_B)
            qh = qs * jnp.exp(jnp.minimum(gcum - ref, 0.0))
            kh = kk * jnp.exp(jnp.minimum(ref - gcum, 0.0))
            a = a + _dot_nt(qh.astype(mm_dtype), kh.astype(mm_dtype)) * mask_ref[li]
        for j in range(HGRN_BLOCK):
            kj = kk if j == 0 else pltpu.roll(kk, j, 0)
            gj = gcum if j == 0 else pltpu.roll(gcum, j, 0)
            e = jnp.exp(jnp.minimum(gcum - gj, 0.0))
            aj = jnp.sum(qs * kj * e, axis=1, keepdims=True)
            a = a + aj * mask_ref[len(levels) + j]
        st = st_ref[hh]
        o = (_dot_nt((qs * jnp.exp(gcum)).astype(mm_dtype), st.astype(mm_dtype))
             + _dot(a.astype(mm_dtype), v.astype(mm_dtype)))
        glast = gcum[c_ - 1:c_, :]
        kd = kk * jnp.exp(glast - gcum)
        st_ref[hh] = jnp.exp(glast) * st + _dot_tn(v.astype(mm_dtype), kd.astype(mm_dtype))
        o_ref[ds, hs] = o * lax.rsqrt(jnp.mean(o * o, axis=1, keepdims=True) + RMS_EPS) * g

    def chunk_body(ci, carry):
        ds = pl.ds(pl.multiple_of(ci * c_, c_), c_)
        for hh in range(heads):
            head_chunk(hh, ds)
        return carry

    lax.fori_loop(0, n_t // c_, chunk_body, 0)
    for hh in range(heads):
        s_out_ref[0, 0, hh] = st_ref[hh].T


def _hgrn(z, lb, g, s0, layer, n_b, n_t, chunk, heads, mm_dtype):
    w = heads * DK_B
    zcol = lambda c: pl.BlockSpec((n_t, w), lambda b, h, c=c: (b, c // w + h))
    sblk = lambda lyr: pl.BlockSpec((1, 1, heads, DK_B, DV_B), lambda b, h: (lyr, b, h, 0, 0))
    masks = jnp.asarray(_hgrn_masks(chunk))
    return pl.pallas_call(
        functools.partial(_hgrn_kernel, n_t=n_t, chunk=chunk, heads=heads, mm_dtype=mm_dtype),
        grid=(n_b, H_B // heads),
        in_specs=[zcol(C_BQ), zcol(C_BF), zcol(C_BI),
                  pl.BlockSpec((1, w), lambda b, h: (0, h)),
                  pl.BlockSpec((1, DV_B), lambda b, h: (0, 0)),
                  sblk(layer),
                  pl.BlockSpec(masks.shape, lambda b, h: (0, 0, 0))],
        out_specs=[pl.BlockSpec((n_t, w), lambda b, h: (b, h)), sblk(0)],
        out_shape=[jax.ShapeDtypeStruct((n_b * n_t, W_B), F32),
                   jax.ShapeDtypeStruct((1, n_b, H_B, DK_B, DV_B), F32)],
        scratch_shapes=[pltpu.VMEM((heads, DV_B, DK_B), F32)],
        compiler_params=_cparams(("parallel", "arbitrary")),
        name="hgrn2",
    )(z, z, z, lb, g, s0, masks)


def _mem_kv_kernel(x_ref, wk_ref, wv_ref, mk_ref, mv_ref):
    x = x_ref[0].astype(BF16)
    mk = _dot(x, wk_ref[...])
    mv = _dot(x, wv_ref[...])
    for h in range(H_M):
        hs = slice(h * DH_M, (h + 1) * DH_M)
        mk_ref[0, 0, :, h, :] = mk[:, hs]
        mv_ref[0, 0, :, h, :] = mv[:, hs]


def _mem_kv(mem, wk, wv):
    n_b = mem.shape[0]
    wspec = pl.BlockSpec((D_MODEL, W_M), lambda b: (0, 0))
    ospec = pl.BlockSpec((1, 1, N_MEM, H_M, DH_M), lambda b: (0, b, 0, 0, 0))
    sds = jax.ShapeDtypeStruct((1, n_b, N_MEM, H_M, DH_M), F32)
    return pl.pallas_call(
        _mem_kv_kernel,
        grid=(n_b,),
        in_specs=[pl.BlockSpec((1, N_MEM, D_MODEL), lambda b: (b, 0, 0)), wspec, wspec],
        out_specs=[ospec, ospec],
        out_shape=[sds, sds],
        compiler_params=_cparams(("parallel",)),
        name="mem_kv",
    )(mem, wk, wv)


def _mem_kernel(q_ref, mk_ref, mv_ref, o_ref, *, mm_dtype):
    for h in range(H_M):
        hs = slice(h * DH_M, (h + 1) * DH_M)
        q = (q_ref[:, hs] * (DH_M ** -0.5)).astype(mm_dtype)
        s = _dot_nt(q, mk_ref[0, 0, :, h, :].astype(mm_dtype))
        p = jnp.exp(s - jnp.max(s, axis=1, keepdims=True))
        l = jnp.sum(p, axis=1, keepdims=True)
        o_ref[:, hs] = _dot(p.astype(mm_dtype), mv_ref[0, 0, :, h, :].astype(mm_dtype)) / l


def _mem_attend(z, mk, mv, layer, n_b, n_t, tq, mm_dtype):
    nq = n_t // tq
    kv = pl.BlockSpec((1, 1, N_MEM, H_M, DH_M), lambda b, i: (layer, b, 0, 0, 0))
    return pl.pallas_call(
        functools.partial(_mem_kernel, mm_dtype=mm_dtype),
        grid=(n_b, nq),
        in_specs=[pl.BlockSpec((tq, W_M), lambda b, i: (b * nq + i, C_MQ // W_M)), kv, kv],
        out_specs=pl.BlockSpec((tq, W_M), lambda b, i: (b * nq + i, 0)),
        out_shape=jax.ShapeDtypeStruct((n_b * n_t, W_M), F32),
        compiler_params=_cparams(("parallel", "parallel")),
        name="mem_attend",
    )(z, mk, mv)


def _merge_kernel(a_ref, ag_ref, bn_ref, bg_ref, mo_ref, mg_ref, h_ref, wo_ref, lg_ref, lbias_ref, o_ref):
    a = (a_ref[...] * _silu(ag_ref[...])).astype(BF16)
    b = (bn_ref[...] * _silu(bg_ref[...])).astype(BF16)
    m = (mo_ref[...] * _silu(mg_ref[...])).astype(BF16)
    y = (_dot(a, wo_ref[0:W_A, :]) + _dot(b, wo_ref[W_A:W_A + W_B, :])
         + _dot(m, wo_ref[W_A + W_B:D_MODEL, :]))
    r = ALPHA * h_ref[...] + y
    xc = r - jnp.mean(r, axis=1, keepdims=True)
    var = jnp.mean(xc * xc, axis=1, keepdims=True)
    o_ref[...] = xc * lax.rsqrt(var + LN_EPS) * lg_ref[...] + lbias_ref[...]


def _merge(a_out, z, bn, m_out, h, w_out, ln_g, ln_b, tm):
    m = h.shape[0]
    row = lambda w: pl.BlockSpec((tm, w), lambda i: (i, 0))
    zcol = lambda c, w: pl.BlockSpec((tm, w), lambda i, c=c, w=w: (i, c // w))
    const = lambda s: pl.BlockSpec(s, lambda i: (0, 0))
    return pl.pallas_call(
        _merge_kernel,
        grid=(m // tm,),
        in_specs=[row(W_A), zcol(C_AG, W_A), row(W_B), zcol(C_BG, W_B), row(W_M), zcol(C_MG, W_M),
                  row(D_MODEL), const((D_MODEL, D_MODEL)), const((1, D_MODEL)), const((1, D_MODEL))],
        out_specs=row(D_MODEL),
        out_shape=jax.ShapeDtypeStruct((m, D_MODEL), F32),
        compiler_params=_cparams(("parallel",)),
        name="merge",
    )(a_out, z, bn, z, m_out, z, h, w_out, ln_g, ln_b)


PAGES_PER_STEP = 8


def _dsa_sample_select_kernel(pt_ref, *refs, n_t, n_keys_pad, steps, nsel):
    pages = refs[:PAGES_PER_STEP]
    lq_ref, sg_ref, iknew_ref, bias_ref, key_ref = refs[PAGES_PER_STEP:]
    j = pl.program_id(1)
    lq = lq_ref[0]
    sg = sg_ref[0]

    def score(ik_t):
        d = _dot(lq, ik_t.astype(BF16))
        r = jnp.maximum(d, 0.0) * jnp.concatenate([sg] * (ik_t.shape[1] // PAGE_SIZE), axis=1)
        s = r[0:n_t]
        for h in range(1, H_IDX):
            s = s + r[h * n_t:(h + 1) * n_t]
        return s

    width = PAGES_PER_STEP * PAGE_SIZE
    ik_step = jnp.concatenate([pages[u][0, 0] for u in range(PAGES_PER_STEP)], axis=1)
    key_ref[:, pl.ds(pl.multiple_of(j * width, width), width)] = score(ik_step)

    @pl.when(j == steps - 1)
    def _():
        past = n_keys_pad - PAGE_SIZE
        s = score(iknew_ref[0])
        t_i = lax.broadcasted_iota(jnp.int32, (n_t, PAGE_SIZE), 0)
        k_i = lax.broadcasted_iota(jnp.int32, (n_t, PAGE_SIZE), 1)
        key_ref[:, past:n_keys_pad] = jnp.where(k_i <= t_i, s, -jnp.inf)

        keys = key_ref[...]

        n_tiles = n_keys_pad // LANE

        def count_ge(cand):
            part = jnp.zeros((n_t, LANE), F32)
            for c in range(n_tiles):
                part = part + jnp.where(keys[:, c * LANE:(c + 1) * LANE] >= cand, 1.0, 0.0)
            return _lane_total(part)

        tau_r, c_ge_r = _radix_select(count_ge, (n_t, LANE), float(n_keys_pad), float(nsel))
        tau = tau_r[:, 0:1]
        tau_eff = jnp.maximum(tau, F32_LOWEST)
        bias_ref[0] = jnp.where(keys >= tau_eff, 0.0, NEG)

        tie_rows = jnp.logical_and(c_ge_r > float(nsel), tau_r > -jnp.inf)
        has_tie = jnp.max(jnp.where(tie_rows, 1.0, 0.0)) > 0.5

        @pl.when(has_tie)
        def _():
            c_gt = jnp.sum(jnp.where(keys > tau, 1.0, 0.0), axis=1, keepdims=True)
            need = float(nsel) - c_gt
            ri = lax.broadcasted_iota(jnp.int32, (PAGE_SIZE, PAGE_SIZE), 0)
            ci = lax.broadcasted_iota(jnp.int32, (PAGE_SIZE, PAGE_SIZE), 1)
            upper = jnp.where(ri <= ci, 1.0, 0.0)
            real = tau > -jnp.inf

            def tie_body(c, run):
                ds = pl.ds(pl.multiple_of(c * PAGE_SIZE, PAGE_SIZE), PAGE_SIZE)
                kc = key_ref[:, ds]
                eqf = jnp.where(kc == tau, 1.0, 0.0)
                pre = _dot(eqf, upper) + run
                keep = jnp.where(kc > tau, 1.0, jnp.where(real, eqf * jnp.where(pre <= need, 1.0, 0.0), 0.0))
                bias_ref[0, :, ds] = jnp.where(keep > 0.5, 0.0, NEG)
                return run + jnp.sum(eqf, axis=1, keepdims=True)

            lax.fori_loop(0, n_keys_pad // PAGE_SIZE, tie_body, jnp.zeros((n_t, 1), F32))


def _dsa_sample_select(page_table, idx_pool, layer, lq, sgb, iknew, n_b, n_t, nsel):
    n_pages = page_table.shape[1]
    steps = n_pages // PAGES_PER_STEP
    n_keys_pad = (n_pages + 1) * PAGE_SIZE
    page_spec = lambda u: pl.BlockSpec(
        (1, 1, D_IDX, PAGE_SIZE), lambda r, j, pt, u=u: (layer, pt[r, j * PAGES_PER_STEP + u], 0, 0))
    per_req = lambda s: pl.BlockSpec((1,) + s, lambda r, j, pt: (r, 0, 0))
    return pl.pallas_call(
        functools.partial(_dsa_sample_select_kernel, n_t=n_t, n_keys_pad=n_keys_pad, steps=steps, nsel=nsel),
        grid_spec=pltpu.PrefetchScalarGridSpec(
            num_scalar_prefetch=1,
            grid=(n_b, steps),
            in_specs=[page_spec(u) for u in range(PAGES_PER_STEP)]
            + [per_req((H_IDX * n_t, D_IDX)), per_req((H_IDX * n_t, PAGE_SIZE)), per_req((D_IDX, PAGE_SIZE))],
            out_specs=per_req((n_t, n_keys_pad)),
            scratch_shapes=[pltpu.VMEM((n_t, n_keys_pad), F32)]),
        out_shape=jax.ShapeDtypeStruct((n_b, n_t, n_keys_pad), F32),
        compiler_params=_cparams(("parallel", "arbitrary")),
        name="dsa_sample_select",
    )(page_table, *([idx_pool] * PAGES_PER_STEP), lq, sgb, iknew)


def _dsa_sample_attend_kernel(pt_ref, *refs, n_t, steps):
    kp = refs[:PAGES_PER_STEP]
    vp = refs[PAGES_PER_STEP:2 * PAGES_PER_STEP]
    qbd_ref, bias_ref, knew_ref, vnew_ref, o_ref, m_ref, l_ref, acc_ref = refs[2 * PAGES_PER_STEP:]
    j = pl.program_id(1)
    rows = H_A * n_t

    @pl.when(j == 0)
    def _():
        m_ref[...] = jnp.full(m_ref.shape, NEG, F32)
        l_ref[...] = jnp.zeros(l_ref.shape, F32)
        acc_ref[...] = jnp.zeros(acc_ref.shape, F32)

    qbd = qbd_ref[0]

    def update(kblk, vblk, bias):
        n = kblk.shape[0]
        s = _dot_nt(qbd, kblk) + jnp.concatenate([bias] * H_A, axis=0)
        m_prev = m_ref[...]
        m_new = jnp.maximum(m_prev, jnp.max(s, axis=1, keepdims=True))
        alpha = jnp.exp2(m_prev - m_new)
        p = jnp.exp2(s - jnp.concatenate([m_new] * (n // LANE), axis=1))
        l_ref[...] = alpha * l_ref[...] + jnp.sum(p, axis=1, keepdims=True)
        pv = _dot(p.astype(BF16), vblk)
        diag = jnp.concatenate(
            [pv[h * n_t:(h + 1) * n_t, h * DH_A:(h + 1) * DH_A] for h in range(H_A)], axis=0)
        acc_ref[...] = alpha * acc_ref[...] + diag
        m_ref[...] = m_new

    def page2d(ref):
        return jnp.concatenate(
            [ref[0, 0, pl.ds(h, PAGE_SIZE, stride=H_A), :] for h in range(H_A)], axis=1).astype(BF16)

    kblk = jnp.concatenate([page2d(kp[u]) for u in range(PAGES_PER_STEP)], axis=0)
    vblk = jnp.concatenate([page2d(vp[u]) for u in range(PAGES_PER_STEP)], axis=0)
    width = PAGES_PER_STEP * PAGE_SIZE
    off = pl.multiple_of(j * width, width)
    update(kblk, vblk, bias_ref[0, :, pl.ds(off, width)])

    @pl.when(j == steps - 1)
    def _():
        past = steps * width
        update(page2d(knew_ref), page2d(vnew_ref), bias_ref[0, :, past:past + PAGE_SIZE])
        o = acc_ref[...] / l_ref[...]
        o_ref[0] = jnp.concatenate([o[h * n_t:(h + 1) * n_t] for h in range(H_A)], axis=1)


def _dsa_sample_attend(page_table, k_pool, v_pool, layer, qbd, bias, knew, vnew, n_b, n_t):
    n_pages = page_table.shape[1]
    steps = n_pages // PAGES_PER_STEP
    n_keys_pad = (n_pages + 1) * PAGE_SIZE
    page_spec = lambda u: pl.BlockSpec(
        (1, 1, PAGE_SIZE * H_A, DH_A), lambda r, j, pt, u=u: (layer, pt[r, j * PAGES_PER_STEP + u], 0, 0))
    per_req = lambda s: pl.BlockSpec((1,) + s, lambda r, j, pt: (r, 0, 0))
    rows = H_A * n_t
    return pl.pallas_call(
        functools.partial(_dsa_sample_attend_kernel, n_t=n_t, steps=steps),
        grid_spec=pltpu.PrefetchScalarGridSpec(
            num_scalar_prefetch=1,
            grid=(n_b, steps),
            in_specs=[page_spec(u) for u in range(PAGES_PER_STEP)] * 2
            + [per_req((rows, W_A)), per_req((n_t, n_keys_pad))]
            + [pl.BlockSpec((1, 1, PAGE_SIZE * H_A, DH_A), lambda r, j, pt: (r, 0, 0, 0))] * 2,
            out_specs=per_req((n_t, W_A)),
            scratch_shapes=[pltpu.VMEM((rows, LANE), F32), pltpu.VMEM((rows, LANE), F32),
                            pltpu.VMEM((rows, DH_A), F32)]),
        out_shape=jax.ShapeDtypeStruct((n_b, n_t, W_A), F32),
        compiler_params=_cparams(("parallel", "arbitrary")),
        name="dsa_sample_attend",
    )(page_table, *([k_pool] * PAGES_PER_STEP), *([v_pool] * PAGES_PER_STEP), qbd, bias, knew, vnew)


def _reorder_w_in(w):
    o_ik = C_IQ + W_IDX
    o_iw = o_ik + D_IDX
    o_bq = o_iw + H_IDX
    main = jnp.concatenate([w[:, :o_ik], w[:, o_bq:]], axis=1).astype(BF16)
    small = jnp.concatenate(
        [w[:, o_ik:o_bq], jnp.zeros((D_MODEL, LANE - D_IDX - H_IDX), w.dtype)], axis=1).astype(BF16)
    return main, small


def kernel(x_prompt, x_sample, mem_prompt, cache_k, cache_v, cache_idx_k, state_hgrn, cache_mem_k,
           cache_mem_v, page_table, w_in, lb_logits, hgrn_norm_g, w_mem_k, w_mem_v, w_out, ln_g, ln_b):
    n_bp, t_p, _ = x_prompt.shape
    n_bs, t_s, _ = x_sample.shape
    n_pages = page_table.shape[1]
    past = n_pages * PAGE_SIZE
    l = 0

    lb_all = jnp.cumsum(jax.nn.softmax(lb_logits.astype(F32), axis=0), axis=0)
    lb = lb_all[l][None, :]
    g_norm = hgrn_norm_g[l][None, :]
    w_main, w_small = _reorder_w_in(w_in[l])
    w_o = w_out[l].astype(BF16)
    lg, lbias = ln_g[l][None, :], ln_b[l][None, :]

    xp = x_prompt.reshape(n_bp * t_p, D_MODEL)
    z, zs = _proj(xp, w_main, w_small, tm=1024, tn=1024)
    pos_p = jnp.arange(t_p, dtype=jnp.int32)
    tabs = _rope_tables(pos_p, DH_A) + _rope_tables(pos_p, D_IDX)
    q, kf, kb, vf, vt, iqw, sgn_t, ikf, ike, iko = _prep(z, zs, tabs, tm=256)
    a_out = _dsa_prompt(q, kb, vt, iqw, sgn_t, ike, iko, n_bp, t_p, tq=256)
    s0 = jnp.zeros((1, n_bp, H_B, DK_B, DV_B), F32)
    bn, s_p = _hgrn(z, lb, g_norm, s0, 0, n_bp, t_p, chunk=128, heads=H_B, mm_dtype=BF16)
    mk_p, mv_p = _mem_kv(mem_prompt, w_mem_k[l].astype(BF16), w_mem_v[l].astype(BF16))
    m_out = _mem_attend(z, mk_p, mv_p, 0, n_bp, t_p, tq=512, mm_dtype=BF16)
    y_p = _merge(a_out, z, bn, m_out, xp, w_o, lg, lbias, tm=256)

    n_s = n_bs * t_s
    xs = x_sample.reshape(n_s, D_MODEL)
    z2, zs2 = _proj(xs, w_main, w_small, tm=n_s, tn=1024)
    pos_s = past + jnp.arange(t_s, dtype=jnp.int32)
    tabs_s = [jnp.tile(t, (n_bs, 1)) for t in _rope_tables(pos_s, DH_A) + _rope_tables(pos_s, D_IDX)]
    q2, kf2, kb2, vf2, vt2, iqw2, sgn_t2, ikf2, ike2, iko2 = _prep(z2, zs2, tabs_s, tm=n_s)
    nsel_s = min(TOPK_MAX, (past + t_s) // 4)
    lq = iqw2.reshape(n_bs, t_s, H_IDX, D_IDX).transpose(0, 2, 1, 3).reshape(n_bs, H_IDX * t_s, D_IDX)
    sgb = jnp.broadcast_to(
        sgn_t2.reshape(H_IDX, n_bs, t_s).transpose(1, 0, 2).reshape(n_bs, H_IDX * t_s, 1),
        (n_bs, H_IDX * t_s, PAGE_SIZE))
    pad_keys = lambda a: jnp.pad(a.reshape((n_bs, t_s) + a.shape[1:]),
                                 ((0, 0), (0, PAGE_SIZE - t_s)) + ((0, 0),) * (a.ndim - 1))
    idx_pool_t = jnp.swapaxes(cache_idx_k, 2, 3)
    bias = _dsa_sample_select(page_table, idx_pool_t, l, lq, sgb, jnp.swapaxes(pad_keys(ikf2), 1, 2),
                              n_bs, t_s, nsel_s)
    q4 = q2.reshape(n_bs, t_s, H_A, DH_A)
    eye = jnp.eye(H_A, dtype=q2.dtype)
    qbd = (q4[:, None, :, :, :] * eye[None, :, None, :, None]).reshape(n_bs, H_A * t_s, W_A)
    n_layers, n_phys = cache_k.shape[:2]
    as_rows = lambda a: a.reshape(a.shape[0], a.shape[1], PAGE_SIZE * H_A, DH_A)
    a_out2 = _dsa_sample_attend(page_table, as_rows(cache_k), as_rows(cache_v), l, qbd, bias,
                                as_rows(pad_keys(kf2)[:, None]), as_rows(pad_keys(vf2)[:, None]), n_bs, t_s)
    bn2, s_s = _hgrn(z2, lb, g_norm, state_hgrn, l, n_bs, t_s, chunk=t_s, heads=H_B, mm_dtype=F32)
    m_out2 = _mem_attend(z2, cache_mem_k, cache_mem_v, l, n_bs, t_s, tq=t_s, mm_dtype=F32)
    y_s = _merge(a_out2.reshape(n_s, W_A), z2, bn2, m_out2, xs, w_o, lg, lbias, tm=n_s)

    return (y_p.reshape(n_bp, t_p, D_MODEL), y_s.reshape(n_bs, t_s, D_MODEL),
            kf.reshape(1, n_bp, t_p, H_A, DH_A), vf.reshape(1, n_bp, t_p, H_A, DH_A),
            ikf.reshape(1, n_bp, t_p, D_IDX), s_p, mk_p, mv_p,
            kf2.reshape(1, n_bs, t_s, H_A, DH_A), vf2.reshape(1, n_bs, t_s, H_A, DH_A),
            ikf2.reshape(1, n_bs, t_s, D_IDX), s_s)
```

```python
import functools

import numpy as np
import jax
import jax.numpy as jnp
from jax import lax
from jax.experimental import pallas as pl
from jax.experimental.pallas import tpu as pltpu

D_MODEL = 2048
DH_A = 128
H_A = 8
W_A = H_A * DH_A
H_IDX = 16
D_IDX = 64
W_IDX = H_IDX * D_IDX
TOPK_MAX = 256
DK_B = 128
DV_B = 128
H_B = 4
W_B = H_B * DV_B
DH_M = 128
H_M = 4
W_M = H_M * DH_M
N_MEM = 256
PAGE_SIZE = 128
ROPE_THETA = 500000.0
ROPE_FRAC = 4
LN_EPS = 1e-5
RMS_EPS = 1e-6
DEPTH = 1
ALPHA = (2.0 * DEPTH) ** 0.25

LANE = 128
VMEM_LIMIT = 48 * 1024 * 1024

C_AQ, C_AK, C_AV, C_AG, C_IQ = 0, 1024, 2048, 3072, 4096
C_BQ, C_BF, C_BI, C_BG, C_MQ, C_MG = 5120, 5632, 6144, 6656, 7168, 7680
N_MAIN = 8192

NEG = -1e30
Q_SCALE = DH_A ** -0.5 * 1.4426950408889634
VT_ROWS = DH_A + 16
INT_MIN = -(2 ** 31)
BF16 = jnp.bfloat16
F32 = jnp.float32


def _dot(a, b, **kw):
    return jnp.dot(a, b, preferred_element_type=F32, **kw)


def _dot_nt(a, b):
    return lax.dot_general(a, b, (((1,), (1,)), ((), ())), preferred_element_type=F32)


def _dot_tn(a, b):
    return lax.dot_general(a, b, (((0,), (0,)), ((), ())), preferred_element_type=F32)


def _sigmoid(x):
    return 1.0 / (1.0 + jnp.exp(-x))


def _silu(x):
    return x * _sigmoid(x)


def _cparams(sem):
    return pltpu.CompilerParams(dimension_semantics=sem, vmem_limit_bytes=VMEM_LIMIT)


def _proj_kernel(x_ref, w_ref, ws_ref, o_ref, os_ref, xb_ref):
    j = pl.program_id(1)

    @pl.when(j == 0)
    def _():
        xb = x_ref[...].astype(BF16)
        xb_ref[...] = xb
        os_ref[...] = _dot(xb, ws_ref[...])

    o_ref[...] = _dot(xb_ref[...], w_ref[...])


def _proj(x, w_main, w_small, tm, tn):
    m = x.shape[0]
    n = w_main.shape[1]
    return pl.pallas_call(
        _proj_kernel,
        grid=(m // tm, n // tn),
        in_specs=[pl.BlockSpec((tm, D_MODEL), lambda i, j: (i, 0)),
                  pl.BlockSpec((D_MODEL, tn), lambda i, j: (0, j)),
                  pl.BlockSpec((D_MODEL, LANE), lambda i, j: (0, 0))],
        out_specs=[pl.BlockSpec((tm, tn), lambda i, j: (i, j)),
                   pl.BlockSpec((tm, LANE), lambda i, j: (i, 0))],
        out_shape=[jax.ShapeDtypeStruct((m, n), F32),
                   jax.ShapeDtypeStruct((m, LANE), F32)],
        scratch_shapes=[pltpu.VMEM((tm, D_MODEL), BF16)],
        compiler_params=_cparams(("parallel", "arbitrary")),
        name="in_proj",
    )(x, w_main, w_small)


def _rope(x, c, sa, sb, half):
    return x * c + pltpu.roll(x, LANE - half, 1) * sa + pltpu.roll(x, half, 1) * sb


def _prep_kernel(aq_ref, ak_ref, av_ref, iq_ref, zs_ref,
                 c1_ref, sa1_ref, sb1_ref, c2_ref, sa2_ref, sb2_ref,
                 q_ref, kf_ref, kb_ref, vf_ref, vt_ref, iqw_ref, sgn_ref,
                 ikf_ref, ike_ref, iko_ref):
    c1, sa1, sb1 = c1_ref[...], sa1_ref[...], sb1_ref[...]
    c2, sa2, sb2 = c2_ref[...], sa2_ref[...], sb2_ref[...]
    half_a = DH_A // ROPE_FRAC // 2
    half_i = D_IDX // ROPE_FRAC // 2
    for h in range(H_A):
        sl = slice(h * DH_A, (h + 1) * DH_A)
        q_ref[:, sl] = (_rope(aq_ref[:, sl], c1, sa1, sb1, half_a) * Q_SCALE).astype(BF16)
        kr = _rope(ak_ref[:, sl], c1, sa1, sb1, half_a)
        kf_ref[:, h, :] = kr
        kb_ref[:, sl] = kr.astype(BF16)
        v = av_ref[:, sl]
        vf_ref[:, h, :] = v
        vt_ref[h * VT_ROWS:h * VT_ROWS + DH_A, :] = v.T.astype(BF16)
        vt_ref[h * VT_ROWS + DH_A:(h + 1) * VT_ROWS, :] = jnp.ones((VT_ROWS - DH_A, v.shape[0]), BF16)

    zs = zs_ref[...]
    lane = lax.broadcasted_iota(jnp.int32, zs.shape, 1)
    ikr = _rope(zs, c2, sa2, sb2, half_i)
    ikf_ref[...] = ikr[:, :D_IDX]
    ike_ref[...] = jnp.where(lane < D_IDX, ikr, 0.0).astype(BF16)
    iko_ref[...] = jnp.where(lane >= D_IDX, pltpu.roll(ikr, D_IDX, 1), 0.0).astype(BF16)
    sgn_t = jnp.sign(pltpu.roll(zs, LANE - D_IDX, 1)).T
    sgn_ref[...] = sgn_t[:H_IDX, :]
    wscale = (D_IDX ** -0.5) * (H_IDX ** -0.5)
    for p in range(H_IDX // 2):
        sl = slice(p * LANE, (p + 1) * LANE)
        wa = jnp.abs(zs[:, D_IDX + 2 * p:D_IDX + 2 * p + 1])
        wb = jnp.abs(zs[:, D_IDX + 2 * p + 1:D_IDX + 2 * p + 2])
        wpair = jnp.where(lane < D_IDX, wa, wb) * wscale
        iqw_ref[:, sl] = (_rope(iq_ref[:, sl], c2, sa2, sb2, half_i) * wpair).astype(BF16)


def _prep(z, zs, tabs, tm):
    m = z.shape[0]
    period = tabs[0].shape[0] // tm
    zblk = lambda c: pl.BlockSpec((tm, 1024), lambda i, c=c: (i, c // 1024))
    tab = pl.BlockSpec((tm, LANE), lambda i: (i % period, 0))
    row = lambda w: pl.BlockSpec((tm, w), lambda i: (i, 0))
    col = lambda w: pl.BlockSpec((w, tm), lambda i: (0, i))
    sds = lambda w, dt: jax.ShapeDtypeStruct((m, w), dt)
    heads =pl.BlockSpec((tm, H_A, DH_A), lambda i: (i, 0, 0))
    heads_sds = jax.ShapeDtypeStruct((m, H_A, DH_A), F32)
    return pl.pallas_call(
        _prep_kernel,
        grid=(m // tm,),
        in_specs=[zblk(C_AQ), zblk(C_AK), zblk(C_AV), zblk(C_IQ), row(LANE)] + [tab] * 6,
        out_specs=[row(W_A), heads, row(W_A), heads, col(H_A * VT_ROWS), row(W_IDX), col(H_IDX),
                   row(D_IDX), row(LANE), row(LANE)],
        out_shape=[sds(W_A, BF16), heads_sds, sds(W_A, BF16), heads_sds,
                   jax.ShapeDtypeStruct((H_A * VT_ROWS, m), BF16), sds(W_IDX, BF16),
                   jax.ShapeDtypeStruct((H_IDX, m), F32),
                   sds(D_IDX, F32), sds(LANE, BF16), sds(LANE, BF16)],
        compiler_params=_cparams(("parallel",)),
        name="rope_prep",
    )(z, z, z, z, zs, *tabs)


def _rope_tables(pos, d):
    r = d // ROPE_FRAC
    half = r // 2
    n = pos.shape[0]
    inv = ROPE_THETA ** (-jnp.arange(half, dtype=F32) / half)
    ang = pos.astype(F32)[:, None] * inv[None, :]
    cos, sin = jnp.cos(ang), jnp.sin(ang)
    zh = jnp.zeros((n, half), F32)
    c = jnp.concatenate([cos, cos, jnp.ones((n, d - r), F32)], axis=1)
    sa = jnp.concatenate([-sin, zh, jnp.zeros((n, d - r), F32)], axis=1)
    sb = jnp.concatenate([zh, sin, jnp.zeros((n, d - r), F32)], axis=1)
    reps = LANE // d
    return [jnp.tile(t, (1, reps)) for t in (c, sa, sb)]


KEY_NEG_INF = -2139095041
F32_LOWEST = -3.4028234663852886e38


def _key_to_float(key):
    key = jnp.maximum(key, jnp.int32(KEY_NEG_INF))
    b = key ^ (lax.shift_right_arithmetic(key, 31) & jnp.int32(0x7FFFFFFF))
    return lax.bitcast_convert_type(b, F32)


def _lane_total(part):
    return _dot(part.astype(BF16), jnp.ones((LANE, LANE), BF16))


def _radix_select(count_ge, shape, total, nsel):
    def body(bi, carry):
        pu, cacc = carry
        bit = jnp.int32(31) - bi
        cand_u = pu | lax.shift_left(jnp.int32(1), bit)
        cnt = count_ge(_key_to_float(cand_u ^ jnp.int32(INT_MIN)))
        ok = cnt >= nsel
        return jnp.where(ok, cand_u, pu), jnp.where(ok, cnt, cacc)

    pu0 = jnp.zeros(shape, jnp.int32)
    c0 = jnp.full(shape, total, F32)
    pu, cacc = lax.fori_loop(0, 32, body, (pu0, c0))
    return _key_to_float(pu ^ jnp.int32(INT_MIN)), cacc


def _dsa_prompt_kernel(q_ref, k_ref, vt_ref, iq_ref, sgn_ref, ike_ref, iko_ref, o_ref,
                       key_ref, bias_ref, m_ref, acc_ref, *, tq, nsel):
    tk = tq
    i = pl.program_id(1)
    nch = i + 1
    sub = 8

    kidx0 = lax.broadcasted_iota(jnp.int32, (tk, tq), 0)
    qidx = i * tq + lax.broadcasted_iota(jnp.int32, (tk, tq), 1)

    def chunk_slice(c):
        return pl.ds(pl.multiple_of(c * tk, tk), tk)

    def score_body(c, carry):
        ds = chunk_slice(c)
        ke = ike_ref[ds, :]
        ko = iko_ref[ds, :]
        acc = jnp.zeros((tk, tq), F32)
        for h in range(H_IDX):
            rhs = iq_ref[:, (h // 2) * LANE:(h // 2 + 1) * LANE]
            d = _dot_nt(ke if h % 2 == 0 else ko, rhs)
            acc = acc + sgn_ref[h:h + 1, :] * jnp.maximum(d, 0.0)
        key_ref[ds, :] = jnp.where(kidx0 + c * tk <= qidx, acc, -jnp.inf)
        return carry

    lax.fori_loop(0, nch, score_body, 0)

    def count(cmp_fn):
        def body(c, part):
            hit = jnp.where(cmp_fn(key_ref[chunk_slice(c), :]), 1.0, 0.0)
            return part + jnp.sum(hit.reshape(tk // sub, sub, tq), axis=0)
        part = lax.fori_loop(0, nch, body, jnp.zeros((sub, tq), F32))
        return jnp.sum(part, axis=0, keepdims=True)

    total = (nch * tk).astype(F32)
    tau, c_ge = _radix_select(lambda cand: count(lambda kc: kc >= cand), (1, tq), total, float(nsel))
    tau_eff = jnp.maximum(tau, F32_LOWEST)

    def bias_body(c, carry):
        ds = chunk_slice(c)
        bias_ref[ds, :] = jnp.where(key_ref[ds, :] >= tau_eff, 0.0, NEG)
        return carry

    lax.fori_loop(0, nch, bias_body, 0)

    tie_q = jnp.logical_and(c_ge > float(nsel), tau > -jnp.inf)
    has_tie = jnp.max(jnp.where(tie_q, 1.0, 0.0)) > 0.5

    @pl.when(has_tie)
    def _():
        need = float(nsel) - count(lambda kc: kc > tau)
        lower = jnp.where(lax.broadcasted_iota(jnp.int32, (tk, tk), 1)
                          <= lax.broadcasted_iota(jnp.int32, (tk, tk), 0), 1.0, 0.0).astype(BF16)
        real = tau > -jnp.inf

        def tie_body(c, run):
            ds = chunk_slice(c)
            kc = key_ref[ds, :]
            eqf = jnp.where(kc == tau, 1.0, 0.0)
            pre = _dot(lower, eqf.astype(BF16)) + run
            keep = jnp.where(kc > tau, 1.0, jnp.where(real, eqf * jnp.where(pre <= need, 1.0, 0.0), 0.0))
            bias_ref[ds, :] = jnp.where(keep > 0.5, 0.0, NEG)
            return run + jnp.sum(eqf, axis=0, keepdims=True)

        lax.fori_loop(0, nch, tie_body, jnp.zeros((1, tq), F32))

    m_ref[...] = jnp.full(m_ref.shape, NEG, F32)
    acc_ref[...] = jnp.zeros(acc_ref.shape, F32)

    def attn_body(c, carry):
        ds = chunk_slice(c)
        bias = bias_ref[ds, :]
        heads = [slice(h * DH_A, (h + 1) * DH_A) for h in range(H_A)]
        ss = [_dot_nt(k_ref[ds, hs], q_ref[:, hs]) for hs in heads]
        ps, alphas = [], []
        for h in range(H_A):
            s = ss[h] + bias
            m_prev = m_ref[h]
            m_new = jnp.maximum(m_prev, jnp.max(s, axis=0, keepdims=True))
            alphas.append(jnp.exp2(m_prev - m_new))
            ps.append(jnp.exp2(s - m_new).astype(BF16))
            m_ref[h] = m_new
        for h in range(H_A):
            acc_ref[h] = alphas[h] * acc_ref[h] + _dot(vt_ref[h * VT_ROWS:(h + 1) * VT_ROWS, ds], ps[h])
        return carry

    lax.fori_loop(0, nch, attn_body, 0)
    for h in range(H_A):
        acc = acc_ref[h]
        o_ref[:, h * DH_A:(h + 1) * DH_A] = (acc[:DH_A] / acc[DH_A:DH_A + 1]).T


def _dsa_prompt(q, k, vt, iqw, sgn_t, ike, iko, n_b, n_t, tq):
    nsel = min(TOPK_MAX, n_t // 4)
    nq = n_t // tq
    qblk = lambda w: pl.BlockSpec((tq, w), lambda b, i: (b * nq + i, 0))
    full = lambda w: pl.BlockSpec((n_t, w), lambda b, i: (b, 0))
    return pl.pallas_call(
        functools.partial(_dsa_prompt_kernel, tq=tq, nsel=nsel),
        grid=(n_b, nq),
        in_specs=[qblk(W_A), full(W_A), pl.BlockSpec((H_A * VT_ROWS, n_t), lambda b, i: (0, b)), qblk(W_IDX),
                  pl.BlockSpec((H_IDX, tq), lambda b, i: (0, b * nq + i)), full(LANE), full(LANE)],
        out_specs=qblk(W_A),
        out_shape=jax.ShapeDtypeStruct((n_b * n_t, W_A), F32),
        scratch_shapes=[pltpu.VMEM((n_t, tq), F32),
                        pltpu.VMEM((n_t, tq), F32),
                        pltpu.VMEM((H_A, 1, tq), F32),
                        pltpu.VMEM((H_A, VT_ROWS, tq), F32)],
        compiler_params=_cparams(("parallel", "arbitrary")),
        name="dsa_prompt",
    )(q, k, vt, iqw, sgn_t, ike, iko)


HGRN_BLOCK = 8


def _hgrn_levels(chunk):
    levels, m = [], HGRN_BLOCK
    while m < chunk:
        levels.append(m)
        m *= 2
    return levels


def _hgrn_masks(chunk):
    row, col = np.indices((chunk, chunk))
    masks = []
    for m in _hgrn_levels(chunk):
        same_pair = (row // (2 * m)) == (col // (2 * m))
        masks.append(same_pair & (row % (2 * m) >= m) & (col % (2 * m) < m))
    for j in range(HGRN_BLOCK):
        masks.append((row - col == j) & (row % HGRN_BLOCK >= j))
    return np.stack(masks).astype(np.float32)


def _hgrn_kernel(bq_ref, bf_ref, bi_ref, lb_ref, g_ref, s0_ref, mask_ref, o_ref, s_out_ref, st_ref,
                 *, n_t, chunk, heads, mm_dtype):
    c_ = chunk
    g = g_ref[...]
    for hh in range(heads):
        st_ref[hh] = s0_ref[0, 0, hh].T
    row = lax.broadcasted_iota(jnp.int32, (c_, c_), 0)
    col = lax.broadcasted_iota(jnp.int32, (c_, c_), 1)
    lower = jnp.where(col <= row, 1.0, 0.0)
    levels = _hgrn_levels(c_)

    def head_chunk(hh, ds):
        hs = slice(hh * DK_B, (hh + 1) * DK_B)
        lb = lb_ref[:, hs]
        bq = bq_ref[ds, hs]
        qs = bq * _sigmoid(bq)
        f = lb + (1.0 - lb) * _sigmoid(bf_ref[ds, hs])
        kk = 1.0 - f
        v = bi_ref[ds, hs]
        gcum = _dot(lower, jnp.log(f), precision=lax.Precision.HIGHEST)
        a = jnp.zeros((c_, c_), F32)
        for li, m in enumerate(levels):
            g3 = gcum.reshape(c_ // (2 * m), 2 * m, DK_B)
            ref = jnp.broadcast_to(g3[:, m - 1:m, :], g3.shape).reshape(c_, DK_B)
            qh = qs * jnp.exp(jnp.minimum(gcum - ref, 0.0))
            kh = kk * jnp.exp(jnp.minimum(ref - gcum, 0.0))
            a = a + _dot_nt(qh.astype(mm_dtype), kh.astype(mm_dtype)) * mask_ref[li]
        for j in range(HGRN_BLOCK):
            kj = kk if j == 0 else pltpu.roll(kk, j, 0)
            gj = gcum if j == 0 else pltpu.roll(gcum, j, 0)
            e = jnp.exp(jnp.minimum(gcum - gj, 0.0))
            aj = jnp.sum(qs * kj * e, axis=1, keepdims=True)
            a = a + aj * mask_ref[len(levels) + j]
        st = st_ref[hh]
        o = (_dot_nt((qs * jnp.exp(gcum)).astype(mm_dtype), st.astype(mm_dtype))
             + _dot(a.astype(mm_dtype), v.astype(mm_dtype)))
        glast = gcum[c_ - 1:c_, :]
        kd = kk * jnp.exp(glast - gcum)
        st_ref[hh] = jnp.exp(glast) * st + _dot_tn(v.astype(mm_dtype), kd.astype(mm_dtype))
        o_ref[ds, hs] = o * lax.rsqrt(jnp.mean(o * o, axis=1, keepdims=True) + RMS_EPS) * g

    def chunk_body(ci, carry):
        ds = pl.ds(pl.multiple_of(ci * c_, c_), c_)
        for hh in range(heads):
            head_chunk(hh, ds)
        return carry

    lax.fori_loop(0, n_t // c_, chunk_body, 0)
    for hh in range(heads):
        s_out_ref[0, 0, hh] = st_ref[hh].T


def _hgrn(z, lb, g, s0, layer, n_b, n_t, chunk, heads, mm_dtype):
    w = heads * DK_B
    zcol = lambda c: pl.BlockSpec((n_t, w), lambda b, h, c=c: (b, c // w + h))
    sblk = lambda lyr: pl.BlockSpec((1, 1, heads, DK_B, DV_B), lambda b, h: (lyr, b, h, 0, 0))
    masks = jnp.asarray(_hgrn_masks(chunk))
    return pl.pallas_call(
        functools.partial(_hgrn_kernel, n_t=n_t, chunk=chunk, heads=heads, mm_dtype=mm_dtype),
        grid=(n_b, H_B // heads),
        in_specs=[zcol(C_BQ), zcol(C_BF), zcol(C_BI),
                  pl.BlockSpec((1, w), lambda b, h: (0, h)),
                  pl.BlockSpec((1, DV_B), lambda b, h: (0, 0)),
                  sblk(layer),
                  pl.BlockSpec(masks.shape, lambda b, h: (0, 0, 0))],
        out_specs=[pl.BlockSpec((n_t, w), lambda b, h: (b, h)), sblk(0)],
        out_shape=[jax.ShapeDtypeStruct((n_b * n_t, W_B), F32),
                   jax.ShapeDtypeStruct((1, n_b, H_B, DK_B, DV_B), F32)],
        scratch_shapes=[pltpu.VMEM((heads, DV_B, DK_B), F32)],
        compiler_params=_cparams(("parallel", "arbitrary")),
        name="hgrn2",
    )(z, z, z, lb, g, s0, masks)


def _mem_kv_kernel(x_ref, wk_ref, wv_ref, mk_ref, mv_ref):
    x = x_ref[0].astype(BF16)
    mk = _dot(x, wk_ref[...])
    mv = _dot(x, wv_ref[...])
    for h in range(H_M):
        hs = slice(h * DH_M, (h + 1) * DH_M)
        mk_ref[0, 0, :, h, :] = mk[:, hs]
        mv_ref[0, 0, :, h, :] = mv[:, hs]


def _mem_kv(mem, wk, wv):
    n_b = mem.shape[0]
    wspec = pl.BlockSpec((D_MODEL, W_M), lambda b: (0, 0))
    ospec = pl.BlockSpec((1, 1, N_MEM, H_M, DH_M), lambda b: (0, b, 0, 0, 0))
    sds = jax.ShapeDtypeStruct((1, n_b, N_MEM, H_M, DH_M), F32)
    return pl.pallas_call(
        _mem_kv_kernel,
        grid=(n_b,),
        in_specs=[pl.BlockSpec((1, N_MEM, D_MODEL), lambda b: (b, 0, 0)), wspec, wspec],
        out_specs=[ospec, ospec],
        out_shape=[sds, sds],
        compiler_params=_cparams(("parallel",)),
        name="mem_kv",
    )(mem, wk, wv)


def _mem_kernel(q_ref, mk_ref, mv_ref, o_ref, *, mm_dtype):
    for h in range(H_M):
        hs = slice(h * DH_M, (h + 1) * DH_M)
        q = (q_ref[:, hs] * (DH_M ** -0.5)).astype(mm_dtype)
        s = _dot_nt(q, mk_ref[0, 0, :, h, :].astype(mm_dtype))
        p = jnp.exp(s - jnp.max(s, axis=1, keepdims=True))
        l = jnp.sum(p, axis=1, keepdims=True)
        o_ref[:, hs] = _dot(p.astype(mm_dtype), mv_ref[0, 0, :, h, :].astype(mm_dtype)) / l


def _mem_attend(z, mk, mv, layer, n_b, n_t, tq, mm_dtype):
    nq = n_t // tq
    kv = pl.BlockSpec((1, 1, N_MEM, H_M, DH_M), lambda b, i: (layer, b, 0, 0, 0))
    return pl.pallas_call(
        functools.partial(_mem_kernel, mm_dtype=mm_dtype),
        grid=(n_b, nq),
        in_specs=[pl.BlockSpec((tq, W_M), lambda b, i: (b * nq + i, C_MQ // W_M)), kv, kv],
        out_specs=pl.BlockSpec((tq, W_M), lambda b, i: (b * nq + i, 0)),
        out_shape=jax.ShapeDtypeStruct((n_b * n_t, W_M), F32),
        compiler_params=_cparams(("parallel", "parallel")),
        name="mem_attend",
    )(z, mk, mv)


def _merge_kernel(a_ref, ag_ref, bn_ref, bg_ref, mo_ref, mg_ref, h_ref, wo_ref, lg_ref, lbias_ref, o_ref):
    a = (a_ref[...] * _silu(ag_ref[...])).astype(BF16)
    b = (bn_ref[...] * _silu(bg_ref[...])).astype(BF16)
    m = (mo_ref[...] * _silu(mg_ref[...])).astype(BF16)
    y = (_dot(a, wo_ref[0:W_A, :]) + _dot(b, wo_ref[W_A:W_A + W_B, :])
         + _dot(m, wo_ref[W_A + W_B:D_MODEL, :]))
    r = ALPHA * h_ref[...] + y
    xc = r - jnp.mean(r, axis=1, keepdims=True)
    var = jnp.mean(xc * xc, axis=1, keepdims=True)
    o_ref[...] = xc * lax.rsqrt(var + LN_EPS) * lg_ref[...] + lbias_ref[...]


def _merge(a_out, z, bn, m_out, h, w_out, ln_g, ln_b, tm):
    m = h.shape[0]
    row = lambda w: pl.BlockSpec((tm, w), lambda i: (i, 0))
    zcol = lambda c, w: pl.BlockSpec((tm, w), lambda i, c=c, w=w: (i, c // w))
    const = lambda s: pl.BlockSpec(s, lambda i: (0, 0), pipeline_mode=pl.Buffered(1))
    return pl.pallas_call(
        _merge_kernel,
        grid=(m // tm,),
        in_specs=[row(W_A), zcol(C_AG, W_A), row(W_B), zcol(C_BG, W_B), row(W_M), zcol(C_MG, W_M),
                  row(D_MODEL), const((D_MODEL, D_MODEL)), const((1, D_MODEL)), const((1, D_MODEL))],
        out_specs=row(D_MODEL),
        out_shape=jax.ShapeDtypeStruct((m, D_MODEL), F32),
        compiler_params=_cparams(("parallel",)),
        name="merge",
    )(a_out, z, bn, z, m_out, z, h, w_out, ln_g, ln_b)


PAGES_PER_STEP = 8


def _dsa_sample_select_kernel(pt_ref, pool_ref, lq_ref, sg_ref, iknew_ref, bias_ref, key_ref, page_buf, sem,
                              *, layer, n_t, n_pages, nsel):
    r = pl.program_id(0)
    slot = r % 2
    n_keys_pad = (n_pages + 1) * PAGE_SIZE

    def page_copy(req, p, s):
        return pltpu.make_async_copy(pool_ref.at[layer, pt_ref[req, p]], page_buf.at[s, p], sem.at[s])

    def for_pages(fn):
        def body(p, carry):
            fn(p)
            return carry
        lax.fori_loop(0, n_pages, body, 0)

    @pl.when(r == 0)
    def _():
        for_pages(lambda p: page_copy(0, p, 0).start())

    @pl.when(r + 1 < pl.num_programs(0))
    def _():
        for_pages(lambda p: page_copy(r + 1, p, 1 - slot).start())

    for_pages(lambda p: page_copy(r, p, slot).wait())

    lq = lq_ref[0]
    sg = sg_ref[0]

    def score(ik_t):
        d = _dot(lq, ik_t.astype(BF16))
        r = jnp.maximum(d, 0.0) * jnp.concatenate([sg] * (ik_t.shape[1] // PAGE_SIZE), axis=1)
        s = r[0:n_t]
        for h in range(1, H_IDX):
            s = s + r[h * n_t:(h + 1) * n_t]
        return s

    width = PAGES_PER_STEP * PAGE_SIZE

    def group_body(gi, carry):
        base = gi * PAGES_PER_STEP
        ik_step = jnp.concatenate([page_buf[slot, base + u] for u in range(PAGES_PER_STEP)], axis=1)
        key_ref[:, pl.ds(pl.multiple_of(gi * width, width), width)] = score(ik_step)
        return carry

    lax.fori_loop(0, n_pages // PAGES_PER_STEP, group_body, 0)

    past = n_pages * PAGE_SIZE
    s_new = score(iknew_ref[0])
    t_i = lax.broadcasted_iota(jnp.int32, (n_t, PAGE_SIZE), 0)
    k_i = lax.broadcasted_iota(jnp.int32, (n_t, PAGE_SIZE), 1)
    key_ref[:, past:n_keys_pad] = jnp.where(k_i <= t_i, s_new, -jnp.inf)

    keys = key_ref[...]
    n_tiles = n_keys_pad // LANE

    def count_ge(cand):
        part = jnp.zeros((n_t, LANE), F32)
        for c in range(n_tiles):
            part = part + jnp.where(keys[:, c * LANE:(c + 1) * LANE] >= cand, 1.0, 0.0)
        return _lane_total(part)

    tau_r, c_ge_r = _radix_select(count_ge, (n_t, LANE), float(n_keys_pad), float(nsel))
    tau = tau_r[:, 0:1]
    tau_eff = jnp.maximum(tau, F32_LOWEST)
    bias_ref[0] = jnp.where(keys >= tau_eff, 0.0, NEG)

    tie_rows = jnp.logical_and(c_ge_r > float(nsel), tau_r > -jnp.inf)
    has_tie = jnp.max(jnp.where(tie_rows, 1.0, 0.0)) > 0.5

    @pl.when(has_tie)
    def _():
        c_gt = jnp.sum(jnp.where(keys > tau, 1.0, 0.0), axis=1, keepdims=True)
        need = float(nsel) - c_gt
        ri = lax.broadcasted_iota(jnp.int32, (PAGE_SIZE, PAGE_SIZE), 0)
        ci = lax.broadcasted_iota(jnp.int32, (PAGE_SIZE, PAGE_SIZE), 1)
        upper = jnp.where(ri <= ci, 1.0, 0.0)
        real = tau > -jnp.inf

        def tie_body(c, run):
            ds = pl.ds(pl.multiple_of(c * PAGE_SIZE, PAGE_SIZE), PAGE_SIZE)
            kc = key_ref[:, ds]
            eqf = jnp.where(kc == tau, 1.0, 0.0)
            pre = _dot(eqf, upper) + run
            keep = jnp.where(kc > tau, 1.0, jnp.where(real, eqf * jnp.where(pre <= need, 1.0, 0.0), 0.0))
            bias_ref[0, :, ds] = jnp.where(keep > 0.5, 0.0, NEG)
            return run + jnp.sum(eqf, axis=1, keepdims=True)

        lax.fori_loop(0, n_keys_pad // PAGE_SIZE, tie_body, jnp.zeros((n_t, 1), F32))


def _dsa_sample_select(page_table, idx_pool, layer, lq, sgb, iknew, n_b, n_t, nsel):
    n_pages = page_table.shape[1]
    n_keys_pad = (n_pages + 1) * PAGE_SIZE
    per_req = lambda s: pl.BlockSpec((1,) + s, lambda r, pt: (r, 0, 0))
    return pl.pallas_call(
        functools.partial(_dsa_sample_select_kernel, layer=layer, n_t=n_t, n_pages=n_pages, nsel=nsel),
        grid_spec=pltpu.PrefetchScalarGridSpec(
            num_scalar_prefetch=1,
            grid=(n_b,),
            in_specs=[pl.BlockSpec(memory_space=pl.ANY),
                      per_req((H_IDX * n_t, D_IDX)), per_req((H_IDX * n_t, PAGE_SIZE)), per_req((D_IDX, PAGE_SIZE))],
            out_specs=per_req((n_t, n_keys_pad)),
            scratch_shapes=[pltpu.VMEM((n_t, n_keys_pad), F32),
                            pltpu.VMEM((2, n_pages, D_IDX, PAGE_SIZE), F32),
                            pltpu.SemaphoreType.DMA((2,))]),
        out_shape=jax.ShapeDtypeStruct((n_b, n_t, n_keys_pad), F32),
        compiler_params=_cparams(("arbitrary",)),
        name="dsa_sample_select",
    )(page_table, idx_pool, lq, sgb, iknew)


def _dsa_sample_attend_kernel(pt_ref, *refs, n_t, steps):
    kp = refs[:PAGES_PER_STEP]
    vp = refs[PAGES_PER_STEP:2 * PAGES_PER_STEP]
    qbd_ref, bias_ref, knew_ref, vnew_ref, o_ref, m_ref, l_ref, acc_ref = refs[2 * PAGES_PER_STEP:]
    j = pl.program_id(1)
    rows = H_A * n_t

    @pl.when(j == 0)
    def _():
        m_ref[...] = jnp.full(m_ref.shape, NEG, F32)
        l_ref[...] = jnp.zeros(l_ref.shape, F32)
        acc_ref[...] = jnp.zeros(acc_ref.shape, F32)

    qbd = qbd_ref[0]

    def update(kblk, vblk, bias):
        n = kblk.shape[0]
        s = _dot_nt(qbd, kblk) + jnp.concatenate([bias] * H_A, axis=0)
        m_prev = m_ref[...]
        m_new = jnp.maximum(m_prev, jnp.max(s, axis=1, keepdims=True))
        alpha = jnp.exp2(m_prev - m_new)
        p = jnp.exp2(s - jnp.concatenate([m_new] * (n // LANE), axis=1))
        l_ref[...] = alpha * l_ref[...] + jnp.sum(p, axis=1, keepdims=True)
        pv = _dot(p.astype(BF16), vblk)
        diag = jnp.concatenate(
            [pv[h * n_t:(h + 1) * n_t, h * DH_A:(h + 1) * DH_A] for h in range(H_A)], axis=0)
        acc_ref[...] = alpha * acc_ref[...] + diag
        m_ref[...] = m_new

    def page2d(ref):
        return jnp.concatenate(
            [ref[0, 0, pl.ds(h, PAGE_SIZE, stride=H_A), :] for h in range(H_A)], axis=1).astype(BF16)

    kblk = jnp.concatenate([page2d(kp[u]) for u in range(PAGES_PER_STEP)], axis=0)
    vblk = jnp.concatenate([page2d(vp[u]) for u in range(PAGES_PER_STEP)], axis=0)
    width = PAGES_PER_STEP * PAGE_SIZE
    off = pl.multiple_of(j * width, width)
    update(kblk, vblk, bias_ref[0, :, pl.ds(off, width)])

    @pl.when(j == steps - 1)
    def _():
        past = steps * width
        update(page2d(knew_ref), page2d(vnew_ref), bias_ref[0, :, past:past + PAGE_SIZE])
        o = acc_ref[...] / l_ref[...]
        o_ref[0] = jnp.concatenate([o[h * n_t:(h + 1) * n_t] for h in range(H_A)], axis=1)


def _dsa_sample_attend(page_table, k_pool, v_pool, layer, qbd, bias, knew, vnew, n_b, n_t):
    n_pages = page_table.shape[1]
    steps = n_pages // PAGES_PER_STEP
    n_keys_pad = (n_pages + 1) * PAGE_SIZE
    page_spec = lambda u: pl.BlockSpec(
        (1, 1, PAGE_SIZE * H_A, DH_A), lambda r, j, pt, u=u: (layer, pt[r, j * PAGES_PER_STEP + u], 0, 0))
    per_req = lambda s: pl.BlockSpec((1,) + s, lambda r, j, pt: (r, 0, 0))
    rows = H_A * n_t
    return pl.pallas_call(
        functools.partial(_dsa_sample_attend_kernel, n_t=n_t, steps=steps),
        grid_spec=pltpu.PrefetchScalarGridSpec(
            num_scalar_prefetch=1,
            grid=(n_b, steps),
            in_specs=[page_spec(u) for u in range(PAGES_PER_STEP)] * 2
            + [per_req((rows, W_A)), per_req((n_t, n_keys_pad))]
            + [pl.BlockSpec((1, 1, PAGE_SIZE * H_A, DH_A), lambda r, j, pt: (r, 0, 0, 0))] * 2,
            out_specs=per_req((n_t, W_A)),
            scratch_shapes=[pltpu.VMEM((rows, LANE), F32), pltpu.VMEM((rows, LANE), F32),
                            pltpu.VMEM((rows, DH_A), F32)]),
        out_shape=jax.ShapeDtypeStruct((n_b, n_t, W_A), F32),
        compiler_params=_cparams(("parallel", "arbitrary")),
        name="dsa_sample_attend",
    )(page_table, *([k_pool] * PAGES_PER_STEP), *([v_pool] * PAGES_PER_STEP), qbd, bias, knew, vnew)


def _w_in_kernel(w_ref, main_ref, small_ref):
    o_ik = C_IQ + W_IDX
    o_bq = o_ik + D_IDX + H_IDX
    w = w_ref[0]
    main_ref[:, :o_ik] = w[:, :o_ik].astype(BF16)
    main_ref[:, o_ik:] = w[:, o_bq:].astype(BF16)
    side = jnp.concatenate([w[:, o_ik:o_bq], jnp.zeros((w.shape[0], LANE - D_IDX - H_IDX), F32)], axis=1)
    small_ref[...] = side.astype(BF16)


def _reorder_w_in(w_in, layer, tr=256):
    n_in = w_in.shape[2]
    return pl.pallas_call(
        _w_in_kernel,
        grid=(D_MODEL // tr,),
        in_specs=[pl.BlockSpec((1, tr, n_in), lambda i: (layer, i, 0))],
        out_specs=[pl.BlockSpec((tr, N_MAIN), lambda i: (i, 0)), pl.BlockSpec((tr, LANE), lambda i: (i, 0))],
        out_shape=[jax.ShapeDtypeStruct((D_MODEL, N_MAIN), BF16), jax.ShapeDtypeStruct((D_MODEL, LANE), BF16)],
        compiler_params=_cparams(("parallel",)),
        name="w_in_layout",
    )(w_in)


def kernel(x_prompt, x_sample, mem_prompt, cache_k, cache_v, cache_idx_k, state_hgrn, cache_mem_k,
           cache_mem_v, page_table, w_in, lb_logits, hgrn_norm_g, w_mem_k, w_mem_v, w_out, ln_g, ln_b):
    n_bp, t_p, _ = x_prompt.shape
    n_bs, t_s, _ = x_sample.shape
    n_pages = page_table.shape[1]
    past = n_pages * PAGE_SIZE
    l = 0

    lb_all = jnp.cumsum(jax.nn.softmax(lb_logits.astype(F32), axis=0), axis=0)
    lb = lb_all[l][None, :]
    g_norm = hgrn_norm_g[l][None, :]
    w_main, w_small = _reorder_w_in(w_in, l)
    w_o = w_out[l].astype(BF16)
    lg, lbias = ln_g[l][None, :], ln_b[l][None, :]

    xp = x_prompt.reshape(n_bp * t_p, D_MODEL)
    z, zs = _proj(xp, w_main, w_small, tm=1024, tn=1024)
    pos_p = jnp.arange(t_p, dtype=jnp.int32)
    tabs = _rope_tables(pos_p, DH_A) + _rope_tables(pos_p, D_IDX)
    q, kf, kb, vf, vt, iqw, sgn_t, ikf, ike, iko = _prep(z, zs, tabs, tm=256)
    a_out = _dsa_prompt(q, kb, vt, iqw, sgn_t, ike, iko, n_bp, t_p, tq=256)
    s0 = jnp.zeros((1, n_bp, H_B, DK_B, DV_B), F32)
    bn, s_p = _hgrn(z, lb, g_norm, s0, 0, n_bp, t_p, chunk=128, heads=H_B, mm_dtype=BF16)
    mk_p, mv_p = _mem_kv(mem_prompt, w_mem_k[l].astype(BF16), w_mem_v[l].astype(BF16))
    m_out = _mem_attend(z, mk_p, mv_p, 0, n_bp, t_p, tq=512, mm_dtype=BF16)
    y_p = _merge(a_out, z, bn, m_out, xp, w_o, lg, lbias, tm=256)

    n_s = n_bs * t_s
    xs = x_sample.reshape(n_s, D_MODEL)
    z2, zs2 = _proj(xs, w_main, w_small, tm=n_s, tn=1024)
    pos_s = past + jnp.arange(t_s, dtype=jnp.int32)
    tabs_s = [jnp.tile(t, (n_bs, 1)) for t in _rope_tables(pos_s, DH_A) + _rope_tables(pos_s, D_IDX)]
    q2, kf2, kb2, vf2, vt2, iqw2, sgn_t2, ikf2, ike2, iko2 = _prep(z2, zs2, tabs_s, tm=n_s)
    nsel_s = min(TOPK_MAX, (past + t_s) // 4)
    lq = iqw2.reshape(n_bs, t_s, H_IDX, D_IDX).transpose(0, 2, 1, 3).reshape(n_bs, H_IDX * t_s, D_IDX)
    sgb = jnp.broadcast_to(
        sgn_t2.reshape(H_IDX, n_bs, t_s).transpose(1, 0, 2).reshape(n_bs, H_IDX * t_s, 1),
        (n_bs, H_IDX * t_s, PAGE_SIZE))
    pad_keys = lambda a: jnp.pad(a.reshape((n_bs, t_s) + a.shape[1:]),
                                 ((0, 0), (0, PAGE_SIZE - t_s)) + ((0, 0),) * (a.ndim - 1))
    idx_pool_t = jnp.swapaxes(cache_idx_k, 2, 3)
    bias = _dsa_sample_select(page_table, idx_pool_t, l, lq, sgb, jnp.swapaxes(pad_keys(ikf2), 1, 2),
                              n_bs, t_s, nsel_s)
    q4 = q2.reshape(n_bs, t_s, H_A, DH_A)
    eye = jnp.eye(H_A, dtype=q2.dtype)
    qbd = (q4[:, None, :, :, :] * eye[None, :, None, :, None]).reshape(n_bs, H_A * t_s, W_A)
    n_layers, n_phys = cache_k.shape[:2]
    as_rows = lambda a: a.reshape(a.shape[0], a.shape[1], PAGE_SIZE * H_A, DH_A)
    a_out2 = _dsa_sample_attend(page_table, as_rows(cache_k), as_rows(cache_v), l, qbd, bias,
                                as_rows(pad_keys(kf2)[:, None]), as_rows(pad_keys(vf2)[:, None]), n_bs, t_s)
    bn2, s_s = _hgrn(z2, lb, g_norm, state_hgrn, l, n_bs, t_s, chunk=t_s, heads=H_B, mm_dtype=F32)
    m_out2 = _mem_attend(z2, cache_mem_k, cache_mem_v, l, n_bs, t_s, tq=t_s, mm_dtype=F32)
    y_s = _merge(a_out2.reshape(n_s, W_A), z2, bn2, m_out2, xs, w_o, lg, lbias, tm=n_s)

    return (y_p.reshape(n_bp, t_p, D_MODEL), y_s.reshape(n_bs, t_s, D_MODEL),
            kf.reshape(1, n_bp, t_p, H_A, DH_A), vf.reshape(1, n_bp, t_p, H_A, DH_A),
            ikf.reshape(1, n_bp, t_p, D_IDX), s_p, mk_p, mv_p,
            kf2.reshape(1, n_bs, t_s, H_A, DH_A), vf2.reshape(1, n_bs, t_s, H_A, DH_A),
            ikf2.reshape(1, n_bs, t_s, D_IDX), s_s)
```

```python
import functools

import numpy as np
import jax
import jax.numpy as jnp
from jax import lax
from jax.experimental import pallas as pl
from jax.experimental.pallas import tpu as pltpu

D_MODEL = 2048
DH_A = 128
H_A = 8
W_A = H_A * DH_A
H_IDX = 16
D_IDX = 64
W_IDX = H_IDX * D_IDX
TOPK_MAX = 256
DK_B = 128
DV_B = 128
H_B = 4
W_B = H_B * DV_B
DH_M = 128
H_M = 4
W_M = H_M * DH_M
N_MEM = 256
PAGE_SIZE = 128
ROPE_THETA = 500000.0
ROPE_FRAC = 4
LN_EPS = 1e-5
RMS_EPS = 1e-6
DEPTH = 1
ALPHA = (2.0 * DEPTH) ** 0.25

LANE = 128
VMEM_LIMIT = 48 * 1024 * 1024

C_AQ, C_AK, C_AV, C_AG, C_IQ = 0, 1024, 2048, 3072, 4096
C_BQ, C_BF, C_BI, C_BG, C_MQ, C_MG = 5120, 5632, 6144, 6656, 7168, 7680
N_MAIN = 8192

NEG = -1e30
Q_SCALE = DH_A ** -0.5 * 1.4426950408889634
VT_ROWS = DH_A + 16
INT_MIN = -(2 ** 31)
BF16 = jnp.bfloat16
F32 = jnp.float32


def _dot(a, b, **kw):
    return jnp.dot(a, b, preferred_element_type=F32, **kw)


def _dot_nt(a, b):
    return lax.dot_general(a, b, (((1,), (1,)), ((), ())), preferred_element_type=F32)


def _dot_tn(a, b):
    return lax.dot_general(a, b, (((0,), (0,)), ((), ())), preferred_element_type=F32)


def _sigmoid(x):
    return 1.0 / (1.0 + jnp.exp(-x))


def _silu(x):
    return x * _sigmoid(x)


def _cparams(sem):
    return pltpu.CompilerParams(dimension_semantics=sem, vmem_limit_bytes=VMEM_LIMIT)


def _proj_kernel(x_ref, w_ref, ws_ref, o_ref, os_ref, xb_ref):
    j = pl.program_id(1)

    @pl.when(j == 0)
    def _():
        xb = x_ref[...].astype(BF16)
        xb_ref[...] = xb
        os_ref[...] = _dot_nt(xb, ws_ref[...])

    o_ref[...] = _dot_nt(xb_ref[...], w_ref[...])


W_SIDE_ROW = C_IQ + W_IDX
W_SIDE_ROWS = D_IDX + H_IDX
N_IN = N_MAIN + W_SIDE_ROWS


def _proj(x, w_t, tm, tn):
    m = x.shape[0]
    n_lo = W_SIDE_ROW // tn

    def w_rows(i, j):
        return ((j * (tn // 16) + jnp.where(j < n_lo, 0, W_SIDE_ROWS // 16)) * 16, 0)

    return pl.pallas_call(
        _proj_kernel,
        grid=(m // tm, N_MAIN // tn),
        in_specs=[pl.BlockSpec((tm, D_MODEL), lambda i, j: (i, 0)),
                  pl.BlockSpec((pl.Element(tn), pl.Element(D_MODEL)), w_rows),
                  pl.BlockSpec((pl.Element(LANE), pl.Element(D_MODEL)), lambda i, j: (W_SIDE_ROW, 0))],
        out_specs=[pl.BlockSpec((tm, tn), lambda i, j: (i, j)),
                   pl.BlockSpec((tm, LANE), lambda i, j: (i, 0))],
        out_shape=[jax.ShapeDtypeStruct((m, N_MAIN), F32),
                   jax.ShapeDtypeStruct((m, LANE), F32)],
        scratch_shapes=[pltpu.VMEM((tm, D_MODEL), BF16)],
        compiler_params=_cparams(("parallel", "arbitrary")),
        name="in_proj",
    )(x, w_t, w_t)


def _cast_kernel(x_ref, o_ref):
    o_ref[...] = x_ref[0].astype(BF16)


def _cast_rows(w, layer, tr):
    rows = w.shape[1]
    return pl.pallas_call(
        _cast_kernel,
        grid=(pl.cdiv(rows, tr),),
        in_specs=[pl.BlockSpec((1, tr, D_MODEL), lambda i: (layer, i, 0))],
        out_specs=pl.BlockSpec((tr, D_MODEL), lambda i: (i, 0)),
        out_shape=jax.ShapeDtypeStruct((rows, D_MODEL), BF16),
        compiler_params=_cparams(("parallel",)),
        name="w_cast",
    )(w)


def _rope(x, c, sa, sb, half):
    return x * c + pltpu.roll(x, LANE - half, 1) * sa + pltpu.roll(x, half, 1) * sb


def _prep_kernel(aq_ref, ak_ref, av_ref, iq_ref, zs_ref,
                 c1_ref, sa1_ref, sb1_ref, c2_ref, sa2_ref, sb2_ref,
                 q_ref, kf_ref, kb_ref, vf_ref, vt_ref, iqw_ref, sgn_ref,
                 ikf_ref, ike_ref, iko_ref):
    c1, sa1, sb1 = c1_ref[...], sa1_ref[...], sb1_ref[...]
    c2, sa2, sb2 = c2_ref[...], sa2_ref[...], sb2_ref[...]
    half_a = DH_A // ROPE_FRAC // 2
    half_i = D_IDX // ROPE_FRAC // 2
    for h in range(H_A):
        sl = slice(h * DH_A, (h + 1) * DH_A)
        q_ref[:, sl] = (_rope(aq_ref[:, sl], c1, sa1, sb1, half_a) * Q_SCALE).astype(BF16)
        kr = _rope(ak_ref[:, sl], c1, sa1, sb1, half_a)
        kf_ref[:, h, :] = kr
        kb_ref[:, sl] = kr.astype(BF16)
        v = av_ref[:, sl]
        vf_ref[:, h, :] = v
        vt_ref[h * VT_ROWS:h * VT_ROWS + DH_A, :] = v.T.astype(BF16)
        vt_ref[h * VT_ROWS + DH_A:(h + 1) * VT_ROWS, :] = jnp.ones((VT_ROWS - DH_A, v.shape[0]), BF16)

    zs = zs_ref[...]
    lane = lax.broadcasted_iota(jnp.int32, zs.shape, 1)
    ikr = _rope(zs, c2, sa2, sb2, half_i)
    ikf_ref[...] = ikr[:, :D_IDX]
    ike_ref[...] = jnp.where(lane < D_IDX, ikr, 0.0).astype(BF16)
    iko_ref[...] = jnp.where(lane >= D_IDX, pltpu.roll(ikr, D_IDX, 1), 0.0).astype(BF16)
    sgn_t = jnp.sign(pltpu.roll(zs, LANE - D_IDX, 1)).T
    sgn_ref[...] = sgn_t[:H_IDX, :]
    wscale = (D_IDX ** -0.5) * (H_IDX ** -0.5)
    for p in range(H_IDX // 2):
        sl = slice(p * LANE, (p + 1) * LANE)
        wa = jnp.abs(zs[:, D_IDX + 2 * p:D_IDX + 2 * p + 1])
        wb = jnp.abs(zs[:, D_IDX + 2 * p + 1:D_IDX + 2 * p + 2])
        wpair = jnp.where(lane < D_IDX, wa, wb) * wscale
        iqw_ref[:, sl] = (_rope(iq_ref[:, sl], c2, sa2, sb2, half_i) * wpair).astype(BF16)


def _prep(z, zs, tabs, tm):
    m = z.shape[0]
    period = tabs[0].shape[0] // tm
    zblk = lambda c: pl.BlockSpec((tm, 1024), lambda i, c=c: (i, c // 1024))
    tab = pl.BlockSpec((tm, LANE), lambda i: (i % period, 0))
    row = lambda w: pl.BlockSpec((tm, w), lambda i: (i, 0))
    col = lambda w: pl.BlockSpec((w, tm), lambda i: (0, i))
    sds = lambda w, dt: jax.ShapeDtypeStruct((m, w), dt)
    heads =pl.BlockSpec((tm, H_A, DH_A), lambda i: (i, 0, 0))
    heads_sds = jax.ShapeDtypeStruct((m, H_A, DH_A), F32)
    return pl.pallas_call(
        _prep_kernel,
        grid=(m // tm,),
        in_specs=[zblk(C_AQ), zblk(C_AK), zblk(C_AV), zblk(C_IQ), row(LANE)] + [tab] * 6,
        out_specs=[row(W_A), heads, row(W_A), heads, col(H_A * VT_ROWS), row(W_IDX), col(H_IDX),
                   row(D_IDX), row(LANE), row(LANE)],
        out_shape=[sds(W_A, BF16), heads_sds, sds(W_A, BF16), heads_sds,
                   jax.ShapeDtypeStruct((H_A * VT_ROWS, m), BF16), sds(W_IDX, BF16),
                   jax.ShapeDtypeStruct((H_IDX, m), F32),
                   sds(D_IDX, F32), sds(LANE, BF16), sds(LANE, BF16)],
        compiler_params=_cparams(("parallel",)),
        name="rope_prep",
    )(z, z, z, z, zs, *tabs)


def _rope_tables(pos, d):
    r = d // ROPE_FRAC
    half = r // 2
    n = pos.shape[0]
    inv = ROPE_THETA ** (-jnp.arange(half, dtype=F32) / half)
    ang = pos.astype(F32)[:, None] * inv[None, :]
    cos, sin = jnp.cos(ang), jnp.sin(ang)
    zh = jnp.zeros((n, half), F32)
    c = jnp.concatenate([cos, cos, jnp.ones((n, d - r), F32)], axis=1)
    sa = jnp.concatenate([-sin, zh, jnp.zeros((n, d - r), F32)], axis=1)
    sb = jnp.concatenate([zh, sin, jnp.zeros((n, d - r), F32)], axis=1)
    reps = LANE // d
    return [jnp.tile(t, (1, reps)) for t in (c, sa, sb)]


KEY_NEG_INF = -2139095041
F32_LOWEST = -3.4028234663852886e38


def _key_to_float(key):
    key = jnp.maximum(key, jnp.int32(KEY_NEG_INF))
    b = key ^ (lax.shift_right_arithmetic(key, 31) & jnp.int32(0x7FFFFFFF))
    return lax.bitcast_convert_type(b, F32)


def _lane_total(part):
    return _dot(part.astype(BF16), jnp.ones((LANE, LANE), BF16))


def _radix_select(count_ge, shape, total, nsel):
    def body(bi, carry):
        pu, cacc = carry
        bit = jnp.int32(31) - bi
        cand_u = pu | lax.shift_left(jnp.int32(1), bit)
        cnt = count_ge(_key_to_float(cand_u ^ jnp.int32(INT_MIN)))
        ok = cnt >= nsel
        return jnp.where(ok, cand_u, pu), jnp.where(ok, cnt, cacc)

    pu0 = jnp.zeros(shape, jnp.int32)
    c0 = jnp.full(shape, total, F32)
    pu, cacc = lax.fori_loop(0, 32, body, (pu0, c0))
    return _key_to_float(pu ^ jnp.int32(INT_MIN)), cacc


def _dsa_prompt_kernel(q_ref, k_ref, vt_ref, iq_ref, sgn_ref, ike_ref, iko_ref, o_ref,
                       key_ref, bias_ref, m_ref, acc_ref, *, tq, nsel):
    tk = tq
    i = pl.program_id(1)
    nch = i + 1
    sub = 8

    kidx0 = lax.broadcasted_iota(jnp.int32, (tk, tq), 0)
    qidx = i * tq + lax.broadcasted_iota(jnp.int32, (tk, tq), 1)

    def chunk_slice(c):
        return pl.ds(pl.multiple_of(c * tk, tk), tk)

    def score_body(c, carry):
        ds = chunk_slice(c)
        ke = ike_ref[ds, :]
        ko = iko_ref[ds, :]
        acc = jnp.zeros((tk, tq), F32)
        for h in range(H_IDX):
            rhs = iq_ref[:, (h // 2) * LANE:(h // 2 + 1) * LANE]
            d = _dot_nt(ke if h % 2 == 0 else ko, rhs)
            acc = acc + sgn_ref[h:h + 1, :] * jnp.maximum(d, 0.0)
        key_ref[ds, :] = jnp.where(kidx0 + c * tk <= qidx, acc, -jnp.inf)
        return carry

    lax.fori_loop(0, nch, score_body, 0)

    def count(cmp_fn):
        def body(c, part):
            hit = jnp.where(cmp_fn(key_ref[chunk_slice(c), :]), 1.0, 0.0)
            return part + jnp.sum(hit.reshape(tk // sub, sub, tq), axis=0)
        part = lax.fori_loop(0, nch, body, jnp.zeros((sub, tq), F32))
        return jnp.sum(part, axis=0, keepdims=True)

    total = (nch * tk).astype(F32)
    tau, c_ge = _radix_select(lambda cand: count(lambda kc: kc >= cand), (1, tq), total, float(nsel))
    tau_eff = jnp.maximum(tau, F32_LOWEST)

    def bias_body(c, carry):
        ds = chunk_slice(c)
        bias_ref[ds, :] = jnp.where(key_ref[ds, :] >= tau_eff, 0.0, NEG)
        return carry

    lax.fori_loop(0, nch, bias_body, 0)

    tie_q = jnp.logical_and(c_ge > float(nsel), tau > -jnp.inf)
    has_tie = jnp.max(jnp.where(tie_q, 1.0, 0.0)) > 0.5

    @pl.when(has_tie)
    def _():
        need = float(nsel) - count(lambda kc: kc > tau)
        lower = jnp.where(lax.broadcasted_iota(jnp.int32, (tk, tk), 1)
                          <= lax.broadcasted_iota(jnp.int32, (tk, tk), 0), 1.0, 0.0).astype(BF16)
        real = tau > -jnp.inf

        def tie_body(c, run):
            ds = chunk_slice(c)
            kc = key_ref[ds, :]
            eqf = jnp.where(kc == tau, 1.0, 0.0)
            pre = _dot(lower, eqf.astype(BF16)) + run
            keep = jnp.where(kc > tau, 1.0, jnp.where(real, eqf * jnp.where(pre <= need, 1.0, 0.0), 0.0))
            bias_ref[ds, :] = jnp.where(keep > 0.5, 0.0, NEG)
            return run + jnp.sum(eqf, axis=0, keepdims=True)

        lax.fori_loop(0, nch, tie_body, jnp.zeros((1, tq), F32))

    m_ref[...] = jnp.full(m_ref.shape, NEG, F32)
    acc_ref[...] = jnp.zeros(acc_ref.shape, F32)

    def attn_body(c, carry):
        ds = chunk_slice(c)
        bias = bias_ref[ds, :]
        heads = [slice(h * DH_A, (h + 1) * DH_A) for h in range(H_A)]
        ss = [_dot_nt(k_ref[ds, hs], q_ref[:, hs]) for hs in heads]
        ps, alphas = [], []
        for h in range(H_A):
            s = ss[h] + bias
            m_prev = m_ref[h]
            m_new = jnp.maximum(m_prev, jnp.max(s, axis=0, keepdims=True))
            alphas.append(jnp.exp2(m_prev - m_new))
            ps.append(jnp.exp2(s - m_new).astype(BF16))
            m_ref[h] = m_new
        for h in range(H_A):
            acc_ref[h] = alphas[h] * acc_ref[h] + _dot(vt_ref[h * VT_ROWS:(h + 1) * VT_ROWS, ds], ps[h])
        return carry

    lax.fori_loop(0, nch, attn_body, 0)
    for h in range(H_A):
        acc = acc_ref[h]
        o_ref[:, h * DH_A:(h + 1) * DH_A] = (acc[:DH_A] / acc[DH_A:DH_A + 1]).T


def _dsa_prompt(q, k, vt, iqw, sgn_t, ike, iko, n_b, n_t, tq):
    nsel = min(TOPK_MAX, n_t // 4)
    nq = n_t // tq
    qblk = lambda w: pl.BlockSpec((tq, w), lambda b, i: (b * nq + i, 0))
    full = lambda w: pl.BlockSpec((n_t, w), lambda b, i: (b, 0))
    return pl.pallas_call(
        functools.partial(_dsa_prompt_kernel, tq=tq, nsel=nsel),
        grid=(n_b, nq),
        in_specs=[qblk(W_A), full(W_A), pl.BlockSpec((H_A * VT_ROWS, n_t), lambda b, i: (0, b)), qblk(W_IDX),
                  pl.BlockSpec((H_IDX, tq), lambda b, i: (0, b * nq + i)), full(LANE), full(LANE)],
        out_specs=qblk(W_A),
        out_shape=jax.ShapeDtypeStruct((n_b * n_t, W_A), F32),
        scratch_shapes=[pltpu.VMEM((n_t, tq), F32),
                        pltpu.VMEM((n_t, tq), F32),
                        pltpu.VMEM((H_A, 1, tq), F32),
                        pltpu.VMEM((H_A, VT_ROWS, tq), F32)],
        compiler_params=_cparams(("parallel", "arbitrary")),
        name="dsa_prompt",
    )(q, k, vt, iqw, sgn_t, ike, iko)


HGRN_BLOCK = 8


def _hgrn_levels(chunk):
    levels, m = [], HGRN_BLOCK
    while m < chunk:
        levels.append(m)
        m *= 2
    return levels


def _hgrn_masks(chunk):
    row, col = np.indices((chunk, chunk))
    masks = []
    for m in _hgrn_levels(chunk):
        same_pair = (row // (2 * m)) == (col // (2 * m))
        masks.append(same_pair & (row % (2 * m) >= m) & (col % (2 * m) < m))
    for j in range(HGRN_BLOCK):
        masks.append((row - col == j) & (row % HGRN_BLOCK >= j))
    return np.stack(masks).astype(np.float32)


def _hgrn_kernel(bq_ref, bf_ref, bi_ref, lb_ref, g_ref, s0_ref, mask_ref, o_ref, s_out_ref, st_ref,
                 *, n_t, chunk, heads, mm_dtype):
    c_ = chunk
    g = g_ref[...]
    for hh in range(heads):
        st_ref[hh] = s0_ref[0, 0, hh].T
    row = lax.broadcasted_iota(jnp.int32, (c_, c_), 0)
    col = lax.broadcasted_iota(jnp.int32, (c_, c_), 1)
    lower = jnp.where(col <= row, 1.0, 0.0)
    levels = _hgrn_levels(c_)

    def head_chunk(hh, ds):
        hs = slice(hh * DK_B, (hh + 1) * DK_B)
        lb = lb_ref[:, hs]
        bq = bq_ref[ds, hs]
        qs = bq * _sigmoid(bq)
        f = lb + (1.0 - lb) * _sigmoid(bf_ref[ds, hs])
        kk = 1.0 - f
        v = bi_ref[ds, hs]
        gcum = _dot(lower, jnp.log2(f), precision=lax.Precision.HIGHEST)
        a = jnp.zeros((c_, c_), F32)
        for li, m in enumerate(levels):
            g3 = gcum.reshape(c_ // (2 * m), 2 * m, DK_B)
            ref = jnp.broadcast_to(g3[:, m - 1:m, :], g3.shape).reshape(c_, DK_B)
            qh = qs * jnp.exp2(jnp.minimum(gcum - ref, 0.0))
            kh = kk * jnp.exp2(jnp.minimum(ref - gcum, 0.0))
            a = a + _dot_nt(qh.astype(mm_dtype), kh.astype(mm_dtype)) * mask_ref[li]
        for j in range(HGRN_BLOCK):
            kj = kk if j == 0 else pltpu.roll(kk, j, 0)
            gj = gcum if j == 0 else pltpu.roll(gcum, j, 0)
            e = jnp.exp2(jnp.minimum(gcum - gj, 0.0))
            aj = jnp.sum(qs * kj * e, axis=1, keepdims=True)
            a = a + aj * mask_ref[len(levels) + j]
        st = st_ref[hh]
        o = (_dot_nt((qs * jnp.exp2(gcum)).astype(mm_dtype), st.astype(mm_dtype))
             + _dot(a.astype(mm_dtype), v.astype(mm_dtype)))
        glast = gcum[c_ - 1:c_, :]
        kd = kk * jnp.exp2(glast - gcum)
        st_ref[hh] = jnp.exp2(glast) * st + _dot_tn(v.astype(mm_dtype), kd.astype(mm_dtype))
        o_ref[ds, hs] = o * lax.rsqrt(jnp.mean(o * o, axis=1, keepdims=True) + RMS_EPS) * g

    def chunk_body(ci, carry):
        ds = pl.ds(pl.multiple_of(ci * c_, c_), c_)
        for hh in range(heads):
            head_chunk(hh, ds)
        return carry

    lax.fori_loop(0, n_t // c_, chunk_body, 0)
    for hh in range(heads):
        s_out_ref[0, 0, hh] = st_ref[hh].T


def _hgrn(z, lb, g, s0, layer, n_b, n_t, chunk, heads, mm_dtype):
    w = heads * DK_B
    zcol = lambda c: pl.BlockSpec((n_t, w), lambda b, h, c=c: (b, c // w + h))
    sblk = lambda lyr: pl.BlockSpec((1, 1, heads, DK_B, DV_B), lambda b, h: (lyr, b, h, 0, 0))
    masks = jnp.asarray(_hgrn_masks(chunk))
    return pl.pallas_call(
        functools.partial(_hgrn_kernel, n_t=n_t, chunk=chunk, heads=heads, mm_dtype=mm_dtype),
        grid=(n_b, H_B // heads),
        in_specs=[zcol(C_BQ), zcol(C_BF), zcol(C_BI),
                  pl.BlockSpec((1, w), lambda b, h: (0, h)),
                  pl.BlockSpec((1, DV_B), lambda b, h: (0, 0)),
                  sblk(layer),
                  pl.BlockSpec(masks.shape, lambda b, h: (0, 0, 0))],
        out_specs=[pl.BlockSpec((n_t, w), lambda b, h: (b, h)), sblk(0)],
        out_shape=[jax.ShapeDtypeStruct((n_b * n_t, W_B), F32),
                   jax.ShapeDtypeStruct((1, n_b, H_B, DK_B, DV_B), F32)],
        scratch_shapes=[pltpu.VMEM((heads, DV_B, DK_B), F32)],
        compiler_params=_cparams(("parallel", "arbitrary")),
        name="hgrn2",
    )(z, z, z, lb, g, s0, masks)


def _mem_kv_kernel(x_ref, wk_ref, wv_ref, mk_ref, mv_ref):
    x = x_ref[0].astype(BF16)
    mk = _dot(x, wk_ref[...])
    mv = _dot(x, wv_ref[...])
    for h in range(H_M):
        hs = slice(h * DH_M, (h + 1) * DH_M)
        mk_ref[0, 0, :, h, :] = mk[:, hs]
        mv_ref[0, 0, :, h, :] = mv[:, hs]


def _mem_kv(mem, wk, wv):
    n_b = mem.shape[0]
    wspec = pl.BlockSpec((D_MODEL, W_M), lambda b: (0, 0))
    ospec = pl.BlockSpec((1, 1, N_MEM, H_M, DH_M), lambda b: (0, b, 0, 0, 0))
    sds = jax.ShapeDtypeStruct((1, n_b, N_MEM, H_M, DH_M), F32)
    return pl.pallas_call(
        _mem_kv_kernel,
        grid=(n_b,),
        in_specs=[pl.BlockSpec((1, N_MEM, D_MODEL), lambda b: (b, 0, 0)), wspec, wspec],
        out_specs=[ospec, ospec],
        out_shape=[sds, sds],
        compiler_params=_cparams(("parallel",)),
        name="mem_kv",
    )(mem, wk, wv)


def _mem_kernel(q_ref, mk_ref, mv_ref, o_ref, *, mm_dtype):
    for h in range(H_M):
        hs = slice(h * DH_M, (h + 1) * DH_M)
        q = (q_ref[:, hs] * (DH_M ** -0.5)).astype(mm_dtype)
        s = _dot_nt(q, mk_ref[0, 0, :, h, :].astype(mm_dtype))
        p = jnp.exp(s - jnp.max(s, axis=1, keepdims=True))
        l = jnp.sum(p, axis=1, keepdims=True)
        o_ref[:, hs] = _dot(p.astype(mm_dtype), mv_ref[0, 0, :, h, :].astype(mm_dtype)) / l


def _mem_attend(z, mk, mv, layer, n_b, n_t, tq, mm_dtype):
    nq = n_t // tq
    kv = pl.BlockSpec((1, 1, N_MEM, H_M, DH_M), lambda b, i: (layer, b, 0, 0, 0))
    return pl.pallas_call(
        functools.partial(_mem_kernel, mm_dtype=mm_dtype),
        grid=(n_b, nq),
        in_specs=[pl.BlockSpec((tq, W_M), lambda b, i: (b * nq + i, C_MQ // W_M)), kv, kv],
        out_specs=pl.BlockSpec((tq, W_M), lambda b, i: (b * nq + i, 0)),
        out_shape=jax.ShapeDtypeStruct((n_b * n_t, W_M), F32),
        compiler_params=_cparams(("parallel", "parallel")),
        name="mem_attend",
    )(z, mk, mv)


def _merge_kernel(a_ref, ag_ref, bn_ref, bg_ref, mo_ref, mg_ref, h_ref, wo_ref, lg_ref, lbias_ref, o_ref):
    a = (a_ref[...] * _silu(ag_ref[...])).astype(BF16)
    b = (bn_ref[...] * _silu(bg_ref[...])).astype(BF16)
    m = (mo_ref[...] * _silu(mg_ref[...])).astype(BF16)
    y = (_dot(a, wo_ref[0:W_A, :]) + _dot(b, wo_ref[W_A:W_A + W_B, :])
         + _dot(m, wo_ref[W_A + W_B:D_MODEL, :]))
    r = ALPHA * h_ref[...] + y
    xc = r - jnp.mean(r, axis=1, keepdims=True)
    var = jnp.mean(xc * xc, axis=1, keepdims=True)
    o_ref[...] = xc * lax.rsqrt(var + LN_EPS) * lg_ref[...] + lbias_ref[...]


def _merge(a_out, z, bn, m_out, h, w_out, ln_g, ln_b, tm):
    m = h.shape[0]
    row = lambda w: pl.BlockSpec((tm, w), lambda i: (i, 0))
    zcol = lambda c, w: pl.BlockSpec((tm, w), lambda i, c=c, w=w: (i, c // w))
    const = lambda s: pl.BlockSpec(s, lambda i: (0, 0), pipeline_mode=pl.Buffered(1))
    return pl.pallas_call(
        _merge_kernel,
        grid=(m // tm,),
        in_specs=[row(W_A), zcol(C_AG, W_A), row(W_B), zcol(C_BG, W_B), row(W_M), zcol(C_MG, W_M),
                  row(D_MODEL), const((D_MODEL, D_MODEL)), const((1, D_MODEL)), const((1, D_MODEL))],
        out_specs=row(D_MODEL),
        out_shape=jax.ShapeDtypeStruct((m, D_MODEL), F32),
        compiler_params=_cparams(("parallel",)),
        name="merge",
    )(a_out, z, bn, z, m_out, z, h, w_out, ln_g, ln_b)


PAGES_PER_STEP = 8


def _dsa_sample_select_kernel(pt_ref, pool_ref, lq_ref, sg_ref, iknew_ref, bias_ref, key_ref, page_buf, sem,
                              *, layer, n_t, n_pages, nsel):
    r = pl.program_id(0)
    slot = r % 2
    n_keys_pad = (n_pages + 1) * PAGE_SIZE

    def page_copy(req, p, s):
        return pltpu.make_async_copy(pool_ref.at[layer, pt_ref[req, p]], page_buf.at[s, p], sem.at[s])

    def for_pages(fn):
        def body(p, carry):
            fn(p)
            return carry
        lax.fori_loop(0, n_pages, body, 0)

    @pl.when(r == 0)
    def _():
        for_pages(lambda p: page_copy(0, p, 0).start())

    @pl.when(r + 1 < pl.num_programs(0))
    def _():
        for_pages(lambda p: page_copy(r + 1, p, 1 - slot).start())

    for_pages(lambda p: page_copy(r, p, slot).wait())

    lq = lq_ref[0]
    sg = sg_ref[0]

    def score(ik_t):
        d = _dot(lq, ik_t.astype(BF16))
        r = jnp.maximum(d, 0.0) * jnp.concatenate([sg] * (ik_t.shape[1] // PAGE_SIZE), axis=1)
        s = r[0:n_t]
        for h in range(1, H_IDX):
            s = s + r[h * n_t:(h + 1) * n_t]
        return s

    width = PAGES_PER_STEP * PAGE_SIZE

    rows = pl.ds(pl.multiple_of(r * n_t, n_t), n_t)

    def group_body(gi, carry):
        base = gi * PAGES_PER_STEP
        ik_step = jnp.concatenate([page_buf[slot, base + u] for u in range(PAGES_PER_STEP)], axis=1)
        key_ref[rows, pl.ds(pl.multiple_of(gi * width, width), width)] = score(ik_step)
        return carry

    lax.fori_loop(0, n_pages // PAGES_PER_STEP, group_body, 0)

    past = n_pages * PAGE_SIZE
    s_new = score(iknew_ref[0])
    t_i = lax.broadcasted_iota(jnp.int32, (n_t, PAGE_SIZE), 0)
    k_i = lax.broadcasted_iota(jnp.int32, (n_t, PAGE_SIZE), 1)
    key_ref[rows, past:n_keys_pad] = jnp.where(k_i <= t_i, s_new, -jnp.inf)

    @pl.when(r == pl.num_programs(0) - 1)
    def _():
        n_rows = key_ref.shape[0]
        n_tiles = n_keys_pad // LANE

        def count_ge(cand):
            part = jnp.zeros((n_rows, LANE), F32)
            for c in range(n_tiles):
                part = part + jnp.where(key_ref[:, c * LANE:(c + 1) * LANE] >= cand, 1.0, 0.0)
            return _lane_total(part)

        tau_r, c_ge_r = _radix_select(count_ge, (n_rows, LANE), float(n_keys_pad), float(nsel))
        tau = tau_r[:, 0:1]
        tau_eff = jnp.maximum(tau, F32_LOWEST)
        bias_ref[...] = jnp.where(key_ref[...] >= tau_eff, 0.0, NEG)

        tie_rows = jnp.logical_and(c_ge_r > float(nsel), tau_r > -jnp.inf)
        has_tie = jnp.max(jnp.where(tie_rows, 1.0, 0.0)) > 0.5

        @pl.when(has_tie)
        def _():
            c_gt = jnp.sum(jnp.where(key_ref[...] > tau, 1.0, 0.0), axis=1, keepdims=True)
            need = float(nsel) - c_gt
            ri = lax.broadcasted_iota(jnp.int32, (PAGE_SIZE, PAGE_SIZE), 0)
            ci = lax.broadcasted_iota(jnp.int32, (PAGE_SIZE, PAGE_SIZE), 1)
            upper = jnp.where(ri <= ci, 1.0, 0.0)
            real = tau > -jnp.inf

            def tie_body(c, run):
                ds = pl.ds(pl.multiple_of(c * PAGE_SIZE, PAGE_SIZE), PAGE_SIZE)
                kc = key_ref[:, ds]
                eqf = jnp.where(kc == tau, 1.0, 0.0)
                pre = _dot(eqf, upper) + run
                keep = jnp.where(kc > tau, 1.0, jnp.where(real, eqf * jnp.where(pre <= need, 1.0, 0.0), 0.0))
                bias_ref[:, ds] = jnp.where(keep > 0.5, 0.0, NEG)
                return run + jnp.sum(eqf, axis=1, keepdims=True)

            lax.fori_loop(0, n_keys_pad // PAGE_SIZE, tie_body, jnp.zeros((n_rows, 1), F32))


def _dsa_sample_select(page_table, idx_pool, layer, lq, sgb, iknew, n_b, n_t, nsel):
    n_pages = page_table.shape[1]
    n_keys_pad = (n_pages + 1) * PAGE_SIZE
    per_req = lambda s: pl.BlockSpec((1,) + s, lambda r, pt: (r, 0, 0))
    return pl.pallas_call(
        functools.partial(_dsa_sample_select_kernel, layer=layer, n_t=n_t, n_pages=n_pages, nsel=nsel),
        grid_spec=pltpu.PrefetchScalarGridSpec(
            num_scalar_prefetch=1,
            grid=(n_b,),
            in_specs=[pl.BlockSpec(memory_space=pl.ANY),
                      per_req((H_IDX * n_t, D_IDX)), per_req((H_IDX * n_t, PAGE_SIZE)), per_req((D_IDX, PAGE_SIZE))],
            out_specs=pl.BlockSpec((n_b * n_t, n_keys_pad), lambda r, pt: (0, 0)),
            scratch_shapes=[pltpu.VMEM((n_b * n_t, n_keys_pad), F32),
                            pltpu.VMEM((2, n_pages, D_IDX, PAGE_SIZE), F32),
                            pltpu.SemaphoreType.DMA((2,))]),
        out_shape=jax.ShapeDtypeStruct((n_b * n_t, n_keys_pad), F32),
        compiler_params=_cparams(("arbitrary",)),
        name="dsa_sample_select",
    )(page_table, idx_pool, lq, sgb, iknew)


def _dsa_sample_attend_kernel(pt_ref, *refs, n_t, steps):
    kp = refs[:PAGES_PER_STEP]
    vp = refs[PAGES_PER_STEP:2 * PAGES_PER_STEP]
    qbd_ref, bias_ref, knew_ref, vnew_ref, o_ref, m_ref, l_ref, acc_ref = refs[2 * PAGES_PER_STEP:]
    j = pl.program_id(1)
    rows = H_A * n_t

    @pl.when(j == 0)
    def _():
        m_ref[...] = jnp.full(m_ref.shape, NEG, F32)
        l_ref[...] = jnp.zeros(l_ref.shape, F32)
        acc_ref[...] = jnp.zeros(acc_ref.shape, F32)

    qbd = qbd_ref[0]

    def update(kblk, vblk, bias):
        n = kblk.shape[0]
        s = _dot_nt(qbd, kblk) + jnp.concatenate([bias] * H_A, axis=0)
        m_prev = m_ref[...]
        m_new = jnp.maximum(m_prev, jnp.max(s, axis=1, keepdims=True))
        alpha = jnp.exp2(m_prev - m_new)
        p = jnp.exp2(s - jnp.concatenate([m_new] * (n // LANE), axis=1))
        l_ref[...] = alpha * l_ref[...] + jnp.sum(p, axis=1, keepdims=True)
        pv = _dot(p.astype(BF16), vblk)
        diag = jnp.concatenate(
            [pv[h * n_t:(h + 1) * n_t, h * DH_A:(h + 1) * DH_A] for h in range(H_A)], axis=0)
        acc_ref[...] = alpha * acc_ref[...] + diag
        m_ref[...] = m_new

    def page2d(ref):
        return jnp.concatenate(
            [ref[0, 0, pl.ds(h, PAGE_SIZE, stride=H_A), :] for h in range(H_A)], axis=1).astype(BF16)

    kblk = jnp.concatenate([page2d(kp[u]) for u in range(PAGES_PER_STEP)], axis=0)
    vblk = jnp.concatenate([page2d(vp[u]) for u in range(PAGES_PER_STEP)], axis=0)
    width = PAGES_PER_STEP * PAGE_SIZE
    off = pl.multiple_of(j * width, width)
    update(kblk, vblk, bias_ref[0, :, pl.ds(off, width)])

    @pl.when(j == steps - 1)
    def _():
        past = steps * width
        update(page2d(knew_ref), page2d(vnew_ref), bias_ref[0, :, past:past + PAGE_SIZE])
        o = acc_ref[...] / l_ref[...]
        o_ref[0] = jnp.concatenate([o[h * n_t:(h + 1) * n_t] for h in range(H_A)], axis=1)


def _dsa_sample_attend(page_table, k_pool, v_pool, layer, qbd, bias, knew, vnew, n_b, n_t):
    n_pages = page_table.shape[1]
    steps = n_pages // PAGES_PER_STEP
    n_keys_pad = (n_pages + 1) * PAGE_SIZE
    page_spec = lambda u: pl.BlockSpec(
        (1, 1, PAGE_SIZE * H_A, DH_A), lambda r, j, pt, u=u: (layer, pt[r, j * PAGES_PER_STEP + u], 0, 0))
    per_req = lambda s: pl.BlockSpec((1,) + s, lambda r, j, pt: (r, 0, 0))
    rows = H_A * n_t
    return pl.pallas_call(
        functools.partial(_dsa_sample_attend_kernel, n_t=n_t, steps=steps),
        grid_spec=pltpu.PrefetchScalarGridSpec(
            num_scalar_prefetch=1,
            grid=(n_b, steps),
            in_specs=[page_spec(u) for u in range(PAGES_PER_STEP)] * 2
            + [per_req((rows, W_A)), per_req((n_t, n_keys_pad))]
            + [pl.BlockSpec((1, 1, PAGE_SIZE * H_A, DH_A), lambda r, j, pt: (r, 0, 0, 0))] * 2,
            out_specs=per_req((n_t, W_A)),
            scratch_shapes=[pltpu.VMEM((rows, LANE), F32), pltpu.VMEM((rows, LANE), F32),
                            pltpu.VMEM((rows, DH_A), F32)]),
        out_shape=jax.ShapeDtypeStruct((n_b, n_t, W_A), F32),
        compiler_params=_cparams(("parallel", "arbitrary")),
        name="dsa_sample_attend",
    )(page_table, *([k_pool] * PAGES_PER_STEP), *([v_pool] * PAGES_PER_STEP), qbd, bias, knew, vnew)


def kernel(x_prompt, x_sample, mem_prompt, cache_k, cache_v, cache_idx_k, state_hgrn, cache_mem_k,
           cache_mem_v, page_table, w_in, lb_logits, hgrn_norm_g, w_mem_k, w_mem_v, w_out, ln_g, ln_b):
    n_bp, t_p, _ = x_prompt.shape
    n_bs, t_s, _ = x_sample.shape
    n_pages = page_table.shape[1]
    past = n_pages * PAGE_SIZE
    l = 0

    lb_all = jnp.cumsum(jax.nn.softmax(lb_logits.astype(F32), axis=0), axis=0)
    lb = lb_all[l][None, :]
    g_norm = hgrn_norm_g[l][None, :]
    w_t = _cast_rows(jnp.swapaxes(w_in, 1, 2), l, tr=1024)
    w_o = w_out[l].astype(BF16)
    lg, lbias = ln_g[l][None, :], ln_b[l][None, :]

    xp = x_prompt.reshape(n_bp * t_p, D_MODEL)
    z, zs = _proj(xp, w_t, tm=1024, tn=1024)
    pos_p = jnp.arange(t_p, dtype=jnp.int32)
    tabs = _rope_tables(pos_p, DH_A) + _rope_tables(pos_p, D_IDX)
    q, kf, kb, vf, vt, iqw, sgn_t, ikf, ike, iko = _prep(z, zs, tabs, tm=256)
    a_out = _dsa_prompt(q, kb, vt, iqw, sgn_t, ike, iko, n_bp, t_p, tq=256)
    s0 = jnp.zeros((1, n_bp, H_B, DK_B, DV_B), F32)
    bn, s_p = _hgrn(z, lb, g_norm, s0, 0, n_bp, t_p, chunk=128, heads=H_B, mm_dtype=BF16)
    mk_p, mv_p = _mem_kv(mem_prompt, w_mem_k[l].astype(BF16), w_mem_v[l].astype(BF16))
    m_out = _mem_attend(z, mk_p, mv_p, 0, n_bp, t_p, tq=512, mm_dtype=BF16)
    y_p = _merge(a_out, z, bn, m_out, xp, w_o, lg, lbias, tm=256)

    n_s = n_bs * t_s
    xs = x_sample.reshape(n_s, D_MODEL)
    z2, zs2 = _proj(xs, w_t, tm=n_s, tn=1024)
    pos_s = past + jnp.arange(t_s, dtype=jnp.int32)
    tabs_s = [jnp.tile(t, (n_bs, 1)) for t in _rope_tables(pos_s, DH_A) + _rope_tables(pos_s, D_IDX)]
    q2, kf2, kb2, vf2, vt2, iqw2, sgn_t2, ikf2, ike2, iko2 = _prep(z2, zs2, tabs_s, tm=n_s)
    nsel_s = min(TOPK_MAX, (past + t_s) // 4)
    lq = iqw2.reshape(n_bs, t_s, H_IDX, D_IDX).transpose(0, 2, 1, 3).reshape(n_bs, H_IDX * t_s, D_IDX)
    sgb = jnp.broadcast_to(
        sgn_t2.reshape(H_IDX, n_bs, t_s).transpose(1, 0, 2).reshape(n_bs, H_IDX * t_s, 1),
        (n_bs, H_IDX * t_s, PAGE_SIZE))
    pad_keys = lambda a: jnp.pad(a.reshape((n_bs, t_s) + a.shape[1:]),
                                 ((0, 0), (0, PAGE_SIZE - t_s)) + ((0, 0),) * (a.ndim - 1))
    idx_pool_t = jnp.swapaxes(cache_idx_k, 2, 3)
    bias = _dsa_sample_select(page_table, idx_pool_t, l, lq, sgb, jnp.swapaxes(pad_keys(ikf2), 1, 2),
                              n_bs, t_s, nsel_s).reshape(n_bs, t_s, -1)
    q4 = q2.reshape(n_bs, t_s, H_A, DH_A)
    eye = jnp.eye(H_A, dtype=q2.dtype)
    qbd = (q4[:, None, :, :, :] * eye[None, :, None, :, None]).reshape(n_bs, H_A * t_s, W_A)
    n_layers, n_phys = cache_k.shape[:2]
    as_rows = lambda a: a.reshape(a.shape[0], a.shape[1], PAGE_SIZE * H_A, DH_A)
    a_out2 = _dsa_sample_attend(page_table, as_rows(cache_k), as_rows(cache_v), l, qbd, bias,
                                as_rows(pad_keys(kf2)[:, None]), as_rows(pad_keys(vf2)[:, None]), n_bs, t_s)
    bn2, s_s = _hgrn(z2, lb, g_norm, state_hgrn, l, n_bs, t_s, chunk=t_s, heads=H_B, mm_dtype=F32)
    m_out2 = _mem_attend(z2, cache_mem_k, cache_mem_v, l, n_bs, t_s, tq=t_s, mm_dtype=F32)
    y_s = _merge(a_out2.reshape(n_s, W_A), z2, bn2, m_out2, xs, w_o, lg, lbias, tm=n_s)

    return (y_p.reshape(n_bp, t_p, D_MODEL), y_s.reshape(n_bs, t_s, D_MODEL),
            kf.reshape(1, n_bp, t_p, H_A, DH_A), vf.reshape(1, n_bp, t_p, H_A, DH_A),
            ikf.reshape(1, n_bp, t_p, D_IDX), s_p, mk_p, mv_p,
            kf2.reshape(1, n_bs, t_s, H_A, DH_A), vf2.reshape(1, n_bs, t_s, H_A, DH_A),
            ikf2.reshape(1, n_bs, t_s, D_IDX), s_s)
```

```python
import functools

import numpy as np
import jax
import jax.numpy as jnp
from jax import lax
from jax.experimental import pallas as pl
from jax.experimental.pallas import tpu as pltpu

D_MODEL = 2048
DH_A = 128
H_A = 8
W_A = H_A * DH_A
H_IDX = 16
D_IDX = 64
W_IDX = H_IDX * D_IDX
TOPK_MAX = 256
DK_B = 128
DV_B = 128
H_B = 4
W_B = H_B * DV_B
DH_M = 128
H_M = 4
W_M = H_M * DH_M
N_MEM = 256
PAGE_SIZE = 128
ROPE_THETA = 500000.0
ROPE_FRAC = 4
LN_EPS = 1e-5
RMS_EPS = 1e-6
DEPTH = 1
ALPHA = (2.0 * DEPTH) ** 0.25

LANE = 128
VMEM_LIMIT = 48 * 1024 * 1024

C_AQ, C_AK, C_AV, C_AG, C_IQ = 0, 1024, 2048, 3072, 4096
C_BQ, C_BF, C_BI, C_BG, C_MQ, C_MG = 5120, 5632, 6144, 6656, 7168, 7680
N_MAIN = 8192

NEG = -1e30
Q_SCALE = DH_A ** -0.5 * 1.4426950408889634
ATT_GROUP = H_A
VT_ROWS = DH_A + 16
INT_MIN = -(2 ** 31)
BF16 = jnp.bfloat16
F32 = jnp.float32


def _dot(a, b, **kw):
    return jnp.dot(a, b, preferred_element_type=F32, **kw)


def _dot_nt(a, b):
    return lax.dot_general(a, b, (((1,), (1,)), ((), ())), preferred_element_type=F32)


def _dot_tn(a, b):
    return lax.dot_general(a, b, (((0,), (0,)), ((), ())), preferred_element_type=F32)


def _sigmoid(x):
    return 1.0 / (1.0 + jnp.exp(-x))


def _silu(x):
    return x * _sigmoid(x)


def _cparams(sem):
    return pltpu.CompilerParams(dimension_semantics=sem, vmem_limit_bytes=VMEM_LIMIT)


def _proj_kernel(x_ref, w_ref, ws_ref, o_ref, os_ref, xb_ref):
    j = pl.program_id(1)

    @pl.when(j == 0)
    def _():
        xb = x_ref[...].astype(BF16)
        xb_ref[...] = xb
        os_ref[...] = _dot_nt(xb, ws_ref[...])

    o_ref[...] = _dot_nt(xb_ref[...], w_ref[...])


W_SIDE_ROW = C_IQ + W_IDX
W_SIDE_ROWS = D_IDX + H_IDX
N_IN = N_MAIN + W_SIDE_ROWS


def _proj(x, w_t, tm, tn):
    m = x.shape[0]
    n_lo = W_SIDE_ROW // tn

    def w_rows(i, j):
        return ((j * (tn // 16) + jnp.where(j < n_lo, 0, W_SIDE_ROWS // 16)) * 16, 0)

    return pl.pallas_call(
        _proj_kernel,
        grid=(m // tm, N_MAIN // tn),
        in_specs=[pl.BlockSpec((tm, D_MODEL), lambda i, j: (i, 0)),
                  pl.BlockSpec((pl.Element(tn), pl.Element(D_MODEL)), w_rows),
                  pl.BlockSpec((pl.Element(LANE), pl.Element(D_MODEL)), lambda i, j: (W_SIDE_ROW, 0))],
        out_specs=[pl.BlockSpec((tm, tn), lambda i, j: (i, j)),
                   pl.BlockSpec((tm, LANE), lambda i, j: (i, 0))],
        out_shape=[jax.ShapeDtypeStruct((m, N_MAIN), F32),
                   jax.ShapeDtypeStruct((m, LANE), F32)],
        scratch_shapes=[pltpu.VMEM((tm, D_MODEL), BF16)],
        compiler_params=_cparams(("parallel", "arbitrary")),
        name="in_proj",
    )(x, w_t, w_t)


def _cast_kernel(x_ref, o_ref):
    o_ref[...] = x_ref[0].astype(BF16)


def _cast_rows(w, layer, tr):
    rows = w.shape[1]
    return pl.pallas_call(
        _cast_kernel,
        grid=(pl.cdiv(rows, tr),),
        in_specs=[pl.BlockSpec((1, tr, D_MODEL), lambda i: (layer, i, 0))],
        out_specs=pl.BlockSpec((tr, D_MODEL), lambda i: (i, 0)),
        out_shape=jax.ShapeDtypeStruct((rows, D_MODEL), BF16),
        compiler_params=_cparams(("parallel",)),
        name="w_cast",
    )(w)


def _rope(x, c, sa, sb, half):
    return x * c + pltpu.roll(x, LANE - half, 1) * sa + pltpu.roll(x, half, 1) * sb


def _prep_kernel(aq_ref, ak_ref, av_ref, iq_ref, zs_ref,
                 c1_ref, sa1_ref, sb1_ref, c2_ref, sa2_ref, sb2_ref,
                 q_ref, kf_ref, kb_ref, vf_ref, vt_ref, iqw_ref, sgn_ref,
                 ikf_ref, ike_ref, iko_ref):
    c1, sa1, sb1 = c1_ref[...], sa1_ref[...], sb1_ref[...]
    c2, sa2, sb2 = c2_ref[...], sa2_ref[...], sb2_ref[...]
    half_a = DH_A // ROPE_FRAC // 2
    half_i = D_IDX // ROPE_FRAC // 2
    k_heads, v_heads = [], []
    for h in range(H_A):
        sl = slice(h * DH_A, (h + 1) * DH_A)
        q_ref[:, sl] = (_rope(aq_ref[:, sl], c1, sa1, sb1, half_a) * Q_SCALE).astype(BF16)
        kr = _rope(ak_ref[:, sl], c1, sa1, sb1, half_a)
        kb_ref[:, sl] = kr.astype(BF16)
        v = av_ref[:, sl]
        vt_ref[h * VT_ROWS:h * VT_ROWS + DH_A, :] = v.T.astype(BF16)
        vt_ref[h * VT_ROWS + DH_A:(h + 1) * VT_ROWS, :] = jnp.ones((VT_ROWS - DH_A, v.shape[0]), BF16)
        k_heads.append(kr)
        v_heads.append(v)
    kf_ref[...] = jnp.swapaxes(jnp.stack(k_heads, axis=0), 0, 1)
    vf_ref[...] = jnp.swapaxes(jnp.stack(v_heads, axis=0), 0, 1)

    zs = zs_ref[...]
    lane = lax.broadcasted_iota(jnp.int32, zs.shape, 1)
    ikr = _rope(zs, c2, sa2, sb2, half_i)
    ikf_ref[...] = ikr[:, :D_IDX]
    ike_ref[...] = jnp.where(lane < D_IDX, ikr, 0.0).astype(BF16)
    iko_ref[...] = jnp.where(lane >= D_IDX, pltpu.roll(ikr, D_IDX, 1), 0.0).astype(BF16)
    sgn_t = jnp.sign(pltpu.roll(zs, LANE - D_IDX, 1)).T
    sgn_ref[...] = sgn_t[:H_IDX, :]
    wscale = (D_IDX ** -0.5) * (H_IDX ** -0.5)
    for p in range(H_IDX // 2):
        sl = slice(p * LANE, (p + 1) * LANE)
        wa = jnp.abs(zs[:, D_IDX + 2 * p:D_IDX + 2 * p + 1])
        wb = jnp.abs(zs[:, D_IDX + 2 * p + 1:D_IDX + 2 * p + 2])
        wpair = jnp.where(lane < D_IDX, wa, wb) * wscale
        iqw_ref[:, sl] = (_rope(iq_ref[:, sl], c2, sa2, sb2, half_i) * wpair).astype(BF16)


def _prep(z, zs, tabs, tm):
    m = z.shape[0]
    period = tabs[0].shape[0] // tm
    zblk = lambda c: pl.BlockSpec((tm, 1024), lambda i, c=c: (i, c // 1024))
    tab = pl.BlockSpec((tm, LANE), lambda i: (i % period, 0))
    row = lambda w: pl.BlockSpec((tm, w), lambda i: (i, 0))
    col = lambda w: pl.BlockSpec((w, tm), lambda i: (0, i))
    sds = lambda w, dt: jax.ShapeDtypeStruct((m, w), dt)
    heads =pl.BlockSpec((tm, H_A, DH_A), lambda i: (i, 0, 0))
    heads_sds = jax.ShapeDtypeStruct((m, H_A, DH_A), F32)
    return pl.pallas_call(
        _prep_kernel,
        grid=(m // tm,),
        in_specs=[zblk(C_AQ), zblk(C_AK), zblk(C_AV), zblk(C_IQ), row(LANE)] + [tab] * 6,
        out_specs=[row(W_A), heads, row(W_A), heads, col(H_A * VT_ROWS), row(W_IDX), col(H_IDX),
                   row(D_IDX), row(LANE), row(LANE)],
        out_shape=[sds(W_A, BF16), heads_sds, sds(W_A, BF16), heads_sds,
                   jax.ShapeDtypeStruct((H_A * VT_ROWS, m), BF16), sds(W_IDX, BF16),
                   jax.ShapeDtypeStruct((H_IDX, m), F32),
                   sds(D_IDX, F32), sds(LANE, BF16), sds(LANE, BF16)],
        compiler_params=_cparams(("parallel",)),
        name="rope_prep",
    )(z, z, z, z, zs, *tabs)


def _rope_tables(pos, d):
    r = d // ROPE_FRAC
    half = r // 2
    n = pos.shape[0]
    inv = ROPE_THETA ** (-jnp.arange(half, dtype=F32) / half)
    ang = pos.astype(F32)[:, None] * inv[None, :]
    cos, sin = jnp.cos(ang), jnp.sin(ang)
    zh = jnp.zeros((n, half), F32)
    c = jnp.concatenate([cos, cos, jnp.ones((n, d - r), F32)], axis=1)
    sa = jnp.concatenate([-sin, zh, jnp.zeros((n, d - r), F32)], axis=1)
    sb = jnp.concatenate([zh, sin, jnp.zeros((n, d - r), F32)], axis=1)
    reps = LANE // d
    return [jnp.tile(t, (1, reps)) for t in (c, sa, sb)]


KEY_NEG_INF = -2139095041
F32_LOWEST = -3.4028234663852886e38


def _key_to_float(key):
    key = jnp.maximum(key, jnp.int32(KEY_NEG_INF))
    b = key ^ (lax.shift_right_arithmetic(key, 31) & jnp.int32(0x7FFFFFFF))
    return lax.bitcast_convert_type(b, F32)


def _lane_total(part):
    return _dot(part.astype(BF16), jnp.ones((LANE, LANE), BF16))


def _radix_select(count_ge, shape, total, nsel):
    def body(bi, carry):
        pu, cacc = carry
        bit = jnp.int32(31) - bi
        cand_u = pu | lax.shift_left(jnp.int32(1), bit)
        cnt = count_ge(_key_to_float(cand_u ^ jnp.int32(INT_MIN)))
        ok = cnt >= nsel
        return jnp.where(ok, cand_u, pu), jnp.where(ok, cnt, cacc)

    pu0 = jnp.zeros(shape, jnp.int32)
    c0 = jnp.full(shape, total, F32)
    pu, cacc = lax.fori_loop(0, 32, body, (pu0, c0))
    return _key_to_float(pu ^ jnp.int32(INT_MIN)), cacc


def _dsa_prompt_kernel(q_ref, k_ref, vt_ref, iq_ref, sgn_ref, ike_ref, iko_ref, o_ref,
                       key_ref, bias_ref, m_ref, acc_ref, *, tq, nsel):
    tk = tq
    i = pl.program_id(1)
    nch = i + 1
    sub = 8

    kidx0 = lax.broadcasted_iota(jnp.int32, (tk, tq), 0)
    qidx = i * tq + lax.broadcasted_iota(jnp.int32, (tk, tq), 1)

    def chunk_slice(c):
        return pl.ds(pl.multiple_of(c * tk, tk), tk)

    def score_body(c, carry):
        ds = chunk_slice(c)
        ke = ike_ref[ds, :]
        ko = iko_ref[ds, :]
        acc = jnp.zeros((tk, tq), F32)
        for h in range(H_IDX):
            rhs = iq_ref[:, (h // 2) * LANE:(h // 2 + 1) * LANE]
            d = _dot_nt(ke if h % 2 == 0 else ko, rhs)
            acc = acc + sgn_ref[h:h + 1, :] * jnp.maximum(d, 0.0)
        key_ref[ds, :] = jnp.where(kidx0 + c * tk <= qidx, acc, -jnp.inf)
        return carry

    lax.fori_loop(0, nch, score_body, 0)

    def count(cmp_fn):
        def hits(c):
            hit = jnp.where(cmp_fn(key_ref[chunk_slice(c), :]), 1.0, 0.0)
            return jnp.sum(hit.reshape(tk // sub, sub, tq), axis=0)

        def body(c2, parts):
            return parts[0] + hits(2 * c2), parts[1] + hits(2 * c2 + 1)

        zero = jnp.zeros((sub, tq), F32)
        p0, p1 = lax.fori_loop(0, nch // 2, body, (zero, zero))
        p0 = lax.cond(nch % 2 == 1, lambda p: p + hits(nch - 1), lambda p: p, p0)
        return jnp.sum(p0 + p1, axis=0, keepdims=True)

    total = (nch * tk).astype(F32)
    tau, c_ge = _radix_select(lambda cand: count(lambda kc: kc >= cand), (1, tq), total, float(nsel))
    tau_eff = jnp.maximum(tau, F32_LOWEST)

    def bias_body(c, carry):
        ds = chunk_slice(c)
        bias_ref[ds, :] = jnp.where(key_ref[ds, :] >= tau_eff, 0.0, NEG)
        return carry

    lax.fori_loop(0, nch, bias_body, 0)

    tie_q = jnp.logical_and(c_ge > float(nsel), tau > -jnp.inf)
    has_tie = jnp.max(jnp.where(tie_q, 1.0, 0.0)) > 0.5

    @pl.when(has_tie)
    def _():
        need = float(nsel) - count(lambda kc: kc > tau)
        lower = jnp.where(lax.broadcasted_iota(jnp.int32, (tk, tk), 1)
                          <= lax.broadcasted_iota(jnp.int32, (tk, tk), 0), 1.0, 0.0).astype(BF16)
        real = tau > -jnp.inf

        def tie_body(c, run):
            ds = chunk_slice(c)
            kc = key_ref[ds, :]
            eqf = jnp.where(kc == tau, 1.0, 0.0)
            pre = _dot(lower, eqf.astype(BF16)) + run
            keep = jnp.where(kc > tau, 1.0, jnp.where(real, eqf * jnp.where(pre <= need, 1.0, 0.0), 0.0))
            bias_ref[ds, :] = jnp.where(keep > 0.5, 0.0, NEG)
            return run + jnp.sum(eqf, axis=0, keepdims=True)

        lax.fori_loop(0, nch, tie_body, jnp.zeros((1, tq), F32))

    m_ref[...] = jnp.full(m_ref.shape, NEG, F32)
    acc_ref[...] = jnp.zeros(acc_ref.shape, F32)

    def attn_body(c, carry):
        ds = chunk_slice(c)
        bias = bias_ref[ds, :]
        for h0 in range(0, H_A, ATT_GROUP):
            group = range(h0, h0 + ATT_GROUP)
            ss = {h: _dot_nt(k_ref[ds, h * DH_A:(h + 1) * DH_A], q_ref[:, h * DH_A:(h + 1) * DH_A])
                  for h in group}
            ps, alphas = {}, {}
            for h in group:
                s = ss[h] + bias
                m_prev = m_ref[h]
                m_new = jnp.maximum(m_prev, jnp.max(s, axis=0, keepdims=True))
                alphas[h] = jnp.exp2(m_prev - m_new)
                ps[h] = jnp.exp2(s - m_new).astype(BF16)
                m_ref[h] = m_new
            for h in group:
                acc_ref[h] = alphas[h] * acc_ref[h] + _dot(vt_ref[h * VT_ROWS:(h + 1) * VT_ROWS, ds], ps[h])
        return carry

    lax.fori_loop(0, nch, attn_body, 0)
    for h in range(H_A):
        acc = acc_ref[h]
        o_ref[:, h * DH_A:(h + 1) * DH_A] = (acc[:DH_A] / acc[DH_A:DH_A + 1]).T


def _dsa_prompt(q, k, vt, iqw, sgn_t, ike, iko, n_b, n_t, tq):
    nsel = min(TOPK_MAX, n_t // 4)
    nq = n_t // tq
    qblk = lambda w: pl.BlockSpec((tq, w), lambda b, i: (b * nq + i, 0))
    full = lambda w: pl.BlockSpec((n_t, w), lambda b, i: (b, 0))
    return pl.pallas_call(
        functools.partial(_dsa_prompt_kernel, tq=tq, nsel=nsel),
        grid=(n_b, nq),
        in_specs=[qblk(W_A), full(W_A), pl.BlockSpec((H_A * VT_ROWS, n_t), lambda b, i: (0, b)), qblk(W_IDX),
                  pl.BlockSpec((H_IDX, tq), lambda b, i: (0, b * nq + i)), full(LANE), full(LANE)],
        out_specs=qblk(W_A),
        out_shape=jax.ShapeDtypeStruct((n_b * n_t, W_A), F32),
        scratch_shapes=[pltpu.VMEM((n_t, tq), F32),
                        pltpu.VMEM((n_t, tq), F32),
                        pltpu.VMEM((H_A, 1, tq), F32),
                        pltpu.VMEM((H_A, VT_ROWS, tq), F32)],
        compiler_params=_cparams(("parallel", "arbitrary")),
        name="dsa_prompt",
    )(q, k, vt, iqw, sgn_t, ike, iko)


HGRN_BLOCK = 8


def _hgrn_levels(chunk):
    levels, m = [], HGRN_BLOCK
    while m < chunk:
        levels.append(m)
        m *= 2
    return levels


def _hgrn_masks(chunk):
    row, col = np.indices((chunk, chunk))
    masks = []
    for m in _hgrn_levels(chunk):
        same_pair = (row // (2 * m)) == (col // (2 * m))
        masks.append(same_pair & (row % (2 * m) >= m) & (col % (2 * m) < m))
    for j in range(HGRN_BLOCK):
        masks.append((row - col == j) & (row % HGRN_BLOCK >= j))
    return np.stack(masks).astype(np.float32)


def _hgrn_kernel(bq_ref, bf_ref, bi_ref, lb_ref, g_ref, s0_ref, mask_ref, o_ref, s_out_ref, st_ref,
                 *, n_t, chunk, heads, mm_dtype):
    c_ = chunk
    g = g_ref[...]
    for hh in range(heads):
        st_ref[hh] = s0_ref[0, 0, hh].T
    row = lax.broadcasted_iota(jnp.int32, (c_, c_), 0)
    col = lax.broadcasted_iota(jnp.int32, (c_, c_), 1)
    lower = jnp.where(col <= row, 1.0, 0.0)
    levels = _hgrn_levels(c_)

    def head_chunk(hh, ds):
        hs = slice(hh * DK_B, (hh + 1) * DK_B)
        lb = lb_ref[:, hs]
        bq = bq_ref[ds, hs]
        qs = bq * _sigmoid(bq)
        f = lb + (1.0 - lb) * _sigmoid(bf_ref[ds, hs])
        kk = 1.0 - f
        v = bi_ref[ds, hs]
        gcum = _dot(lower, jnp.log2(f), precision=lax.Precision.HIGHEST)
        a = jnp.zeros((c_, c_), F32)
        for li, m in enumerate(levels):
            g3 = gcum.reshape(c_ // (2 * m), 2 * m, DK_B)
            ref = jnp.broadcast_to(g3[:, m - 1:m, :], g3.shape).reshape(c_, DK_B)
            qh = qs * jnp.exp2(jnp.minimum(gcum - ref, 0.0))
            kh = kk * jnp.exp2(jnp.minimum(ref - gcum, 0.0))
            a = a + _dot_nt(qh.astype(mm_dtype), kh.astype(mm_dtype)) * mask_ref[li]
        for j in range(HGRN_BLOCK):
            kj = kk if j == 0 else pltpu.roll(kk, j, 0)
            gj = gcum if j == 0 else pltpu.roll(gcum, j, 0)
            e = jnp.exp2(jnp.minimum(gcum - gj, 0.0))
            aj = jnp.sum(qs * kj * e, axis=1, keepdims=True)
            a = a + aj * mask_ref[len(levels) + j]
        st = st_ref[hh]
        o = (_dot_nt((qs * jnp.exp2(gcum)).astype(mm_dtype), st.astype(mm_dtype))
             + _dot(a.astype(mm_dtype), v.astype(mm_dtype)))
        glast = gcum[c_ - 1:c_, :]
        kd = kk * jnp.exp2(glast - gcum)
        st_ref[hh] = jnp.exp2(glast) * st + _dot_tn(v.astype(mm_dtype), kd.astype(mm_dtype))
        o_ref[ds, hs] = o * lax.rsqrt(jnp.mean(o * o, axis=1, keepdims=True) + RMS_EPS) * g

    def chunk_body(ci, carry):
        ds = pl.ds(pl.multiple_of(ci * c_, c_), c_)
        for hh in range(heads):
            head_chunk(hh, ds)
        return carry

    lax.fori_loop(0, n_t // c_, chunk_body, 0)
    for hh in range(heads):
        s_out_ref[0, 0, hh] = st_ref[hh].T


def _hgrn(z, lb, g, s0, layer, n_b, n_t, chunk, heads, mm_dtype):
    w = heads * DK_B
    zcol = lambda c: pl.BlockSpec((n_t, w), lambda b, h, c=c: (b, c // w + h))
    sblk = lambda lyr: pl.BlockSpec((1, 1, heads, DK_B, DV_B), lambda b, h: (lyr, b, h, 0, 0))
    masks = jnp.asarray(_hgrn_masks(chunk))
    return pl.pallas_call(
        functools.partial(_hgrn_kernel, n_t=n_t, chunk=chunk, heads=heads, mm_dtype=mm_dtype),
        grid=(n_b, H_B // heads),
        in_specs=[zcol(C_BQ), zcol(C_BF), zcol(C_BI),
                  pl.BlockSpec((1, w), lambda b, h: (0, h)),
                  pl.BlockSpec((1, DV_B), lambda b, h: (0, 0)),
                  sblk(layer),
                  pl.BlockSpec(masks.shape, lambda b, h: (0, 0, 0))],
        out_specs=[pl.BlockSpec((n_t, w), lambda b, h: (b, h)), sblk(0)],
        out_shape=[jax.ShapeDtypeStruct((n_b * n_t, W_B), F32),
                   jax.ShapeDtypeStruct((1, n_b, H_B, DK_B, DV_B), F32)],
        scratch_shapes=[pltpu.VMEM((heads, DV_B, DK_B), F32)],
        compiler_params=_cparams(("parallel", "arbitrary")),
        name="hgrn2",
    )(z, z, z, lb, g, s0, masks)


def _mem_kv_kernel(x_ref, wk_ref, wv_ref, mk_ref, mv_ref):
    x = x_ref[0].astype(BF16)
    mk = _dot(x, wk_ref[...])
    mv = _dot(x, wv_ref[...])
    heads = lambda a: jnp.stack([a[:, h * DH_M:(h + 1) * DH_M] for h in range(H_M)], axis=0)
    mk_ref[0, 0] = jnp.swapaxes(heads(mk), 0, 1)
    mv_ref[0, 0] = jnp.swapaxes(heads(mv), 0, 1)


def _mem_kv(mem, wk, wv):
    n_b = mem.shape[0]
    wspec = pl.BlockSpec((D_MODEL, W_M), lambda b: (0, 0))
    ospec = pl.BlockSpec((1, 1, N_MEM, H_M, DH_M), lambda b: (0, b, 0, 0, 0))
    sds = jax.ShapeDtypeStruct((1, n_b, N_MEM, H_M, DH_M), F32)
    return pl.pallas_call(
        _mem_kv_kernel,
        grid=(n_b,),
        in_specs=[pl.BlockSpec((1, N_MEM, D_MODEL), lambda b: (b, 0, 0)), wspec, wspec],
        out_specs=[ospec, ospec],
        out_shape=[sds, sds],
        compiler_params=_cparams(("parallel",)),
        name="mem_kv",
    )(mem, wk, wv)


def _mem_kernel(q_ref, mk_ref, mv_ref, o_ref, *, mm_dtype):
    mk = jnp.swapaxes(mk_ref[0, 0], 0, 1).astype(mm_dtype)
    mv = jnp.swapaxes(mv_ref[0, 0], 0, 1).astype(mm_dtype)
    for h in range(H_M):
        hs = slice(h * DH_M, (h + 1) * DH_M)
        q = (q_ref[:, hs] * (DH_M ** -0.5)).astype(mm_dtype)
        s = _dot_nt(q, mk[h])
        p = jnp.exp(s - jnp.max(s, axis=1, keepdims=True))
        l = jnp.sum(p, axis=1, keepdims=True)
        o_ref[:, hs] = _dot(p.astype(mm_dtype), mv[h]) / l


def _mem_attend(z, mk, mv, layer, n_b, n_t, tq, mm_dtype):
    nq = n_t // tq
    kv = pl.BlockSpec((1, 1, N_MEM, H_M, DH_M), lambda b, i: (layer, b, 0, 0, 0))
    return pl.pallas_call(
        functools.partial(_mem_kernel, mm_dtype=mm_dtype),
        grid=(n_b, nq),
        in_specs=[pl.BlockSpec((tq, W_M), lambda b, i: (b * nq + i, C_MQ // W_M)), kv, kv],
        out_specs=pl.BlockSpec((tq, W_M), lambda b, i: (b * nq + i, 0)),
        out_shape=jax.ShapeDtypeStruct((n_b * n_t, W_M), F32),
        compiler_params=_cparams(("parallel", "parallel")),
        name="mem_attend",
    )(z, mk, mv)


def _merge_kernel(a_ref, ag_ref, bn_ref, bg_ref, mo_ref, mg_ref, h_ref, wo_ref, lg_ref, lbias_ref, o_ref):
    a = (a_ref[...] * _silu(ag_ref[...])).astype(BF16)
    b = (bn_ref[...] * _silu(bg_ref[...])).astype(BF16)
    m = (mo_ref[...] * _silu(mg_ref[...])).astype(BF16)
    y = (_dot(a, wo_ref[0:W_A, :]) + _dot(b, wo_ref[W_A:W_A + W_B, :])
         + _dot(m, wo_ref[W_A + W_B:D_MODEL, :]))
    r = ALPHA * h_ref[...] + y
    xc = r - jnp.mean(r, axis=1, keepdims=True)
    var = jnp.mean(xc * xc, axis=1, keepdims=True)
    o_ref[...] = xc * lax.rsqrt(var + LN_EPS) * lg_ref[...] + lbias_ref[...]


def _merge(a_out, z, bn, m_out, h, w_out, ln_g, ln_b, tm):
    m = h.shape[0]
    row = lambda w: pl.BlockSpec((tm, w), lambda i: (i, 0))
    zcol = lambda c, w: pl.BlockSpec((tm, w), lambda i, c=c, w=w: (i, c // w))
    const = lambda s: pl.BlockSpec(s, lambda i: (0, 0), pipeline_mode=pl.Buffered(1))
    return pl.pallas_call(
        _merge_kernel,
        grid=(m // tm,),
        in_specs=[row(W_A), zcol(C_AG, W_A), row(W_B), zcol(C_BG, W_B), row(W_M), zcol(C_MG, W_M),
                  row(D_MODEL), const((D_MODEL, D_MODEL)), const((1, D_MODEL)), const((1, D_MODEL))],
        out_specs=row(D_MODEL),
        out_shape=jax.ShapeDtypeStruct((m, D_MODEL), F32),
        compiler_params=_cparams(("parallel",)),
        name="merge",
    )(a_out, z, bn, z, m_out, z, h, w_out, ln_g, ln_b)


PAGES_PER_STEP = 8


def _dsa_sample_select_kernel(pt_ref, pool_ref, lq_ref, sg_ref, iknew_ref, bias_ref, key_ref, page_buf, sem,
                              *, layer, n_t, n_pages, nsel):
    r = pl.program_id(0)
    slot = r % 2
    n_keys_pad = (n_pages + 1) * PAGE_SIZE

    def page_copy(req, p, s):
        return pltpu.make_async_copy(pool_ref.at[layer, pt_ref[req, p]], page_buf.at[s, p], sem.at[s])

    def for_pages(fn):
        def body(p, carry):
            fn(p)
            return carry
        lax.fori_loop(0, n_pages, body, 0)

    @pl.when(r == 0)
    def _():
        for_pages(lambda p: page_copy(0, p, 0).start())

    @pl.when(r + 1 < pl.num_programs(0))
    def _():
        for_pages(lambda p: page_copy(r + 1, p, 1 - slot).start())

    for_pages(lambda p: page_copy(r, p, slot).wait())

    lq = lq_ref[0]
    sg = sg_ref[0]

    def score(ik_t):
        d = _dot(lq, ik_t.astype(BF16))
        r = jnp.maximum(d, 0.0) * jnp.concatenate([sg] * (ik_t.shape[1] // PAGE_SIZE), axis=1)
        s = r[0:n_t]
        for h in range(1, H_IDX):
            s = s + r[h * n_t:(h + 1) * n_t]
        return s

    width = PAGES_PER_STEP * PAGE_SIZE

    rows = pl.ds(pl.multiple_of(r * n_t, n_t), n_t)

    def group_body(gi, carry):
        base = gi * PAGES_PER_STEP
        ik_step = jnp.concatenate([page_buf[slot, base + u] for u in range(PAGES_PER_STEP)], axis=1)
        key_ref[rows, pl.ds(pl.multiple_of(gi * width, width), width)] = score(ik_step)
        return carry

    lax.fori_loop(0, n_pages // PAGES_PER_STEP, group_body, 0)

    past = n_pages * PAGE_SIZE
    s_new = score(iknew_ref[0])
    t_i = lax.broadcasted_iota(jnp.int32, (n_t, PAGE_SIZE), 0)
    k_i = lax.broadcasted_iota(jnp.int32, (n_t, PAGE_SIZE), 1)
    key_ref[rows, past:n_keys_pad] = jnp.where(k_i <= t_i, s_new, -jnp.inf)

    @pl.when(r == pl.num_programs(0) - 1)
    def _():
        n_rows = key_ref.shape[0]
        n_tiles = n_keys_pad // LANE

        def count_ge(cand):
            part = jnp.zeros((n_rows, LANE), F32)
            for c in range(n_tiles):
                part = part + jnp.where(key_ref[:, c * LANE:(c + 1) * LANE] >= cand, 1.0, 0.0)
            return _lane_total(part)

        tau_r, c_ge_r = _radix_select(count_ge, (n_rows, LANE), float(n_keys_pad), float(nsel))
        tau = tau_r[:, 0:1]
        tau_eff = jnp.maximum(tau, F32_LOWEST)
        bias_ref[...] = jnp.where(key_ref[...] >= tau_eff, 0.0, NEG)

        tie_rows = jnp.logical_and(c_ge_r > float(nsel), tau_r > -jnp.inf)
        has_tie = jnp.max(jnp.where(tie_rows, 1.0, 0.0)) > 0.5

        @pl.when(has_tie)
        def _():
            c_gt = jnp.sum(jnp.where(key_ref[...] > tau, 1.0, 0.0), axis=1, keepdims=True)
            need = float(nsel) - c_gt
            ri = lax.broadcasted_iota(jnp.int32, (PAGE_SIZE, PAGE_SIZE), 0)
            ci = lax.broadcasted_iota(jnp.int32, (PAGE_SIZE, PAGE_SIZE), 1)
            upper = jnp.where(ri <= ci, 1.0, 0.0)
            real = tau > -jnp.inf

            def tie_body(c, run):
                ds = pl.ds(pl.multiple_of(c * PAGE_SIZE, PAGE_SIZE), PAGE_SIZE)
                kc = key_ref[:, ds]
                eqf = jnp.where(kc == tau, 1.0, 0.0)
                pre = _dot(eqf, upper) + run
                keep = jnp.where(kc > tau, 1.0, jnp.where(real, eqf * jnp.where(pre <= need, 1.0, 0.0), 0.0))
                bias_ref[:, ds] = jnp.where(keep > 0.5, 0.0, NEG)
                return run + jnp.sum(eqf, axis=1, keepdims=True)

            lax.fori_loop(0, n_keys_pad // PAGE_SIZE, tie_body, jnp.zeros((n_rows, 1), F32))


def _dsa_sample_select(page_table, idx_pool, layer, lq, sgb, iknew, n_b, n_t, nsel):
    n_pages = page_table.shape[1]
    n_keys_pad = (n_pages + 1) * PAGE_SIZE
    per_req = lambda s: pl.BlockSpec((1,) + s, lambda r, pt: (r, 0, 0))
    return pl.pallas_call(
        functools.partial(_dsa_sample_select_kernel, layer=layer, n_t=n_t, n_pages=n_pages, nsel=nsel),
        grid_spec=pltpu.PrefetchScalarGridSpec(
            num_scalar_prefetch=1,
            grid=(n_b,),
            in_specs=[pl.BlockSpec(memory_space=pl.ANY),
                      per_req((H_IDX * n_t, D_IDX)), per_req((H_IDX * n_t, PAGE_SIZE)), per_req((D_IDX, PAGE_SIZE))],
            out_specs=pl.BlockSpec((n_b * n_t, n_keys_pad), lambda r, pt: (0, 0)),
            scratch_shapes=[pltpu.VMEM((n_b * n_t, n_keys_pad), F32),
                            pltpu.VMEM((2, n_pages, D_IDX, PAGE_SIZE), F32),
                            pltpu.SemaphoreType.DMA((2,))]),
        out_shape=jax.ShapeDtypeStruct((n_b * n_t, n_keys_pad), F32),
        compiler_params=_cparams(("arbitrary",)),
        name="dsa_sample_select",
    )(page_table, idx_pool, lq, sgb, iknew)


def _dsa_sample_attend_kernel(pt_ref, *refs, n_t, steps):
    kp = refs[:PAGES_PER_STEP]
    vp = refs[PAGES_PER_STEP:2 * PAGES_PER_STEP]
    qbd_ref, bias_ref, knew_ref, vnew_ref, o_ref, m_ref, l_ref, acc_ref = refs[2 * PAGES_PER_STEP:]
    j = pl.program_id(1)
    rows = H_A * n_t

    @pl.when(j == 0)
    def _():
        m_ref[...] = jnp.full(m_ref.shape, NEG, F32)
        l_ref[...] = jnp.zeros(l_ref.shape, F32)
        acc_ref[...] = jnp.zeros(acc_ref.shape, F32)

    qbd = qbd_ref[0]

    def update(kblk, vblk, bias):
        n = kblk.shape[0]
        s = _dot_nt(qbd, kblk) + jnp.concatenate([bias] * H_A, axis=0)
        m_prev = m_ref[...]
        m_new = jnp.maximum(m_prev, jnp.max(s, axis=1, keepdims=True))
        alpha = jnp.exp2(m_prev - m_new)
        p = jnp.exp2(s - jnp.concatenate([m_new] * (n // LANE), axis=1))
        l_ref[...] = alpha * l_ref[...] + jnp.sum(p, axis=1, keepdims=True)
        pv = _dot(p.astype(BF16), vblk)
        diag = jnp.concatenate(
            [pv[h * n_t:(h + 1) * n_t, h * DH_A:(h + 1) * DH_A] for h in range(H_A)], axis=0)
        acc_ref[...] = alpha * acc_ref[...] + diag
        m_ref[...] = m_new

    def page2d(ref):
        return jnp.concatenate(
            [ref[0, 0, pl.ds(h, PAGE_SIZE, stride=H_A), :] for h in range(H_A)], axis=1).astype(BF16)

    kblk = jnp.concatenate([page2d(kp[u]) for u in range(PAGES_PER_STEP)], axis=0)
    vblk = jnp.concatenate([page2d(vp[u]) for u in range(PAGES_PER_STEP)], axis=0)
    width = PAGES_PER_STEP * PAGE_SIZE
    off = pl.multiple_of(j * width, width)
    update(kblk, vblk, bias_ref[0, :, pl.ds(off, width)])

    @pl.when(j == steps - 1)
    def _():
        past = steps * width
        update(page2d(knew_ref), page2d(vnew_ref), bias_ref[0, :, past:past + PAGE_SIZE])
        o = acc_ref[...] / l_ref[...]
        o_ref[0] = jnp.concatenate([o[h * n_t:(h + 1) * n_t] for h in range(H_A)], axis=1)


def _dsa_sample_attend(page_table, k_pool, v_pool, layer, qbd, bias, knew, vnew, n_b, n_t):
    n_pages = page_table.shape[1]
    steps = n_pages // PAGES_PER_STEP
    n_keys_pad = (n_pages + 1) * PAGE_SIZE
    page_spec = lambda u: pl.BlockSpec(
        (1, 1, PAGE_SIZE * H_A, DH_A), lambda r, j, pt, u=u: (layer, pt[r, j * PAGES_PER_STEP + u], 0, 0))
    per_req = lambda s: pl.BlockSpec((1,) + s, lambda r, j, pt: (r, 0, 0))
    rows = H_A * n_t
    return pl.pallas_call(
        functools.partial(_dsa_sample_attend_kernel, n_t=n_t, steps=steps),
        grid_spec=pltpu.PrefetchScalarGridSpec(
            num_scalar_prefetch=1,
            grid=(n_b, steps),
            in_specs=[page_spec(u) for u in range(PAGES_PER_STEP)] * 2
            + [per_req((rows, W_A)), per_req((n_t, n_keys_pad))]
            + [pl.BlockSpec((1, 1, PAGE_SIZE * H_A, DH_A), lambda r, j, pt: (r, 0, 0, 0))] * 2,
            out_specs=per_req((n_t, W_A)),
            scratch_shapes=[pltpu.VMEM((rows, LANE), F32), pltpu.VMEM((rows, LANE), F32),
                            pltpu.VMEM((rows, DH_A), F32)]),
        out_shape=jax.ShapeDtypeStruct((n_b, n_t, W_A), F32),
        compiler_params=_cparams(("parallel", "arbitrary")),
        name="dsa_sample_attend",
    )(page_table, *([k_pool] * PAGES_PER_STEP), *([v_pool] * PAGES_PER_STEP), qbd, bias, knew, vnew)


def kernel(x_prompt, x_sample, mem_prompt, cache_k, cache_v, cache_idx_k, state_hgrn, cache_mem_k,
           cache_mem_v, page_table, w_in, lb_logits, hgrn_norm_g, w_mem_k, w_mem_v, w_out, ln_g, ln_b):
    n_bp, t_p, _ = x_prompt.shape
    n_bs, t_s, _ = x_sample.shape
    n_pages = page_table.shape[1]
    past = n_pages * PAGE_SIZE
    l = 0

    lb_all = jnp.cumsum(jax.nn.softmax(lb_logits.astype(F32), axis=0), axis=0)
    lb = lb_all[l][None, :]
    g_norm = hgrn_norm_g[l][None, :]
    w_t = _cast_rows(jnp.swapaxes(w_in, 1, 2), l, tr=1024)
    w_o = w_out[l].astype(BF16)
    lg, lbias = ln_g[l][None, :], ln_b[l][None, :]

    xp = x_prompt.reshape(n_bp * t_p, D_MODEL)
    z, zs = _proj(xp, w_t, tm=1024, tn=1024)
    pos_p = jnp.arange(t_p, dtype=jnp.int32)
    tabs = _rope_tables(pos_p, DH_A) + _rope_tables(pos_p, D_IDX)
    q, kf, kb, vf, vt, iqw, sgn_t, ikf, ike, iko = _prep(z, zs, tabs, tm=256)
    a_out = _dsa_prompt(q, kb, vt, iqw, sgn_t, ike, iko, n_bp, t_p, tq=256)
    s0 = jnp.zeros((1, n_bp, H_B, DK_B, DV_B), F32)
    bn, s_p = _hgrn(z, lb, g_norm, s0, 0, n_bp, t_p, chunk=128, heads=H_B, mm_dtype=BF16)
    mk_p, mv_p = _mem_kv(mem_prompt, w_mem_k[l].astype(BF16), w_mem_v[l].astype(BF16))
    m_out = _mem_attend(z, mk_p, mv_p, 0, n_bp, t_p, tq=512, mm_dtype=BF16)
    y_p = _merge(a_out, z, bn, m_out, xp, w_o, lg, lbias, tm=256)

    n_s = n_bs * t_s
    xs = x_sample.reshape(n_s, D_MODEL)
    z2, zs2 = _proj(xs, w_t, tm=n_s, tn=1024)
    pos_s = past + jnp.arange(t_s, dtype=jnp.int32)
    tabs_s = [jnp.tile(t, (n_bs, 1)) for t in _rope_tables(pos_s, DH_A) + _rope_tables(pos_s, D_IDX)]
    q2, kf2, kb2, vf2, vt2, iqw2, sgn_t2, ikf2, ike2, iko2 = _prep(z2, zs2, tabs_s, tm=n_s)
    nsel_s = min(TOPK_MAX, (past + t_s) // 4)
    lq = iqw2.reshape(n_bs, t_s, H_IDX, D_IDX).transpose(0, 2, 1, 3).reshape(n_bs, H_IDX * t_s, D_IDX)
    sgb = jnp.broadcast_to(
        sgn_t2.reshape(H_IDX, n_bs, t_s).transpose(1, 0, 2).reshape(n_bs, H_IDX * t_s, 1),
        (n_bs, H_IDX * t_s, PAGE_SIZE))
    pad_keys = lambda a: jnp.pad(a.reshape((n_bs, t_s) + a.shape[1:]),
                                 ((0, 0), (0, PAGE_SIZE - t_s)) + ((0, 0),) * (a.ndim - 1))
    idx_pool_t = jnp.swapaxes(cache_idx_k, 2, 3)
    bias = _dsa_sample_select(page_table, idx_pool_t, l, lq, sgb, jnp.swapaxes(pad_keys(ikf2), 1, 2),
                              n_bs, t_s, nsel_s).reshape(n_bs, t_s, -1)
    q4 = q2.reshape(n_bs, t_s, H_A, DH_A)
    eye = jnp.eye(H_A, dtype=q2.dtype)
    qbd = (q4[:, None, :, :, :] * eye[None, :, None, :, None]).reshape(n_bs, H_A * t_s, W_A)
    n_layers, n_phys = cache_k.shape[:2]
    as_rows = lambda a: a.reshape(a.shape[0], a.shape[1], PAGE_SIZE * H_A, DH_A)
    a_out2 = _dsa_sample_attend(page_table, as_rows(cache_k), as_rows(cache_v), l, qbd, bias,
                                as_rows(pad_keys(kf2)[:, None]), as_rows(pad_keys(vf2)[:, None]), n_bs, t_s)
    bn2, s_s = _hgrn(z2, lb, g_norm, state_hgrn, l, n_bs, t_s, chunk=t_s, heads=H_B, mm_dtype=F32)
    m_out2 = _mem_attend(z2, cache_mem_k, cache_mem_v, l, n_bs, t_s, tq=t_s, mm_dtype=F32)
    y_s = _merge(a_out2.reshape(n_s, W_A), z2, bn2, m_out2, xs, w_o, lg, lbias, tm=n_s)

    return (y_p.reshape(n_bp, t_p, D_MODEL), y_s.reshape(n_bs, t_s, D_MODEL),
            kf.reshape(1, n_bp, t_p, H_A, DH_A), vf.reshape(1, n_bp, t_p, H_A, DH_A),
            ikf.reshape(1, n_bp, t_p, D_IDX), s_p, mk_p, mv_p,
            kf2.reshape(1, n_bs, t_s, H_A, DH_A), vf2.reshape(1, n_bs, t_s, H_A, DH_A),
            ikf2.reshape(1, n_bs, t_s, D_IDX), s_s)
```

```python
import functools

import numpy as np
import jax
import jax.numpy as jnp
from jax import lax
from jax.experimental import pallas as pl
from jax.experimental.pallas import tpu as pltpu

D_MODEL = 2048
DH_A = 128
H_A = 8
W_A = H_A * DH_A
H_IDX = 16
D_IDX = 64
W_IDX = H_IDX * D_IDX
TOPK_MAX = 256
DK_B = 128
DV_B = 128
H_B = 4
W_B = H_B * DV_B
DH_M = 128
H_M = 4
W_M = H_M * DH_M
N_MEM = 256
PAGE_SIZE = 128
ROPE_THETA = 500000.0
ROPE_FRAC = 4
LN_EPS = 1e-5
RMS_EPS = 1e-6
DEPTH = 1
ALPHA = (2.0 * DEPTH) ** 0.25

LANE = 128
VMEM_LIMIT = 48 * 1024 * 1024

C_AQ, C_AK, C_AV, C_AG, C_IQ = 0, 1024, 2048, 3072, 4096
C_BQ, C_BF, C_BI, C_BG, C_MQ, C_MG = 5120, 5632, 6144, 6656, 7168, 7680
N_MAIN = 8192

NEG = -1e30
Q_SCALE = DH_A ** -0.5 * 1.4426950408889634
ATT_GROUP = H_A
VT_ROWS = DH_A + 16
INT_MIN = -(2 ** 31)
BF16 = jnp.bfloat16
F32 = jnp.float32


def _dot(a, b, **kw):
    return jnp.dot(a, b, preferred_element_type=F32, **kw)


def _dot_nt(a, b):
    return lax.dot_general(a, b, (((1,), (1,)), ((), ())), preferred_element_type=F32)


def _dot_tn(a, b):
    return lax.dot_general(a, b, (((0,), (0,)), ((), ())), preferred_element_type=F32)


def _sigmoid(x):
    return 1.0 / (1.0 + jnp.exp(-x))


def _silu(x):
    return x * _sigmoid(x)


def _cparams(sem):
    return pltpu.CompilerParams(dimension_semantics=sem, vmem_limit_bytes=VMEM_LIMIT)


def _proj_kernel(x_ref, w_ref, ws_ref, o_ref, os_ref, xb_ref):
    j = pl.program_id(1)

    @pl.when(j == 0)
    def _():
        xb = x_ref[...].astype(BF16)
        xb_ref[...] = xb
        os_ref[...] = _dot_nt(xb, ws_ref[...])

    o_ref[...] = _dot_nt(xb_ref[...], w_ref[...])


W_SIDE_ROW = C_IQ + W_IDX
W_SIDE_ROWS = D_IDX + H_IDX
N_IN = N_MAIN + W_SIDE_ROWS


def _proj(x, w_t, tm, tn):
    m = x.shape[0]
    n_lo = W_SIDE_ROW // tn

    def w_rows(i, j):
        return ((j * (tn // 16) + jnp.where(j < n_lo, 0, W_SIDE_ROWS // 16)) * 16, 0)

    return pl.pallas_call(
        _proj_kernel,
        grid=(m // tm, N_MAIN // tn),
        in_specs=[pl.BlockSpec((tm, D_MODEL), lambda i, j: (i, 0)),
                  pl.BlockSpec((pl.Element(tn), pl.Element(D_MODEL)), w_rows),
                  pl.BlockSpec((pl.Element(LANE), pl.Element(D_MODEL)), lambda i, j: (W_SIDE_ROW, 0))],
        out_specs=[pl.BlockSpec((tm, tn), lambda i, j: (i, j)),
                   pl.BlockSpec((tm, LANE), lambda i, j: (i, 0))],
        out_shape=[jax.ShapeDtypeStruct((m, N_MAIN), F32),
                   jax.ShapeDtypeStruct((m, LANE), F32)],
        scratch_shapes=[pltpu.VMEM((tm, D_MODEL), BF16)],
        compiler_params=_cparams(("parallel", "arbitrary")),
        name="in_proj",
    )(x, w_t, w_t)


def _cast_kernel(x_ref, o_ref):
    o_ref[...] = x_ref[0].astype(BF16)


def _cast_rows(w, layer, tr):
    rows = w.shape[1]
    return pl.pallas_call(
        _cast_kernel,
        grid=(pl.cdiv(rows, tr),),
        in_specs=[pl.BlockSpec((1, tr, D_MODEL), lambda i: (layer, i, 0))],
        out_specs=pl.BlockSpec((tr, D_MODEL), lambda i: (i, 0)),
        out_shape=jax.ShapeDtypeStruct((rows, D_MODEL), BF16),
        compiler_params=_cparams(("parallel",)),
        name="w_cast",
    )(w)


def _swap_halves_index(shape, d):
    half = d // ROPE_FRAC // 2
    lane = lax.broadcasted_iota(jnp.int32, shape, 1)
    pos = lane & (d - 1)
    return jnp.where(pos < half, lane + half, jnp.where(pos < 2 * half, lane - half, lane))


def _rope(x, c, s, swap):
    return x * c + jnp.take_along_axis(x, swap, axis=1) * s


def _prep_kernel(aq_ref, ak_ref, av_ref, iq_ref, zs_ref, c1_ref, s1_ref, c2_ref, s2_ref,
                 q_ref, kf_ref, kb_ref, vf_ref, vt_ref, iqw_ref, sgn_ref,
                 ikf_ref, ike_ref, iko_ref):
    c1, s1, c2, s2 = c1_ref[...], s1_ref[...], c2_ref[...], s2_ref[...]
    swap_a = _swap_halves_index(c1.shape, DH_A)
    swap_i = _swap_halves_index(c2.shape, D_IDX)
    k_heads, v_heads = [], []
    for h in range(H_A):
        sl = slice(h * DH_A, (h + 1) * DH_A)
        q_ref[:, sl] = (_rope(aq_ref[:, sl], c1, s1, swap_a) * Q_SCALE).astype(BF16)
        kr = _rope(ak_ref[:, sl], c1, s1, swap_a)
        kb_ref[:, sl] = kr.astype(BF16)
        v = av_ref[:, sl]
        vt_ref[h * VT_ROWS:h * VT_ROWS + DH_A, :] = v.T.astype(BF16)
        vt_ref[h * VT_ROWS + DH_A:(h + 1) * VT_ROWS, :] = jnp.ones((VT_ROWS - DH_A, v.shape[0]), BF16)
        k_heads.append(kr)
        v_heads.append(v)
    kf_ref[...] = jnp.swapaxes(jnp.stack(k_heads, axis=0), 0, 1)
    vf_ref[...] = jnp.swapaxes(jnp.stack(v_heads, axis=0), 0, 1)

    zs = zs_ref[...]
    lane = lax.broadcasted_iota(jnp.int32, zs.shape, 1)
    ikr = _rope(zs, c2, s2, swap_i)
    ikf_ref[...] = ikr[:, :D_IDX]
    ike_ref[...] = jnp.where(lane < D_IDX, ikr, 0.0).astype(BF16)
    iko_ref[...] = jnp.where(lane >= D_IDX, pltpu.roll(ikr, D_IDX, 1), 0.0).astype(BF16)
    sgn_t = jnp.sign(pltpu.roll(zs, LANE - D_IDX, 1)).T
    sgn_ref[...] = sgn_t[:H_IDX, :]
    wscale = (D_IDX ** -0.5) * (H_IDX ** -0.5)
    for p in range(H_IDX // 2):
        sl = slice(p * LANE, (p + 1) * LANE)
        wa = jnp.abs(zs[:, D_IDX + 2 * p:D_IDX + 2 * p + 1])
        wb = jnp.abs(zs[:, D_IDX + 2 * p + 1:D_IDX + 2 * p + 2])
        wpair = jnp.where(lane < D_IDX, wa, wb) * wscale
        iqw_ref[:, sl] = (_rope(iq_ref[:, sl], c2, s2, swap_i) * wpair).astype(BF16)


def _prep(z, zs, tabs, tm):
    m = z.shape[0]
    period = tabs[0].shape[0] // tm
    zblk = lambda c: pl.BlockSpec((tm, 1024), lambda i, c=c: (i, c // 1024))
    tab = pl.BlockSpec((tm, LANE), lambda i: (i % period, 0))
    row = lambda w: pl.BlockSpec((tm, w), lambda i: (i, 0))
    col = lambda w: pl.BlockSpec((w, tm), lambda i: (0, i))
    sds = lambda w, dt: jax.ShapeDtypeStruct((m, w), dt)
    heads =pl.BlockSpec((tm, H_A, DH_A), lambda i: (i, 0, 0))
    heads_sds = jax.ShapeDtypeStruct((m, H_A, DH_A), F32)
    return pl.pallas_call(
        _prep_kernel,
        grid=(m // tm,),
        in_specs=[zblk(C_AQ), zblk(C_AK), zblk(C_AV), zblk(C_IQ), row(LANE)] + [tab] * 4,
        out_specs=[row(W_A), heads, row(W_A), heads, col(H_A * VT_ROWS), row(W_IDX), col(H_IDX),
                   row(D_IDX), row(LANE), row(LANE)],
        out_shape=[sds(W_A, BF16), heads_sds, sds(W_A, BF16), heads_sds,
                   jax.ShapeDtypeStruct((H_A * VT_ROWS, m), BF16), sds(W_IDX, BF16),
                   jax.ShapeDtypeStruct((H_IDX, m), F32),
                   sds(D_IDX, F32), sds(LANE, BF16), sds(LANE, BF16)],
        compiler_params=_cparams(("parallel",)),
        name="rope_prep",
    )(z, z, z, z, zs, *tabs)


def _rope_tables(pos, d):
    r = d // ROPE_FRAC
    half = r // 2
    n = pos.shape[0]
    inv = ROPE_THETA ** (-jnp.arange(half, dtype=F32) / half)
    ang = pos.astype(F32)[:, None] * inv[None, :]
    cos, sin = jnp.cos(ang), jnp.sin(ang)
    c = jnp.concatenate([cos, cos, jnp.ones((n, d - r), F32)], axis=1)
    s = jnp.concatenate([-sin, sin, jnp.zeros((n, d - r), F32)], axis=1)
    reps = LANE // d
    return [jnp.tile(t, (1, reps)) for t in (c, s)]


KEY_NEG_INF = -2139095041
F32_LOWEST = -3.4028234663852886e38


def _key_to_float(key):
    key = jnp.maximum(key, jnp.int32(KEY_NEG_INF))
    b = key ^ (lax.shift_right_arithmetic(key, 31) & jnp.int32(0x7FFFFFFF))
    return lax.bitcast_convert_type(b, F32)


def _lane_total(part):
    return _dot(part.astype(BF16), jnp.ones((LANE, LANE), BF16))


def _radix_select(count_ge, shape, total, nsel):
    def body(bi, carry):
        pu, cacc = carry
        bit = jnp.int32(31) - bi
        cand_u = pu | lax.shift_left(jnp.int32(1), bit)
        cnt = count_ge(_key_to_float(cand_u ^ jnp.int32(INT_MIN)))
        ok = cnt >= nsel
        return jnp.where(ok, cand_u, pu), jnp.where(ok, cnt, cacc)

    pu0 = jnp.zeros(shape, jnp.int32)
    c0 = jnp.full(shape, total, F32)
    pu, cacc = lax.fori_loop(0, 32, body, (pu0, c0))
    return _key_to_float(pu ^ jnp.int32(INT_MIN)), cacc


def _dsa_prompt_kernel(q_ref, k_ref, vt_ref, iq_ref, sgn_ref, ike_ref, iko_ref, o_ref,
                       key_ref, bias_ref, m_ref, acc_ref, *, tq, nsel):
    tk = tq
    i = pl.program_id(1)
    nch = i + 1
    sub = 8

    kidx0 = lax.broadcasted_iota(jnp.int32, (tk, tq), 0)
    qidx = i * tq + lax.broadcasted_iota(jnp.int32, (tk, tq), 1)

    def chunk_slice(c):
        return pl.ds(pl.multiple_of(c * tk, tk), tk)

    def score_body(c, carry):
        ds = chunk_slice(c)
        ke = ike_ref[ds, :]
        ko = iko_ref[ds, :]
        acc = jnp.zeros((tk, tq), F32)
        for h in range(H_IDX):
            rhs = iq_ref[:, (h // 2) * LANE:(h // 2 + 1) * LANE]
            d = _dot_nt(ke if h % 2 == 0 else ko, rhs)
            acc = acc + sgn_ref[h:h + 1, :] * jnp.maximum(d, 0.0)
        key_ref[ds, :] = jnp.where(kidx0 + c * tk <= qidx, acc, -jnp.inf)
        return carry

    lax.fori_loop(0, nch, score_body, 0)

    def count(cmp_fn):
        def hits(c):
            hit = jnp.where(cmp_fn(key_ref[chunk_slice(c), :]), 1.0, 0.0)
            return jnp.sum(hit.reshape(tk // sub, sub, tq), axis=0)

        def body(c2, parts):
            return parts[0] + hits(2 * c2), parts[1] + hits(2 * c2 + 1)

        zero = jnp.zeros((sub, tq), F32)
        p0, p1 = lax.fori_loop(0, nch // 2, body, (zero, zero))
        p0 = lax.cond(nch % 2 == 1, lambda p: p + hits(nch - 1), lambda p: p, p0)
        return jnp.sum(p0 + p1, axis=0, keepdims=True)

    total = (nch * tk).astype(F32)
    tau, c_ge = _radix_select(lambda cand: count(lambda kc: kc >= cand), (1, tq), total, float(nsel))
    tau_eff = jnp.maximum(tau, F32_LOWEST)

    def bias_body(c, carry):
        ds = chunk_slice(c)
        bias_ref[ds, :] = jnp.where(key_ref[ds, :] >= tau_eff, 0.0, NEG)
        return carry

    lax.fori_loop(0, nch, bias_body, 0)

    tie_q = jnp.logical_and(c_ge > float(nsel), tau > -jnp.inf)
    has_tie = jnp.max(jnp.where(tie_q, 1.0, 0.0)) > 0.5

    @pl.when(has_tie)
    def _():
        need = float(nsel) - count(lambda kc: kc > tau)
        lower = jnp.where(lax.broadcasted_iota(jnp.int32, (tk, tk), 1)
                          <= lax.broadcasted_iota(jnp.int32, (tk, tk), 0), 1.0, 0.0).astype(BF16)
        real = tau > -jnp.inf

        def tie_body(c, run):
            ds = chunk_slice(c)
            kc = key_ref[ds, :]
            eqf = jnp.where(kc == tau, 1.0, 0.0)
            pre = _dot(lower, eqf.astype(BF16)) + run
            keep = jnp.where(kc > tau, 1.0, jnp.where(real, eqf * jnp.where(pre <= need, 1.0, 0.0), 0.0))
            bias_ref[ds, :] = jnp.where(keep > 0.5, 0.0, NEG)
            return run + jnp.sum(eqf, axis=0, keepdims=True)

        lax.fori_loop(0, nch, tie_body, jnp.zeros((1, tq), F32))

    m_ref[...] = jnp.full(m_ref.shape, NEG, F32)
    acc_ref[...] = jnp.zeros(acc_ref.shape, F32)

    def attn_body(c, carry):
        ds = chunk_slice(c)
        bias = bias_ref[ds, :]
        for h0 in range(0, H_A, ATT_GROUP):
            group = range(h0, h0 + ATT_GROUP)
            ss = {h: _dot_nt(k_ref[ds, h * DH_A:(h + 1) * DH_A], q_ref[:, h * DH_A:(h + 1) * DH_A])
                  for h in group}
            ps, alphas = {}, {}
            for h in group:
                s = ss[h] + bias
                m_prev = m_ref[h]
                m_new = jnp.maximum(m_prev, jnp.max(s, axis=0, keepdims=True))
                alphas[h] = jnp.exp2(m_prev - m_new)
                ps[h] = jnp.exp2(s - m_new).astype(BF16)
                m_ref[h] = m_new
            for h in group:
                acc_ref[h] = alphas[h] * acc_ref[h] + _dot(vt_ref[h * VT_ROWS:(h + 1) * VT_ROWS, ds], ps[h])
        return carry

    lax.fori_loop(0, nch, attn_body, 0)
    for h in range(H_A):
        acc = acc_ref[h]
        o_ref[:, h * DH_A:(h + 1) * DH_A] = (acc[:DH_A] / acc[DH_A:DH_A + 1]).T


def _dsa_prompt(q, k, vt, iqw, sgn_t, ike, iko, n_b, n_t, tq):
    nsel = min(TOPK_MAX, n_t // 4)
    nq = n_t // tq
    qblk = lambda w: pl.BlockSpec((tq, w), lambda b, i: (b * nq + i, 0))
    full = lambda w: pl.BlockSpec((n_t, w), lambda b, i: (b, 0))
    return pl.pallas_call(
        functools.partial(_dsa_prompt_kernel, tq=tq, nsel=nsel),
        grid=(n_b, nq),
        in_specs=[qblk(W_A), full(W_A), pl.BlockSpec((H_A * VT_ROWS, n_t), lambda b, i: (0, b)), qblk(W_IDX),
                  pl.BlockSpec((H_IDX, tq), lambda b, i: (0, b * nq + i)), full(LANE), full(LANE)],
        out_specs=qblk(W_A),
        out_shape=jax.ShapeDtypeStruct((n_b * n_t, W_A), F32),
        scratch_shapes=[pltpu.VMEM((n_t, tq), F32),
                        pltpu.VMEM((n_t, tq), F32),
                        pltpu.VMEM((H_A, 1, tq), F32),
                        pltpu.VMEM((H_A, VT_ROWS, tq), F32)],
        compiler_params=_cparams(("parallel", "arbitrary")),
        name="dsa_prompt",
    )(q, k, vt, iqw, sgn_t, ike, iko)


HGRN_BLOCK = 8


def _hgrn_levels(chunk):
    levels, m = [], HGRN_BLOCK
    while m < chunk:
        levels.append(m)
        m *= 2
    return levels


def _hgrn_masks(chunk):
    row, col = np.indices((chunk, chunk))
    masks = []
    for m in _hgrn_levels(chunk):
        same_pair = (row // (2 * m)) == (col // (2 * m))
        masks.append(same_pair & (row % (2 * m) >= m) & (col % (2 * m) < m))
    for j in range(HGRN_BLOCK):
        masks.append((row - col == j) & (row % HGRN_BLOCK >= j))
    return np.stack(masks).astype(np.float32)


def _hgrn_kernel(bq_ref, bf_ref, bi_ref, lb_ref, g_ref, s0_ref, mask_ref, o_ref, s_out_ref, st_ref,
                 *, n_t, chunk, heads, mm_dtype):
    c_ = chunk
    g = g_ref[...]
    for hh in range(heads):
        st_ref[hh] = s0_ref[0, 0, hh].T
    row = lax.broadcasted_iota(jnp.int32, (c_, c_), 0)
    col = lax.broadcasted_iota(jnp.int32, (c_, c_), 1)
    lower = jnp.where(col <= row, 1.0, 0.0)
    levels = _hgrn_levels(c_)

    def head_chunk(hh, ds):
        hs = slice(hh * DK_B, (hh + 1) * DK_B)
        lb = lb_ref[:, hs]
        bq = bq_ref[ds, hs]
        qs = bq * _sigmoid(bq)
        f = lb + (1.0 - lb) * _sigmoid(bf_ref[ds, hs])
        kk = 1.0 - f
        v = bi_ref[ds, hs]
        gcum = _dot(lower, jnp.log2(f), precision=lax.Precision.HIGHEST)
        a = jnp.zeros((c_, c_), F32)
        for li, m in enumerate(levels):
            g3 = gcum.reshape(c_ // (2 * m), 2 * m, DK_B)
            ref = jnp.broadcast_to(g3[:, m - 1:m, :], g3.shape).reshape(c_, DK_B)
            qh = qs * jnp.exp2(jnp.minimum(gcum - ref, 0.0))
            kh = kk * jnp.exp2(jnp.minimum(ref - gcum, 0.0))
            a = a + _dot_nt(qh.astype(mm_dtype), kh.astype(mm_dtype)) * mask_ref[li]
        for j in range(HGRN_BLOCK):
            kj = kk if j == 0 else pltpu.roll(kk, j, 0)
            gj = gcum if j == 0 else pltpu.roll(gcum, j, 0)
            e = jnp.exp2(jnp.minimum(gcum - gj, 0.0))
            aj = jnp.sum(qs * kj * e, axis=1, keepdims=True)
            a = a + aj * mask_ref[len(levels) + j]
        st = st_ref[hh]
        o = (_dot_nt((qs * jnp.exp2(gcum)).astype(mm_dtype), st.astype(mm_dtype))
             + _dot(a.astype(mm_dtype), v.astype(mm_dtype)))
        glast = gcum[c_ - 1:c_, :]
        kd = kk * jnp.exp2(glast - gcum)
        st_ref[hh] = jnp.exp2(glast) * st + _dot_tn(v.astype(mm_dtype), kd.astype(mm_dtype))
        o_ref[ds, hs] = o * lax.rsqrt(jnp.mean(o * o, axis=1, keepdims=True) + RMS_EPS) * g

    def chunk_body(ci, carry):
        ds = pl.ds(pl.multiple_of(ci * c_, c_), c_)
        for hh in range(heads):
            head_chunk(hh, ds)
        return carry

    lax.fori_loop(0, n_t // c_, chunk_body, 0)
    for hh in range(heads):
        s_out_ref[0, 0, hh] = st_ref[hh].T


def _hgrn(z, lb, g, s0, layer, n_b, n_t, chunk, heads, mm_dtype):
    w = heads * DK_B
    zcol = lambda c: pl.BlockSpec((n_t, w), lambda b, h, c=c: (b, c // w + h))
    sblk = lambda lyr: pl.BlockSpec((1, 1, heads, DK_B, DV_B), lambda b, h: (lyr, b, h, 0, 0))
    masks = jnp.asarray(_hgrn_masks(chunk))
    return pl.pallas_call(
        functools.partial(_hgrn_kernel, n_t=n_t, chunk=chunk, heads=heads, mm_dtype=mm_dtype),
        grid=(n_b, H_B // heads),
        in_specs=[zcol(C_BQ), zcol(C_BF), zcol(C_BI),
                  pl.BlockSpec((1, w), lambda b, h: (0, h)),
                  pl.BlockSpec((1, DV_B), lambda b, h: (0, 0)),
                  sblk(layer),
                  pl.BlockSpec(masks.shape, lambda b, h: (0, 0, 0))],
        out_specs=[pl.BlockSpec((n_t, w), lambda b, h: (b, h)), sblk(0)],
        out_shape=[jax.ShapeDtypeStruct((n_b * n_t, W_B), F32),
                   jax.ShapeDtypeStruct((1, n_b, H_B, DK_B, DV_B), F32)],
        scratch_shapes=[pltpu.VMEM((heads, DV_B, DK_B), F32)],
        compiler_params=_cparams(("parallel", "arbitrary")),
        name="hgrn2",
    )(z, z, z, lb, g, s0, masks)


def _mem_kv_kernel(x_ref, wk_ref, wv_ref, mk_ref, mv_ref):
    x = x_ref[0].astype(BF16)
    mk = _dot(x, wk_ref[...])
    mv = _dot(x, wv_ref[...])
    heads = lambda a: jnp.stack([a[:, h * DH_M:(h + 1) * DH_M] for h in range(H_M)], axis=0)
    mk_ref[0, 0] = jnp.swapaxes(heads(mk), 0, 1)
    mv_ref[0, 0] = jnp.swapaxes(heads(mv), 0, 1)


def _mem_kv(mem, wk, wv):
    n_b = mem.shape[0]
    wspec = pl.BlockSpec((D_MODEL, W_M), lambda b: (0, 0))
    ospec = pl.BlockSpec((1, 1, N_MEM, H_M, DH_M), lambda b: (0, b, 0, 0, 0))
    sds = jax.ShapeDtypeStruct((1, n_b, N_MEM, H_M, DH_M), F32)
    return pl.pallas_call(
        _mem_kv_kernel,
        grid=(n_b,),
        in_specs=[pl.BlockSpec((1, N_MEM, D_MODEL), lambda b: (b, 0, 0)), wspec, wspec],
        out_specs=[ospec, ospec],
        out_shape=[sds, sds],
        compiler_params=_cparams(("parallel",)),
        name="mem_kv",
    )(mem, wk, wv)


def _mem_kernel(q_ref, mk_ref, mv_ref, o_ref, *, mm_dtype):
    mk = jnp.swapaxes(mk_ref[0, 0], 0, 1).astype(mm_dtype)
    mv = jnp.swapaxes(mv_ref[0, 0], 0, 1).astype(mm_dtype)
    for h in range(H_M):
        hs = slice(h * DH_M, (h + 1) * DH_M)
        q = (q_ref[:, hs] * (DH_M ** -0.5)).astype(mm_dtype)
        s = _dot_nt(q, mk[h])
        p = jnp.exp(s - jnp.max(s, axis=1, keepdims=True))
        l = jnp.sum(p, axis=1, keepdims=True)
        o_ref[:, hs] = _dot(p.astype(mm_dtype), mv[h]) / l


def _mem_attend(z, mk, mv, layer, n_b, n_t, tq, mm_dtype):
    nq = n_t // tq
    kv = pl.BlockSpec((1, 1, N_MEM, H_M, DH_M), lambda b, i: (layer, b, 0, 0, 0))
    return pl.pallas_call(
        functools.partial(_mem_kernel, mm_dtype=mm_dtype),
        grid=(n_b, nq),
        in_specs=[pl.BlockSpec((tq, W_M), lambda b, i: (b * nq + i, C_MQ // W_M)), kv, kv],
        out_specs=pl.BlockSpec((tq, W_M), lambda b, i: (b * nq + i, 0)),
        out_shape=jax.ShapeDtypeStruct((n_b * n_t, W_M), F32),
        compiler_params=_cparams(("parallel", "parallel")),
        name="mem_attend",
    )(z, mk, mv)


def _merge_kernel(a_ref, ag_ref, bn_ref, bg_ref, mo_ref, mg_ref, h_ref, wo_ref, lg_ref, lbias_ref, o_ref):
    a = (a_ref[...] * _silu(ag_ref[...])).astype(BF16)
    b = (bn_ref[...] * _silu(bg_ref[...])).astype(BF16)
    m = (mo_ref[...] * _silu(mg_ref[...])).astype(BF16)
    y = (_dot(a, wo_ref[0:W_A, :]) + _dot(b, wo_ref[W_A:W_A + W_B, :])
         + _dot(m, wo_ref[W_A + W_B:D_MODEL, :]))
    r = ALPHA * h_ref[...] + y
    xc = r - jnp.mean(r, axis=1, keepdims=True)
    var = jnp.mean(xc * xc, axis=1, keepdims=True)
    o_ref[...] = xc * lax.rsqrt(var + LN_EPS) * lg_ref[...] + lbias_ref[...]


def _merge(a_out, z, bn, m_out, h, w_out, ln_g, ln_b, tm):
    m = h.shape[0]
    row = lambda w: pl.BlockSpec((tm, w), lambda i: (i, 0))
    zcol = lambda c, w: pl.BlockSpec((tm, w), lambda i, c=c, w=w: (i, c // w))
    const = lambda s: pl.BlockSpec(s, lambda i: (0, 0), pipeline_mode=pl.Buffered(1))
    return pl.pallas_call(
        _merge_kernel,
        grid=(m // tm,),
        in_specs=[row(W_A), zcol(C_AG, W_A), row(W_B), zcol(C_BG, W_B), row(W_M), zcol(C_MG, W_M),
                  row(D_MODEL), const((D_MODEL, D_MODEL)), const((1, D_MODEL)), const((1, D_MODEL))],
        out_specs=row(D_MODEL),
        out_shape=jax.ShapeDtypeStruct((m, D_MODEL), F32),
        compiler_params=_cparams(("parallel",)),
        name="merge",
    )(a_out, z, bn, z, m_out, z, h, w_out, ln_g, ln_b)


PAGES_PER_STEP = 8


def _dsa_sample_select_kernel(pt_ref, pool_ref, lq_ref, sg_ref, iknew_ref, bias_ref, key_ref, page_buf, sem,
                              *, layer, n_t, n_pages, nsel):
    r = pl.program_id(0)
    slot = r % 2
    n_keys_pad = (n_pages + 1) * PAGE_SIZE

    def page_copy(req, p, s):
        return pltpu.make_async_copy(pool_ref.at[layer, pt_ref[req, p]], page_buf.at[s, p], sem.at[s])

    def for_pages(fn):
        def body(p, carry):
            fn(p)
            return carry
        lax.fori_loop(0, n_pages, body, 0)

    @pl.when(r == 0)
    def _():
        for_pages(lambda p: page_copy(0, p, 0).start())

    @pl.when(r + 1 < pl.num_programs(0))
    def _():
        for_pages(lambda p: page_copy(r + 1, p, 1 - slot).start())

    for_pages(lambda p: page_copy(r, p, slot).wait())

    lq = lq_ref[0]
    sg = sg_ref[0]

    def score(ik_t):
        d = _dot(lq, ik_t.astype(BF16))
        r = jnp.maximum(d, 0.0) * jnp.concatenate([sg] * (ik_t.shape[1] // PAGE_SIZE), axis=1)
        s = r[0:n_t]
        for h in range(1, H_IDX):
            s = s + r[h * n_t:(h + 1) * n_t]
        return s

    width = PAGES_PER_STEP * PAGE_SIZE

    rows = pl.ds(pl.multiple_of(r * n_t, n_t), n_t)

    def group_body(gi, carry):
        base = gi * PAGES_PER_STEP
        ik_step = jnp.concatenate([page_buf[slot, base + u] for u in range(PAGES_PER_STEP)], axis=1)
        key_ref[rows, pl.ds(pl.multiple_of(gi * width, width), width)] = score(ik_step)
        return carry

    lax.fori_loop(0, n_pages // PAGES_PER_STEP, group_body, 0)

    past = n_pages * PAGE_SIZE
    s_new = score(iknew_ref[0])
    t_i = lax.broadcasted_iota(jnp.int32, (n_t, PAGE_SIZE), 0)
    k_i = lax.broadcasted_iota(jnp.int32, (n_t, PAGE_SIZE), 1)
    key_ref[rows, past:n_keys_pad] = jnp.where(k_i <= t_i, s_new, -jnp.inf)

    @pl.when(r == pl.num_programs(0) - 1)
    def _():
        n_rows = key_ref.shape[0]
        n_tiles = n_keys_pad // LANE

        def count_ge(cand):
            part = jnp.zeros((n_rows, LANE), F32)
            for c in range(n_tiles):
                part = part + jnp.where(key_ref[:, c * LANE:(c + 1) * LANE] >= cand, 1.0, 0.0)
            return _lane_total(part)

        tau_r, c_ge_r = _radix_select(count_ge, (n_rows, LANE), float(n_keys_pad), float(nsel))
        tau = tau_r[:, 0:1]
        tau_eff = jnp.maximum(tau, F32_LOWEST)
        bias_ref[...] = jnp.where(key_ref[...] >= tau_eff, 0.0, NEG)

        tie_rows = jnp.logical_and(c_ge_r > float(nsel), tau_r > -jnp.inf)
        has_tie = jnp.max(jnp.where(tie_rows, 1.0, 0.0)) > 0.5

        @pl.when(has_tie)
        def _():
            c_gt = jnp.sum(jnp.where(key_ref[...] > tau, 1.0, 0.0), axis=1, keepdims=True)
            need = float(nsel) - c_gt
            ri = lax.broadcasted_iota(jnp.int32, (PAGE_SIZE, PAGE_SIZE), 0)
            ci = lax.broadcasted_iota(jnp.int32, (PAGE_SIZE, PAGE_SIZE), 1)
            upper = jnp.where(ri <= ci, 1.0, 0.0)
            real = tau > -jnp.inf

            def tie_body(c, run):
                ds = pl.ds(pl.multiple_of(c * PAGE_SIZE, PAGE_SIZE), PAGE_SIZE)
                kc = key_ref[:, ds]
                eqf = jnp.where(kc == tau, 1.0, 0.0)
                pre = _dot(eqf, upper) + run
                keep = jnp.where(kc > tau, 1.0, jnp.where(real, eqf * jnp.where(pre <= need, 1.0, 0.0), 0.0))
                bias_ref[:, ds] = jnp.where(keep > 0.5, 0.0, NEG)
                return run + jnp.sum(eqf, axis=1, keepdims=True)

            lax.fori_loop(0, n_keys_pad // PAGE_SIZE, tie_body, jnp.zeros((n_rows, 1), F32))


def _dsa_sample_select(page_table, idx_pool, layer, lq, sgb, iknew, n_b, n_t, nsel):
    n_pages = page_table.shape[1]
    n_keys_pad = (n_pages + 1) * PAGE_SIZE
    per_req = lambda s: pl.BlockSpec((1,) + s, lambda r, pt: (r, 0, 0))
    return pl.pallas_call(
        functools.partial(_dsa_sample_select_kernel, layer=layer, n_t=n_t, n_pages=n_pages, nsel=nsel),
        grid_spec=pltpu.PrefetchScalarGridSpec(
            num_scalar_prefetch=1,
            grid=(n_b,),
            in_specs=[pl.BlockSpec(memory_space=pl.ANY),
                      per_req((H_IDX * n_t, D_IDX)), per_req((H_IDX * n_t, PAGE_SIZE)), per_req((D_IDX, PAGE_SIZE))],
            out_specs=pl.BlockSpec((n_b * n_t, n_keys_pad), lambda r, pt: (0, 0)),
            scratch_shapes=[pltpu.VMEM((n_b * n_t, n_keys_pad), F32),
                            pltpu.VMEM((2, n_pages, D_IDX, PAGE_SIZE), F32),
                            pltpu.SemaphoreType.DMA((2,))]),
        out_shape=jax.ShapeDtypeStruct((n_b * n_t, n_keys_pad), F32),
        compiler_params=_cparams(("arbitrary",)),
        name="dsa_sample_select",
    )(page_table, idx_pool, lq, sgb, iknew)


def _dsa_sample_attend_kernel(pt_ref, *refs, n_t, steps):
    kp = refs[:PAGES_PER_STEP]
    vp = refs[PAGES_PER_STEP:2 * PAGES_PER_STEP]
    qbd_ref, bias_ref, knew_ref, vnew_ref, o_ref, m_ref, l_ref, acc_ref = refs[2 * PAGES_PER_STEP:]
    j = pl.program_id(1)
    rows = H_A * n_t

    @pl.when(j == 0)
    def _():
        m_ref[...] = jnp.full(m_ref.shape, NEG, F32)
        l_ref[...] = jnp.zeros(l_ref.shape, F32)
        acc_ref[...] = jnp.zeros(acc_ref.shape, F32)

    qbd = qbd_ref[0]

    def update(kblk, vblk, bias):
        n = kblk.shape[0]
        s = _dot_nt(qbd, kblk) + jnp.concatenate([bias] * H_A, axis=0)
        m_prev = m_ref[...]
        m_new = jnp.maximum(m_prev, jnp.max(s, axis=1, keepdims=True))
        alpha = jnp.exp2(m_prev - m_new)
        p = jnp.exp2(s - jnp.concatenate([m_new] * (n // LANE), axis=1))
        l_ref[...] = alpha * l_ref[...] + jnp.sum(p, axis=1, keepdims=True)
        pv = _dot(p.astype(BF16), vblk)
        diag = jnp.concatenate(
            [pv[h * n_t:(h + 1) * n_t, h * DH_A:(h + 1) * DH_A] for h in range(H_A)], axis=0)
        acc_ref[...] = alpha * acc_ref[...] + diag
        m_ref[...] = m_new

    def page2d(ref):
        return jnp.concatenate(
            [ref[0, 0, pl.ds(h, PAGE_SIZE, stride=H_A), :] for h in range(H_A)], axis=1).astype(BF16)

    kblk = jnp.concatenate([page2d(kp[u]) for u in range(PAGES_PER_STEP)], axis=0)
    vblk = jnp.concatenate([page2d(vp[u]) for u in range(PAGES_PER_STEP)], axis=0)
    width = PAGES_PER_STEP * PAGE_SIZE
    off = pl.multiple_of(j * width, width)
    update(kblk, vblk, bias_ref[0, :, pl.ds(off, width)])

    @pl.when(j == steps - 1)
    def _():
        past = steps * width
        update(page2d(knew_ref), page2d(vnew_ref), bias_ref[0, :, past:past + PAGE_SIZE])
        o = acc_ref[...] / l_ref[...]
        o_ref[0] = jnp.concatenate([o[h * n_t:(h + 1) * n_t] for h in range(H_A)], axis=1)


def _dsa_sample_attend(page_table, k_pool, v_pool, layer, qbd, bias, knew, vnew, n_b, n_t):
    n_pages = page_table.shape[1]
    steps = n_pages // PAGES_PER_STEP
    n_keys_pad = (n_pages + 1) * PAGE_SIZE
    page_spec = lambda u: pl.BlockSpec(
        (1, 1, PAGE_SIZE * H_A, DH_A), lambda r, j, pt, u=u: (layer, pt[r, j * PAGES_PER_STEP + u], 0, 0))
    per_req = lambda s: pl.BlockSpec((1,) + s, lambda r, j, pt: (r, 0, 0))
    rows = H_A * n_t
    return pl.pallas_call(
        functools.partial(_dsa_sample_attend_kernel, n_t=n_t, steps=steps),
        grid_spec=pltpu.PrefetchScalarGridSpec(
            num_scalar_prefetch=1,
            grid=(n_b, steps),
            in_specs=[page_spec(u) for u in range(PAGES_PER_STEP)] * 2
            + [per_req((rows, W_A)), per_req((n_t, n_keys_pad))]
            + [pl.BlockSpec((1, 1, PAGE_SIZE * H_A, DH_A), lambda r, j, pt: (r, 0, 0, 0))] * 2,
            out_specs=per_req((n_t, W_A)),
            scratch_shapes=[pltpu.VMEM((rows, LANE), F32), pltpu.VMEM((rows, LANE), F32),
                            pltpu.VMEM((rows, DH_A), F32)]),
        out_shape=jax.ShapeDtypeStruct((n_b, n_t, W_A), F32),
        compiler_params=_cparams(("parallel", "arbitrary")),
        name="dsa_sample_attend",
    )(page_table, *([k_pool] * PAGES_PER_STEP), *([v_pool] * PAGES_PER_STEP), qbd, bias, knew, vnew)


def kernel(x_prompt, x_sample, mem_prompt, cache_k, cache_v, cache_idx_k, state_hgrn, cache_mem_k,
           cache_mem_v, page_table, w_in, lb_logits, hgrn_norm_g, w_mem_k, w_mem_v, w_out, ln_g, ln_b):
    n_bp, t_p, _ = x_prompt.shape
    n_bs, t_s, _ = x_sample.shape
    n_pages = page_table.shape[1]
    past = n_pages * PAGE_SIZE
    l = 0

    lb_all = jnp.cumsum(jax.nn.softmax(lb_logits.astype(F32), axis=0), axis=0)
    lb = lb_all[l][None, :]
    g_norm = hgrn_norm_g[l][None, :]
    w_t = _cast_rows(jnp.swapaxes(w_in, 1, 2), l, tr=1024)
    w_o = w_out[l].astype(BF16)
    lg, lbias = ln_g[l][None, :], ln_b[l][None, :]

    xp = x_prompt.reshape(n_bp * t_p, D_MODEL)
    z, zs = _proj(xp, w_t, tm=1024, tn=1024)
    pos_p = jnp.arange(t_p, dtype=jnp.int32)
    tabs = _rope_tables(pos_p, DH_A) + _rope_tables(pos_p, D_IDX)
    q, kf, kb, vf, vt, iqw, sgn_t, ikf, ike, iko = _prep(z, zs, tabs, tm=256)
    a_out = _dsa_prompt(q, kb, vt, iqw, sgn_t, ike, iko, n_bp, t_p, tq=256)
    s0 = jnp.zeros((1, n_bp, H_B, DK_B, DV_B), F32)
    bn, s_p = _hgrn(z, lb, g_norm, s0, 0, n_bp, t_p, chunk=128, heads=H_B, mm_dtype=BF16)
    mk_p, mv_p = _mem_kv(mem_prompt, w_mem_k[l].astype(BF16), w_mem_v[l].astype(BF16))
    m_out = _mem_attend(z, mk_p, mv_p, 0, n_bp, t_p, tq=512, mm_dtype=BF16)
    y_p = _merge(a_out, z, bn, m_out, xp, w_o, lg, lbias, tm=256)

    n_s = n_bs * t_s
    xs = x_sample.reshape(n_s, D_MODEL)
    z2, zs2 = _proj(xs, w_t, tm=n_s, tn=1024)
    pos_s = past + jnp.arange(t_s, dtype=jnp.int32)
    tabs_s = [jnp.tile(t, (n_bs, 1)) for t in _rope_tables(pos_s, DH_A) + _rope_tables(pos_s, D_IDX)]
    q2, kf2, kb2, vf2, vt2, iqw2, sgn_t2, ikf2, ike2, iko2 = _prep(z2, zs2, tabs_s, tm=n_s)
    nsel_s = min(TOPK_MAX, (past + t_s) // 4)
    lq = iqw2.reshape(n_bs, t_s, H_IDX, D_IDX).transpose(0, 2, 1, 3).reshape(n_bs, H_IDX * t_s, D_IDX)
    sgb = jnp.broadcast_to(
        sgn_t2.reshape(H_IDX, n_bs, t_s).transpose(1, 0, 2).reshape(n_bs, H_IDX * t_s, 1),
        (n_bs, H_IDX * t_s, PAGE_SIZE))
    pad_keys = lambda a: jnp.pad(a.reshape((n_bs, t_s) + a.shape[1:]),
                                 ((0, 0), (0, PAGE_SIZE - t_s)) + ((0, 0),) * (a.ndim - 1))
    idx_pool_t = jnp.swapaxes(cache_idx_k, 2, 3)
    bias = _dsa_sample_select(page_table, idx_pool_t, l, lq, sgb, jnp.swapaxes(pad_keys(ikf2), 1, 2),
                              n_bs, t_s, nsel_s).reshape(n_bs, t_s, -1)
    q4 = q2.reshape(n_bs, t_s, H_A, DH_A)
    eye = jnp.eye(H_A, dtype=q2.dtype)
    qbd = (q4[:, None, :, :, :] * eye[None, :, None, :, None]).reshape(n_bs, H_A * t_s, W_A)
    n_layers, n_phys = cache_k.shape[:2]
    as_rows = lambda a: a.reshape(a.shape[0], a.shape[1], PAGE_SIZE * H_A, DH_A)
    a_out2 = _dsa_sample_attend(page_table, as_rows(cache_k), as_rows(cache_v), l, qbd, bias,
                                as_rows(pad_keys(kf2)[:, None]), as_rows(pad_keys(vf2)[:, None]), n_bs, t_s)
    bn2, s_s = _hgrn(z2, lb, g_norm, state_hgrn, l, n_bs, t_s, chunk=t_s, heads=H_B, mm_dtype=F32)
    m_out2 = _mem_attend(z2, cache_mem_k, cache_mem_v, l, n_bs, t_s, tq=t_s, mm_dtype=F32)
    y_s = _merge(a_out2.reshape(n_s, W_A), z2, bn2, m_out2, xs, w_o, lg, lbias, tm=n_s)

    return (y_p.reshape(n_bp, t_p, D_MODEL), y_s.reshape(n_bs, t_s, D_MODEL),
            kf.reshape(1, n_bp, t_p, H_A, DH_A), vf.reshape(1, n_bp, t_p, H_A, DH_A),
            ikf.reshape(1, n_bp, t_p, D_IDX), s_p, mk_p, mv_p,
            kf2.reshape(1, n_bs, t_s, H_A, DH_A), vf2.reshape(1, n_bs, t_s, H_A, DH_A),
            ikf2.reshape(1, n_bs, t_s, D_IDX), s_s)
```

```python
import functools

import numpy as np
import jax
import jax.numpy as jnp
from jax import lax
from jax.experimental import pallas as pl
from jax.experimental.pallas import tpu as pltpu

D_MODEL = 2048
DH_A = 128
H_A = 8
W_A = H_A * DH_A
H_IDX = 16
D_IDX = 64
W_IDX = H_IDX * D_IDX
TOPK_MAX = 256
DK_B = 128
DV_B = 128
H_B = 4
W_B = H_B * DV_B
DH_M = 128
H_M = 4
W_M = H_M * DH_M
N_MEM = 256
PAGE_SIZE = 128
ROPE_THETA = 500000.0
ROPE_FRAC = 4
LN_EPS = 1e-5
RMS_EPS = 1e-6
DEPTH = 1
ALPHA = (2.0 * DEPTH) ** 0.25

LANE = 128
VMEM_LIMIT = 48 * 1024 * 1024

C_AQ, C_AK, C_AV, C_AG, C_IQ = 0, 1024, 2048, 3072, 4096
C_BQ, C_BF, C_BI, C_BG, C_MQ, C_MG = 5120, 5632, 6144, 6656, 7168, 7680
N_MAIN = 8192

NEG = -1e30
Q_SCALE = DH_A ** -0.5 * 1.4426950408889634
ATT_GROUP = H_A
VT_ROWS = DH_A + 16
INT_MIN = -(2 ** 31)
BF16 = jnp.bfloat16
F32 = jnp.float32


def _dot(a, b, **kw):
    return jnp.dot(a, b, preferred_element_type=F32, **kw)


def _dot_nt(a, b):
    return lax.dot_general(a, b, (((1,), (1,)), ((), ())), preferred_element_type=F32)


def _dot_tn(a, b):
    return lax.dot_general(a, b, (((0,), (0,)), ((), ())), preferred_element_type=F32)


def _sigmoid(x):
    return 1.0 / (1.0 + jnp.exp(-x))


def _silu(x):
    return x * _sigmoid(x)


def _cparams(sem):
    return pltpu.CompilerParams(dimension_semantics=sem, vmem_limit_bytes=VMEM_LIMIT)


def _proj_kernel(x_ref, w_ref, ws_ref, o_ref, os_ref, xb_ref):
    j = pl.program_id(1)

    @pl.when(j == 0)
    def _():
        xb = x_ref[...].astype(BF16)
        xb_ref[...] = xb
        os_ref[...] = _dot_nt(xb, ws_ref[...])

    o_ref[...] = _dot_nt(xb_ref[...], w_ref[...])


W_SIDE_ROW = C_IQ + W_IDX
W_SIDE_ROWS = D_IDX + H_IDX
N_IN = N_MAIN + W_SIDE_ROWS


def _proj(x, w_t, tm, tn):
    m = x.shape[0]
    n_lo = W_SIDE_ROW // tn

    def w_rows(i, j):
        return ((j * (tn // 16) + jnp.where(j < n_lo, 0, W_SIDE_ROWS // 16)) * 16, 0)

    return pl.pallas_call(
        _proj_kernel,
        grid=(m // tm, N_MAIN // tn),
        in_specs=[pl.BlockSpec((tm, D_MODEL), lambda i, j: (i, 0)),
                  pl.BlockSpec((pl.Element(tn), pl.Element(D_MODEL)), w_rows),
                  pl.BlockSpec((pl.Element(LANE), pl.Element(D_MODEL)), lambda i, j: (W_SIDE_ROW, 0))],
        out_specs=[pl.BlockSpec((tm, tn), lambda i, j: (i, j)),
                   pl.BlockSpec((tm, LANE), lambda i, j: (i, 0))],
        out_shape=[jax.ShapeDtypeStruct((m, N_MAIN), F32),
                   jax.ShapeDtypeStruct((m, LANE), F32)],
        scratch_shapes=[pltpu.VMEM((tm, D_MODEL), BF16)],
        compiler_params=_cparams(("parallel", "arbitrary")),
        name="in_proj",
    )(x, w_t, w_t)


def _cast_kernel(x_ref, o_ref):
    o_ref[...] = x_ref[0].astype(BF16)


def _cast_rows(w, layer, tr):
    rows = w.shape[1]
    return pl.pallas_call(
        _cast_kernel,
        grid=(pl.cdiv(rows, tr),),
        in_specs=[pl.BlockSpec((1, tr, D_MODEL), lambda i: (layer, i, 0))],
        out_specs=pl.BlockSpec((tr, D_MODEL), lambda i: (i, 0)),
        out_shape=jax.ShapeDtypeStruct((rows, D_MODEL), BF16),
        compiler_params=_cparams(("parallel",)),
        name="w_cast",
    )(w)


def _swap_halves_index(shape, d):
    half = d // ROPE_FRAC // 2
    lane = lax.broadcasted_iota(jnp.int32, shape, 1)
    pos = lane & (d - 1)
    return jnp.where(pos < half, lane + half, jnp.where(pos < 2 * half, lane - half, lane))


def _rope(x, c, s, swap):
    return x * c + jnp.take_along_axis(x, swap, axis=1) * s


def _prep_kernel(aq_ref, ak_ref, av_ref, iq_ref, zs_ref, c1_ref, s1_ref, c2_ref, s2_ref,
                 q_ref, kf_ref, kb_ref, vf_ref, vt_ref, iqw_ref, sgn_ref,
                 ikf_ref, ike_ref, iko_ref):
    c1, s1, c2, s2 = c1_ref[...], s1_ref[...], c2_ref[...], s2_ref[...]
    swap_a = _swap_halves_index(c1.shape, DH_A)
    swap_i = _swap_halves_index(c2.shape, D_IDX)
    k_heads, v_heads = [], []
    for h in range(H_A):
        sl = slice(h * DH_A, (h + 1) * DH_A)
        q_ref[:, sl] = (_rope(aq_ref[:, sl], c1, s1, swap_a) * Q_SCALE).astype(BF16)
        kr = _rope(ak_ref[:, sl], c1, s1, swap_a)
        kb_ref[:, sl] = kr.astype(BF16)
        v = av_ref[:, sl]
        vt_ref[h * VT_ROWS:h * VT_ROWS + DH_A, :] = v.T.astype(BF16)
        vt_ref[h * VT_ROWS + DH_A:(h + 1) * VT_ROWS, :] = jnp.ones((VT_ROWS - DH_A, v.shape[0]), BF16)
        k_heads.append(kr)
        v_heads.append(v)
    kf_ref[...] = jnp.swapaxes(jnp.stack(k_heads, axis=0), 0, 1)
    vf_ref[...] = jnp.swapaxes(jnp.stack(v_heads, axis=0), 0, 1)

    zs = zs_ref[...]
    lane = lax.broadcasted_iota(jnp.int32, zs.shape, 1)
    ikr = _rope(zs, c2, s2, swap_i)
    ikf_ref[...] = ikr[:, :D_IDX]
    ike_ref[...] = jnp.where(lane < D_IDX, ikr, 0.0).astype(BF16)
    iko_ref[...] = jnp.where(lane >= D_IDX, pltpu.roll(ikr, D_IDX, 1), 0.0).astype(BF16)
    sgn_t = jnp.sign(pltpu.roll(zs, LANE - D_IDX, 1)).T
    sgn_ref[...] = sgn_t[:H_IDX, :]
    wscale = (D_IDX ** -0.5) * (H_IDX ** -0.5)
    for p in range(H_IDX // 2):
        sl = slice(p * LANE, (p + 1) * LANE)
        wa = jnp.abs(zs[:, D_IDX + 2 * p:D_IDX + 2 * p + 1])
        wb = jnp.abs(zs[:, D_IDX + 2 * p + 1:D_IDX + 2 * p + 2])
        wpair = jnp.where(lane < D_IDX, wa, wb) * wscale
        iqw_ref[:, sl] = (_rope(iq_ref[:, sl], c2, s2, swap_i) * wpair).astype(BF16)


def _prep(z, zs, tabs, tm):
    m = z.shape[0]
    period = tabs[0].shape[0] // tm
    zblk = lambda c: pl.BlockSpec((tm, 1024), lambda i, c=c: (i, c // 1024))
    tab = pl.BlockSpec((tm, LANE), lambda i: (i % period, 0))
    row = lambda w: pl.BlockSpec((tm, w), lambda i: (i, 0))
    col = lambda w: pl.BlockSpec((w, tm), lambda i: (0, i))
    sds = lambda w, dt: jax.ShapeDtypeStruct((m, w), dt)
    heads =pl.BlockSpec((tm, H_A, DH_A), lambda i: (i, 0, 0))
    heads_sds = jax.ShapeDtypeStruct((m, H_A, DH_A), F32)
    return pl.pallas_call(
        _prep_kernel,
        grid=(m // tm,),
        in_specs=[zblk(C_AQ), zblk(C_AK), zblk(C_AV), zblk(C_IQ), row(LANE)] + [tab] * 4,
        out_specs=[row(W_A), heads, row(W_A), heads, col(H_A * VT_ROWS), row(W_IDX), col(H_IDX),
                   row(D_IDX), row(LANE), row(LANE)],
        out_shape=[sds(W_A, BF16), heads_sds, sds(W_A, BF16), heads_sds,
                   jax.ShapeDtypeStruct((H_A * VT_ROWS, m), BF16), sds(W_IDX, BF16),
                   jax.ShapeDtypeStruct((H_IDX, m), F32),
                   sds(D_IDX, F32), sds(LANE, BF16), sds(LANE, BF16)],
        compiler_params=_cparams(("parallel",)),
        name="rope_prep",
    )(z, z, z, z, zs, *tabs)


def _rope_tables(pos, d):
    r = d // ROPE_FRAC
    half = r // 2
    n = pos.shape[0]
    inv = ROPE_THETA ** (-jnp.arange(half, dtype=F32) / half)
    ang = pos.astype(F32)[:, None] * inv[None, :]
    cos, sin = jnp.cos(ang), jnp.sin(ang)
    c = jnp.concatenate([cos, cos, jnp.ones((n, d - r), F32)], axis=1)
    s = jnp.concatenate([-sin, sin, jnp.zeros((n, d - r), F32)], axis=1)
    reps = LANE // d
    return [jnp.tile(t, (1, reps)) for t in (c, s)]


KEY_NEG_INF = -2139095041
F32_LOWEST = -3.4028234663852886e38


def _key_to_float(key):
    key = jnp.maximum(key, jnp.int32(KEY_NEG_INF))
    b = key ^ (lax.shift_right_arithmetic(key, 31) & jnp.int32(0x7FFFFFFF))
    return lax.bitcast_convert_type(b, F32)


def _lane_total(part):
    return _dot(part.astype(BF16), jnp.ones((LANE, LANE), BF16))


def _radix_select(count_ge, shape, total, nsel):
    def body(bi, carry):
        pu, cacc = carry
        bit = jnp.int32(31) - bi
        cand_u = pu | lax.shift_left(jnp.int32(1), bit)
        cnt = count_ge(_key_to_float(cand_u ^ jnp.int32(INT_MIN)))
        ok = cnt >= nsel
        return jnp.where(ok, cand_u, pu), jnp.where(ok, cnt, cacc)

    pu0 = jnp.zeros(shape, jnp.int32)
    c0 = jnp.full(shape, total, F32)
    pu, cacc = lax.fori_loop(0, 32, body, (pu0, c0))
    return _key_to_float(pu ^ jnp.int32(INT_MIN)), cacc


def _dsa_prompt_kernel(q_ref, k_ref, vt_ref, iq_ref, sgn_ref, ike_ref, iko_ref, gate_ref, o_ref,
                       key_ref, bias_ref, m_ref, acc_ref, *, tq, nsel):
    tk = tq
    i = pl.program_id(1)
    nch = i + 1
    sub = 8

    kidx0 = lax.broadcasted_iota(jnp.int32, (tk, tq), 0)
    qidx = i * tq + lax.broadcasted_iota(jnp.int32, (tk, tq), 1)

    def chunk_slice(c):
        return pl.ds(pl.multiple_of(c * tk, tk), tk)

    def score_body(c, carry):
        ds = chunk_slice(c)
        ke = ike_ref[ds, :]
        ko = iko_ref[ds, :]
        acc = jnp.zeros((tk, tq), F32)
        for h in range(H_IDX):
            rhs = iq_ref[:, (h // 2) * LANE:(h // 2 + 1) * LANE]
            d = _dot_nt(ke if h % 2 == 0 else ko, rhs)
            acc = acc + sgn_ref[h:h + 1, :] * jnp.maximum(d, 0.0)
        key_ref[ds, :] = jnp.where(kidx0 + c * tk <= qidx, acc, -jnp.inf)
        return carry

    lax.fori_loop(0, nch, score_body, 0)

    def count(cmp_fn):
        def hits(c):
            hit = jnp.where(cmp_fn(key_ref[chunk_slice(c), :]), 1.0, 0.0)
            return jnp.sum(hit.reshape(tk // sub, sub, tq), axis=0)

        def body(c2, parts):
            return parts[0] + hits(2 * c2), parts[1] + hits(2 * c2 + 1)

        zero = jnp.zeros((sub, tq), F32)
        p0, p1 = lax.fori_loop(0, nch // 2, body, (zero, zero))
        p0 = lax.cond(nch % 2 == 1, lambda p: p + hits(nch - 1), lambda p: p, p0)
        return jnp.sum(p0 + p1, axis=0, keepdims=True)

    total = (nch * tk).astype(F32)
    tau, c_ge = _radix_select(lambda cand: count(lambda kc: kc >= cand), (1, tq), total, float(nsel))
    tau_eff = jnp.maximum(tau, F32_LOWEST)

    def bias_body(c, carry):
        ds = chunk_slice(c)
        bias_ref[ds, :] = jnp.where(key_ref[ds, :] >= tau_eff, 0.0, NEG)
        return carry

    lax.fori_loop(0, nch, bias_body, 0)

    tie_q = jnp.logical_and(c_ge > float(nsel), tau > -jnp.inf)
    has_tie = jnp.max(jnp.where(tie_q, 1.0, 0.0)) > 0.5

    @pl.when(has_tie)
    def _():
        need = float(nsel) - count(lambda kc: kc > tau)
        lower = jnp.where(lax.broadcasted_iota(jnp.int32, (tk, tk), 1)
                          <= lax.broadcasted_iota(jnp.int32, (tk, tk), 0), 1.0, 0.0).astype(BF16)
        real = tau > -jnp.inf

        def tie_body(c, run):
            ds = chunk_slice(c)
            kc = key_ref[ds, :]
            eqf = jnp.where(kc == tau, 1.0, 0.0)
            pre = _dot(lower, eqf.astype(BF16)) + run
            keep = jnp.where(kc > tau, 1.0, jnp.where(real, eqf * jnp.where(pre <= need, 1.0, 0.0), 0.0))
            bias_ref[ds, :] = jnp.where(keep > 0.5, 0.0, NEG)
            return run + jnp.sum(eqf, axis=0, keepdims=True)

        lax.fori_loop(0, nch, tie_body, jnp.zeros((1, tq), F32))

    m_ref[...] = jnp.full(m_ref.shape, NEG, F32)
    acc_ref[...] = jnp.zeros(acc_ref.shape, F32)

    def attn_body(c, carry):
        ds = chunk_slice(c)
        bias = bias_ref[ds, :]
        for h0 in range(0, H_A, ATT_GROUP):
            group = range(h0, h0 + ATT_GROUP)
            ss = {h: _dot_nt(k_ref[ds, h * DH_A:(h + 1) * DH_A], q_ref[:, h * DH_A:(h + 1) * DH_A])
                  for h in group}
            ps, alphas = {}, {}
            for h in group:
                s = ss[h] + bias
                m_prev = m_ref[h]
                m_new = jnp.maximum(m_prev, jnp.max(s, axis=0, keepdims=True))
                alphas[h] = jnp.exp2(m_prev - m_new)
                ps[h] = jnp.exp2(s - m_new).astype(BF16)
                m_ref[h] = m_new
            for h in group:
                acc_ref[h] = alphas[h] * acc_ref[h] + _dot(vt_ref[h * VT_ROWS:(h + 1) * VT_ROWS, ds], ps[h])
        return carry

    lax.fori_loop(0, nch, attn_body, 0)
    for h in range(H_A):
        acc = acc_ref[h]
        hs = slice(h * DH_A, (h + 1) * DH_A)
        o_ref[:, hs] = ((acc[:DH_A] / acc[DH_A:DH_A + 1]).T * _silu(gate_ref[:, hs])).astype(o_ref.dtype)


def _dsa_prompt(q, k, vt, iqw, sgn_t, ike, iko, z, n_b, n_t, tq):
    nsel = min(TOPK_MAX, n_t // 4)
    nq = n_t // tq
    qblk = lambda w: pl.BlockSpec((tq, w), lambda b, i: (b * nq + i, 0))
    full = lambda w: pl.BlockSpec((n_t, w), lambda b, i: (b, 0))
    return pl.pallas_call(
        functools.partial(_dsa_prompt_kernel, tq=tq, nsel=nsel),
        grid=(n_b, nq),
        in_specs=[qblk(W_A), full(W_A), pl.BlockSpec((H_A * VT_ROWS, n_t), lambda b, i: (0, b)), qblk(W_IDX),
                  pl.BlockSpec((H_IDX, tq), lambda b, i: (0, b * nq + i)), full(LANE), full(LANE),
                  pl.BlockSpec((tq, W_A), lambda b, i: (b * nq + i, C_AG // W_A))],
        out_specs=qblk(W_A),
        out_shape=jax.ShapeDtypeStruct((n_b * n_t, W_A), BF16),
        scratch_shapes=[pltpu.VMEM((n_t, tq), F32),
                        pltpu.VMEM((n_t, tq), F32),
                        pltpu.VMEM((H_A, 1, tq), F32),
                        pltpu.VMEM((H_A, VT_ROWS, tq), F32)],
        compiler_params=_cparams(("parallel", "arbitrary")),
        name="dsa_prompt",
    )(q, k, vt, iqw, sgn_t, ike, iko, z)


HGRN_BLOCK = 8


def _hgrn_levels(chunk):
    levels, m = [], HGRN_BLOCK
    while m < chunk:
        levels.append(m)
        m *= 2
    return levels


def _hgrn_masks(chunk):
    row, col = np.indices((chunk, chunk))
    masks = []
    for m in _hgrn_levels(chunk):
        same_pair = (row // (2 * m)) == (col // (2 * m))
        masks.append(same_pair & (row % (2 * m) >= m) & (col % (2 * m) < m))
    for j in range(HGRN_BLOCK):
        masks.append((row - col == j) & (row % HGRN_BLOCK >= j))
    return np.stack(masks).astype(np.float32)


def _hgrn_kernel(bq_ref, bf_ref, bi_ref, gate_ref, lb_ref, g_ref, s0_ref, mask_ref, o_ref, s_out_ref, st_ref,
                 *, n_t, chunk, heads, mm_dtype):
    c_ = chunk
    g = g_ref[...]
    for hh in range(heads):
        st_ref[hh] = s0_ref[0, 0, hh].T
    row = lax.broadcasted_iota(jnp.int32, (c_, c_), 0)
    col = lax.broadcasted_iota(jnp.int32, (c_, c_), 1)
    lower = jnp.where(col <= row, 1.0, 0.0)
    levels = _hgrn_levels(c_)

    def head_chunk(hh, ds):
        hs = slice(hh * DK_B, (hh + 1) * DK_B)
        lb = lb_ref[:, hs]
        bq = bq_ref[ds, hs]
        qs = bq * _sigmoid(bq)
        f = lb + (1.0 - lb) * _sigmoid(bf_ref[ds, hs])
        kk = 1.0 - f
        v = bi_ref[ds, hs]
        gcum = _dot(lower, jnp.log2(f), precision=lax.Precision.HIGHEST)
        a = jnp.zeros((c_, c_), F32)
        for li, m in enumerate(levels):
            g3 = gcum.reshape(c_ // (2 * m), 2 * m, DK_B)
            ref = jnp.broadcast_to(g3[:, m - 1:m, :], g3.shape).reshape(c_, DK_B)
            qh = qs * jnp.exp2(jnp.minimum(gcum - ref, 0.0))
            kh = kk * jnp.exp2(jnp.minimum(ref - gcum, 0.0))
            a = a + _dot_nt(qh.astype(mm_dtype), kh.astype(mm_dtype)) * mask_ref[li]
        for j in range(HGRN_BLOCK):
            kj = kk if j == 0 else pltpu.roll(kk, j, 0)
            gj = gcum if j == 0 else pltpu.roll(gcum, j, 0)
            e = jnp.exp2(jnp.minimum(gcum - gj, 0.0))
            aj = jnp.sum(qs * kj * e, axis=1, keepdims=True)
            a = a + aj * mask_ref[len(levels) + j]
        st = st_ref[hh]
        o = (_dot_nt((qs * jnp.exp2(gcum)).astype(mm_dtype), st.astype(mm_dtype))
             + _dot(a.astype(mm_dtype), v.astype(mm_dtype)))
        glast = gcum[c_ - 1:c_, :]
        kd = kk * jnp.exp2(glast - gcum)
        st_ref[hh] = jnp.exp2(glast) * st + _dot_tn(v.astype(mm_dtype), kd.astype(mm_dtype))
        bn = o * lax.rsqrt(jnp.mean(o * o, axis=1, keepdims=True) + RMS_EPS) * g
        o_ref[ds, hs] = (bn * _silu(gate_ref[ds, hs])).astype(o_ref.dtype)

    def chunk_body(ci, carry):
        ds = pl.ds(pl.multiple_of(ci * c_, c_), c_)
        for hh in range(heads):
            head_chunk(hh, ds)
        return carry

    lax.fori_loop(0, n_t // c_, chunk_body, 0)
    for hh in range(heads):
        s_out_ref[0, 0, hh] = st_ref[hh].T


def _hgrn(z, lb, g, s0, layer, n_b, n_t, chunk, heads, mm_dtype, out_dtype):
    w = heads * DK_B
    zcol = lambda c: pl.BlockSpec((n_t, w), lambda b, h, c=c: (b, c // w + h))
    sblk = lambda lyr: pl.BlockSpec((1, 1, heads, DK_B, DV_B), lambda b, h: (lyr, b, h, 0, 0))
    masks = jnp.asarray(_hgrn_masks(chunk))
    return pl.pallas_call(
        functools.partial(_hgrn_kernel, n_t=n_t, chunk=chunk, heads=heads, mm_dtype=mm_dtype),
        grid=(n_b, H_B // heads),
        in_specs=[zcol(C_BQ), zcol(C_BF), zcol(C_BI), zcol(C_BG),
                  pl.BlockSpec((1, w), lambda b, h: (0, h)),
                  pl.BlockSpec((1, DV_B), lambda b, h: (0, 0)),
                  sblk(layer),
                  pl.BlockSpec(masks.shape, lambda b, h: (0, 0, 0))],
        out_specs=[pl.BlockSpec((n_t, w), lambda b, h: (b, h)), sblk(0)],
        out_shape=[jax.ShapeDtypeStruct((n_b * n_t, W_B), out_dtype),
                   jax.ShapeDtypeStruct((1, n_b, H_B, DK_B, DV_B), F32)],
        scratch_shapes=[pltpu.VMEM((heads, DV_B, DK_B), F32)],
        compiler_params=_cparams(("parallel", "arbitrary")),
        name="hgrn2",
    )(z, z, z, z, lb, g, s0, masks)


def _mem_kv_kernel(x_ref, wk_ref, wv_ref, mk_ref, mv_ref):
    x = x_ref[0].astype(BF16)
    mk = _dot(x, wk_ref[...])
    mv = _dot(x, wv_ref[...])
    heads = lambda a: jnp.stack([a[:, h * DH_M:(h + 1) * DH_M] for h in range(H_M)], axis=0)
    mk_ref[0, 0] = jnp.swapaxes(heads(mk), 0, 1)
    mv_ref[0, 0] = jnp.swapaxes(heads(mv), 0, 1)


def _mem_kv(mem, wk, wv):
    n_b = mem.shape[0]
    wspec = pl.BlockSpec((D_MODEL, W_M), lambda b: (0, 0))
    ospec = pl.BlockSpec((1, 1, N_MEM, H_M, DH_M), lambda b: (0, b, 0, 0, 0))
    sds = jax.ShapeDtypeStruct((1, n_b, N_MEM, H_M, DH_M), F32)
    return pl.pallas_call(
        _mem_kv_kernel,
        grid=(n_b,),
        in_specs=[pl.BlockSpec((1, N_MEM, D_MODEL), lambda b: (b, 0, 0)), wspec, wspec],
        out_specs=[ospec, ospec],
        out_shape=[sds, sds],
        compiler_params=_cparams(("parallel",)),
        name="mem_kv",
    )(mem, wk, wv)


def _mem_kernel(q_ref, gate_ref, mk_ref, mv_ref, o_ref, *, mm_dtype):
    mk = jnp.swapaxes(mk_ref[0, 0], 0, 1).astype(mm_dtype)
    mv = jnp.swapaxes(mv_ref[0, 0], 0, 1).astype(mm_dtype)
    for h in range(H_M):
        hs = slice(h * DH_M, (h + 1) * DH_M)
        q = (q_ref[:, hs] * (DH_M ** -0.5)).astype(mm_dtype)
        s = _dot_nt(q, mk[h])
        p = jnp.exp(s - jnp.max(s, axis=1, keepdims=True))
        l = jnp.sum(p, axis=1, keepdims=True)
        o = _dot(p.astype(mm_dtype), mv[h]) / l
        o_ref[:, hs] = (o * _silu(gate_ref[:, hs])).astype(o_ref.dtype)


def _mem_attend(z, mk, mv, layer, n_b, n_t, tq, mm_dtype, out_dtype):
    nq = n_t // tq
    kv = pl.BlockSpec((1, 1, N_MEM, H_M, DH_M), lambda b, i: (layer, b, 0, 0, 0))
    zcol = lambda c: pl.BlockSpec((tq, W_M), lambda b, i, c=c: (b * nq + i, c // W_M))
    return pl.pallas_call(
        functools.partial(_mem_kernel, mm_dtype=mm_dtype),
        grid=(n_b, nq),
        in_specs=[zcol(C_MQ), zcol(C_MG), kv, kv],
        out_specs=pl.BlockSpec((tq, W_M), lambda b, i: (b * nq + i, 0)),
        out_shape=jax.ShapeDtypeStruct((n_b * n_t, W_M), out_dtype),
        compiler_params=_cparams(("parallel", "parallel")),
        name="mem_attend",
    )(z, z, mk, mv)


def _merge_kernel(a_ref, b_ref, m_ref, h_ref, wo_ref, lg_ref, lbias_ref, o_ref):
    y = (_dot(a_ref[...].astype(BF16), wo_ref[0:W_A, :]) + _dot(b_ref[...].astype(BF16), wo_ref[W_A:W_A + W_B, :])
         + _dot(m_ref[...].astype(BF16), wo_ref[W_A + W_B:D_MODEL, :]))
    r = ALPHA * h_ref[...] + y
    xc = r - jnp.mean(r, axis=1, keepdims=True)
    var = jnp.mean(xc * xc, axis=1, keepdims=True)
    o_ref[...] = xc * lax.rsqrt(var + LN_EPS) * lg_ref[...] + lbias_ref[...]


def _merge(a, b, m_br, h, w_out, ln_g, ln_b, tm):
    m = h.shape[0]
    row = lambda w: pl.BlockSpec((tm, w), lambda i: (i, 0))
    const = lambda s: pl.BlockSpec(s, lambda i: (0, 0), pipeline_mode=pl.Buffered(1))
    return pl.pallas_call(
        _merge_kernel,
        grid=(m // tm,),
        in_specs=[row(W_A), row(W_B), row(W_M), row(D_MODEL),
                  const((D_MODEL, D_MODEL)), const((1, D_MODEL)), const((1, D_MODEL))],
        out_specs=row(D_MODEL),
        out_shape=jax.ShapeDtypeStruct((m, D_MODEL), F32),
        compiler_params=_cparams(("parallel",)),
        name="merge",
    )(a, b, m_br, h, w_out, ln_g, ln_b)


PAGES_PER_STEP = 8


def _dsa_sample_select_kernel(pt_ref, pool_ref, lq_ref, sg_ref, iknew_ref, bias_ref, key_ref, page_buf, sem,
                              *, layer, n_t, n_pages, nsel):
    r = pl.program_id(0)
    slot = r % 2
    n_keys_pad = (n_pages + 1) * PAGE_SIZE

    def page_copy(req, p, s):
        return pltpu.make_async_copy(pool_ref.at[layer, pt_ref[req, p]], page_buf.at[s, p], sem.at[s])

    def for_pages(fn):
        def body(p, carry):
            fn(p)
            return carry
        lax.fori_loop(0, n_pages, body, 0)

    @pl.when(r == 0)
    def _():
        for_pages(lambda p: page_copy(0, p, 0).start())

    @pl.when(r + 1 < pl.num_programs(0))
    def _():
        for_pages(lambda p: page_copy(r + 1, p, 1 - slot).start())

    for_pages(lambda p: page_copy(r, p, slot).wait())

    lq = lq_ref[0]
    sg = sg_ref[0]

    def score(ik_t):
        d = _dot(lq, ik_t.astype(BF16))
        r = jnp.maximum(d, 0.0) * jnp.concatenate([sg] * (ik_t.shape[1] // PAGE_SIZE), axis=1)
        s = r[0:n_t]
        for h in range(1, H_IDX):
            s = s + r[h * n_t:(h + 1) * n_t]
        return s

    width = PAGES_PER_STEP * PAGE_SIZE

    rows = pl.ds(pl.multiple_of(r * n_t, n_t), n_t)

    def group_body(gi, carry):
        base = gi * PAGES_PER_STEP
        ik_step = jnp.concatenate([page_buf[slot, base + u] for u in range(PAGES_PER_STEP)], axis=1)
        key_ref[rows, pl.ds(pl.multiple_of(gi * width, width), width)] = score(ik_step)
        return carry

    lax.fori_loop(0, n_pages // PAGES_PER_STEP, group_body, 0)

    past = n_pages * PAGE_SIZE
    s_new = score(iknew_ref[0])
    t_i = lax.broadcasted_iota(jnp.int32, (n_t, PAGE_SIZE), 0)
    k_i = lax.broadcasted_iota(jnp.int32, (n_t, PAGE_SIZE), 1)
    key_ref[rows, past:n_keys_pad] = jnp.where(k_i <= t_i, s_new, -jnp.inf)

    @pl.when(r == pl.num_programs(0) - 1)
    def _():
        n_rows = key_ref.shape[0]
        n_tiles = n_keys_pad // LANE

        def count_ge(cand):
            part = jnp.zeros((n_rows, LANE), F32)
            for c in range(n_tiles):
                part = part + jnp.where(key_ref[:, c * LANE:(c + 1) * LANE] >= cand, 1.0, 0.0)
            return _lane_total(part)

        tau_r, c_ge_r = _radix_select(count_ge, (n_rows, LANE), float(n_keys_pad), float(nsel))
        tau = tau_r[:, 0:1]
        tau_eff = jnp.maximum(tau, F32_LOWEST)
        bias_ref[...] = jnp.where(key_ref[...] >= tau_eff, 0.0, NEG)

        tie_rows = jnp.logical_and(c_ge_r > float(nsel), tau_r > -jnp.inf)
        has_tie = jnp.max(jnp.where(tie_rows, 1.0, 0.0)) > 0.5

        @pl.when(has_tie)
        def _():
            c_gt = jnp.sum(jnp.where(key_ref[...] > tau, 1.0, 0.0), axis=1, keepdims=True)
            need = float(nsel) - c_gt
            ri = lax.broadcasted_iota(jnp.int32, (PAGE_SIZE, PAGE_SIZE), 0)
            ci = lax.broadcasted_iota(jnp.int32, (PAGE_SIZE, PAGE_SIZE), 1)
            upper = jnp.where(ri <= ci, 1.0, 0.0)
            real = tau > -jnp.inf

            def tie_body(c, run):
                ds = pl.ds(pl.multiple_of(c * PAGE_SIZE, PAGE_SIZE), PAGE_SIZE)
                kc = key_ref[:, ds]
                eqf = jnp.where(kc == tau, 1.0, 0.0)
                pre = _dot(eqf, upper) + run
                keep = jnp.where(kc > tau, 1.0, jnp.where(real, eqf * jnp.where(pre <= need, 1.0, 0.0), 0.0))
                bias_ref[:, ds] = jnp.where(keep > 0.5, 0.0, NEG)
                return run + jnp.sum(eqf, axis=1, keepdims=True)

            lax.fori_loop(0, n_keys_pad // PAGE_SIZE, tie_body, jnp.zeros((n_rows, 1), F32))


def _dsa_sample_select(page_table, idx_pool, layer, lq, sgb, iknew, n_b, n_t, nsel):
    n_pages = page_table.shape[1]
    n_keys_pad = (n_pages + 1) * PAGE_SIZE
    per_req = lambda s: pl.BlockSpec((1,) + s, lambda r, pt: (r, 0, 0))
    return pl.pallas_call(
        functools.partial(_dsa_sample_select_kernel, layer=layer, n_t=n_t, n_pages=n_pages, nsel=nsel),
        grid_spec=pltpu.PrefetchScalarGridSpec(
            num_scalar_prefetch=1,
            grid=(n_b,),
            in_specs=[pl.BlockSpec(memory_space=pl.ANY),
                      per_req((H_IDX * n_t, D_IDX)), per_req((H_IDX * n_t, PAGE_SIZE)), per_req((D_IDX, PAGE_SIZE))],
            out_specs=pl.BlockSpec((n_b * n_t, n_keys_pad), lambda r, pt: (0, 0)),
            scratch_shapes=[pltpu.VMEM((n_b * n_t, n_keys_pad), F32),
                            pltpu.VMEM((2, n_pages, D_IDX, PAGE_SIZE), F32),
                            pltpu.SemaphoreType.DMA((2,))]),
        out_shape=jax.ShapeDtypeStruct((n_b * n_t, n_keys_pad), F32),
        compiler_params=_cparams(("arbitrary",)),
        name="dsa_sample_select",
    )(page_table, idx_pool, lq, sgb, iknew)


def _dsa_sample_attend_kernel(pt_ref, *refs, n_t, steps):
    kp = refs[:PAGES_PER_STEP]
    vp = refs[PAGES_PER_STEP:2 * PAGES_PER_STEP]
    qbd_ref, bias_ref, knew_ref, vnew_ref, gate_ref, o_ref, m_ref, l_ref, acc_ref = refs[2 * PAGES_PER_STEP:]
    j = pl.program_id(1)
    rows = H_A * n_t

    @pl.when(j == 0)
    def _():
        m_ref[...] = jnp.full(m_ref.shape, NEG, F32)
        l_ref[...] = jnp.zeros(l_ref.shape, F32)
        acc_ref[...] = jnp.zeros(acc_ref.shape, F32)

    qbd = qbd_ref[0]

    def update(kblk, vblk, bias):
        n = kblk.shape[0]
        s = _dot_nt(qbd, kblk) + jnp.concatenate([bias] * H_A, axis=0)
        m_prev = m_ref[...]
        m_new = jnp.maximum(m_prev, jnp.max(s, axis=1, keepdims=True))
        alpha = jnp.exp2(m_prev - m_new)
        p = jnp.exp2(s - jnp.concatenate([m_new] * (n // LANE), axis=1))
        l_ref[...] = alpha * l_ref[...] + jnp.sum(p, axis=1, keepdims=True)
        pv = _dot(p.astype(BF16), vblk)
        diag = jnp.concatenate(
            [pv[h * n_t:(h + 1) * n_t, h * DH_A:(h + 1) * DH_A] for h in range(H_A)], axis=0)
        acc_ref[...] = alpha * acc_ref[...] + diag
        m_ref[...] = m_new

    def page2d(ref):
        return jnp.concatenate(
            [ref[0, 0, pl.ds(h, PAGE_SIZE, stride=H_A), :] for h in range(H_A)], axis=1).astype(BF16)

    kblk = jnp.concatenate([page2d(kp[u]) for u in range(PAGES_PER_STEP)], axis=0)
    vblk = jnp.concatenate([page2d(vp[u]) for u in range(PAGES_PER_STEP)], axis=0)
    width = PAGES_PER_STEP * PAGE_SIZE
    off = pl.multiple_of(j * width, width)
    update(kblk, vblk, bias_ref[0, :, pl.ds(off, width)])

    @pl.when(j == steps - 1)
    def _():
        past = steps * width
        update(page2d(knew_ref), page2d(vnew_ref), bias_ref[0, :, past:past + PAGE_SIZE])
        o = acc_ref[...] / l_ref[...]
        o = jnp.concatenate([o[h * n_t:(h + 1) * n_t] for h in range(H_A)], axis=1)
        o_ref[0] = o * _silu(gate_ref[...])


def _dsa_sample_attend(page_table, k_pool, v_pool, layer, qbd, bias, knew, vnew, z, n_b, n_t):
    n_pages = page_table.shape[1]
    steps = n_pages // PAGES_PER_STEP
    n_keys_pad = (n_pages + 1) * PAGE_SIZE
    page_spec = lambda u: pl.BlockSpec(
        (1, 1, PAGE_SIZE * H_A, DH_A), lambda r, j, pt, u=u: (layer, pt[r, j * PAGES_PER_STEP + u], 0, 0))
    per_req = lambda s: pl.BlockSpec((1,) + s, lambda r, j, pt: (r, 0, 0))
    rows = H_A * n_t
    return pl.pallas_call(
        functools.partial(_dsa_sample_attend_kernel, n_t=n_t, steps=steps),
        grid_spec=pltpu.PrefetchScalarGridSpec(
            num_scalar_prefetch=1,
            grid=(n_b, steps),
            in_specs=[page_spec(u) for u in range(PAGES_PER_STEP)] * 2
            + [per_req((rows, W_A)), per_req((n_t, n_keys_pad))]
            + [pl.BlockSpec((1, 1, PAGE_SIZE * H_A, DH_A), lambda r, j, pt: (r, 0, 0, 0))] * 2
            + [pl.BlockSpec((n_t, W_A), lambda r, j, pt: (r, C_AG // W_A))],
            out_specs=per_req((n_t, W_A)),
            scratch_shapes=[pltpu.VMEM((rows, LANE), F32), pltpu.VMEM((rows, LANE), F32),
                            pltpu.VMEM((rows, DH_A), F32)]),
        out_shape=jax.ShapeDtypeStruct((n_b, n_t, W_A), F32),
        compiler_params=_cparams(("parallel", "arbitrary")),
        name="dsa_sample_attend",
    )(page_table, *([k_pool] * PAGES_PER_STEP), *([v_pool] * PAGES_PER_STEP), qbd, bias, knew, vnew, z)


def kernel(x_prompt, x_sample, mem_prompt, cache_k, cache_v, cache_idx_k, state_hgrn, cache_mem_k,
           cache_mem_v, page_table, w_in, lb_logits, hgrn_norm_g, w_mem_k, w_mem_v, w_out, ln_g, ln_b):
    n_bp, t_p, _ = x_prompt.shape
    n_bs, t_s, _ = x_sample.shape
    n_pages = page_table.shape[1]
    past = n_pages * PAGE_SIZE
    l = 0

    lb_all = jnp.cumsum(jax.nn.softmax(lb_logits.astype(F32), axis=0), axis=0)
    lb = lb_all[l][None, :]
    g_norm = hgrn_norm_g[l][None, :]
    w_t = _cast_rows(jnp.swapaxes(w_in, 1, 2), l, tr=1024)
    w_o = w_out[l].astype(BF16)
    lg, lbias = ln_g[l][None, :], ln_b[l][None, :]

    xp = x_prompt.reshape(n_bp * t_p, D_MODEL)
    z, zs = _proj(xp, w_t, tm=1024, tn=1024)
    pos_p = jnp.arange(t_p, dtype=jnp.int32)
    tabs = _rope_tables(pos_p, DH_A) + _rope_tables(pos_p, D_IDX)
    q, kf, kb, vf, vt, iqw, sgn_t, ikf, ike, iko = _prep(z, zs, tabs, tm=256)
    a_out = _dsa_prompt(q, kb, vt, iqw, sgn_t, ike, iko, z, n_bp, t_p, tq=256)
    s0 = jnp.zeros((1, n_bp, H_B, DK_B, DV_B), F32)
    bn, s_p = _hgrn(z, lb, g_norm, s0, 0, n_bp, t_p, chunk=128, heads=H_B, mm_dtype=BF16, out_dtype=BF16)
    mk_p, mv_p = _mem_kv(mem_prompt, w_mem_k[l].astype(BF16), w_mem_v[l].astype(BF16))
    m_out = _mem_attend(z, mk_p, mv_p, 0, n_bp, t_p, tq=512, mm_dtype=BF16, out_dtype=BF16)
    y_p = _merge(a_out, bn, m_out, xp, w_o, lg, lbias, tm=256)

    n_s = n_bs * t_s
    xs = x_sample.reshape(n_s, D_MODEL)
    z2, zs2 = _proj(xs, w_t, tm=n_s, tn=1024)
    pos_s = past + jnp.arange(t_s, dtype=jnp.int32)
    tabs_s = [jnp.tile(t, (n_bs, 1)) for t in _rope_tables(pos_s, DH_A) + _rope_tables(pos_s, D_IDX)]
    q2, kf2, kb2, vf2, vt2, iqw2, sgn_t2, ikf2, ike2, iko2 = _prep(z2, zs2, tabs_s, tm=n_s)
    nsel_s = min(TOPK_MAX, (past + t_s) // 4)
    lq = iqw2.reshape(n_bs, t_s, H_IDX, D_IDX).transpose(0, 2, 1, 3).reshape(n_bs, H_IDX * t_s, D_IDX)
    sgb = jnp.broadcast_to(
        sgn_t2.reshape(H_IDX, n_bs, t_s).transpose(1, 0, 2).reshape(n_bs, H_IDX * t_s, 1),
        (n_bs, H_IDX * t_s, PAGE_SIZE))
    pad_keys = lambda a: jnp.pad(a.reshape((n_bs, t_s) + a.shape[1:]),
                                 ((0, 0), (0, PAGE_SIZE - t_s)) + ((0, 0),) * (a.ndim - 1))
    idx_pool_t = jnp.swapaxes(cache_idx_k, 2, 3)
    bias = _dsa_sample_select(page_table, idx_pool_t, l, lq, sgb, jnp.swapaxes(pad_keys(ikf2), 1, 2),
                              n_bs, t_s, nsel_s).reshape(n_bs, t_s, -1)
    q4 = q2.reshape(n_bs, t_s, H_A, DH_A)
    eye = jnp.eye(H_A, dtype=q2.dtype)
    qbd = (q4[:, None, :, :, :] * eye[None, :, None, :, None]).reshape(n_bs, H_A * t_s, W_A)
    n_layers, n_phys = cache_k.shape[:2]
    as_rows = lambda a: a.reshape(a.shape[0], a.shape[1], PAGE_SIZE * H_A, DH_A)
    a_out2 = _dsa_sample_attend(page_table, as_rows(cache_k), as_rows(cache_v), l, qbd, bias,
                                as_rows(pad_keys(kf2)[:, None]), as_rows(pad_keys(vf2)[:, None]), z2, n_bs, t_s)
    bn2, s_s = _hgrn(z2, lb, g_norm, state_hgrn, l, n_bs, t_s, chunk=t_s, heads=H_B, mm_dtype=F32, out_dtype=F32)
    m_out2 = _mem_attend(z2, cache_mem_k, cache_mem_v, l, n_bs, t_s, tq=t_s, mm_dtype=F32, out_dtype=F32)
    y_s = _merge(a_out2.reshape(n_s, W_A), bn2, m_out2, xs, w_o, lg, lbias, tm=n_s)

    return (y_p.reshape(n_bp, t_p, D_MODEL), y_s.reshape(n_bs, t_s, D_MODEL),
            kf.reshape(1, n_bp, t_p, H_A, DH_A), vf.reshape(1, n_bp, t_p, H_A, DH_A),
            ikf.reshape(1, n_bp, t_p, D_IDX), s_p, mk_p, mv_p,
            kf2.reshape(1, n_bs, t_s, H_A, DH_A), vf2.reshape(1, n_bs, t_s, H_A, DH_A),
            ikf2.reshape(1, n_bs, t_s, D_IDX), s_s)
```

```python
import functools

import numpy as np
import jax
import jax.numpy as jnp
from jax import lax
from jax.experimental import pallas as pl
from jax.experimental.pallas import tpu as pltpu

D_MODEL = 2048
DH_A = 128
H_A = 8
W_A = H_A * DH_A
H_IDX = 16
D_IDX = 64
W_IDX = H_IDX * D_IDX
TOPK_MAX = 256
DK_B = 128
DV_B = 128
H_B = 4
W_B = H_B * DV_B
DH_M = 128
H_M = 4
W_M = H_M * DH_M
N_MEM = 256
PAGE_SIZE = 128
ROPE_THETA = 500000.0
ROPE_FRAC = 4
LN_EPS = 1e-5
RMS_EPS = 1e-6
DEPTH = 1
ALPHA = (2.0 * DEPTH) ** 0.25

LANE = 128
VMEM_LIMIT = 48 * 1024 * 1024

C_AQ, C_AK, C_AV, C_AG, C_IQ = 0, 1024, 2048, 3072, 4096
C_BQ, C_BF, C_BI, C_BG, C_MQ, C_MG = 5120, 5632, 6144, 6656, 7168, 7680
N_MAIN = 8192

NEG = -1e30
Q_SCALE = DH_A ** -0.5 * 1.4426950408889634
ATT_GROUP = H_A
VT_ROWS = DH_A + 16
INT_MIN = -(2 ** 31)
BF16 = jnp.bfloat16
F32 = jnp.float32


def _dot(a, b, **kw):
    return jnp.dot(a, b, preferred_element_type=F32, **kw)


def _dot_nt(a, b):
    return lax.dot_general(a, b, (((1,), (1,)), ((), ())), preferred_element_type=F32)


def _dot_tn(a, b):
    return lax.dot_general(a, b, (((0,), (0,)), ((), ())), preferred_element_type=F32)


def _sigmoid(x):
    return 1.0 / (1.0 + jnp.exp(-x))


def _silu(x):
    return x * _sigmoid(x)


def _cparams(sem):
    return pltpu.CompilerParams(dimension_semantics=sem, vmem_limit_bytes=VMEM_LIMIT)


def _proj_kernel(x_ref, w_ref, ws_ref, o_ref, os_ref, xb_ref):
    j = pl.program_id(1)

    @pl.when(j == 0)
    def _():
        xb = x_ref[...].astype(BF16)
        xb_ref[...] = xb
        os_ref[...] = _dot_nt(xb, ws_ref[...])

    o_ref[...] = _dot_nt(xb_ref[...], w_ref[...])


W_SIDE_ROW = C_IQ + W_IDX
W_SIDE_ROWS = D_IDX + H_IDX
N_IN = N_MAIN + W_SIDE_ROWS


def _proj(x, w_t, tm, tn):
    m = x.shape[0]
    n_lo = W_SIDE_ROW // tn

    def w_rows(i, j):
        return ((j * (tn // 16) + jnp.where(j < n_lo, 0, W_SIDE_ROWS // 16)) * 16, 0)

    return pl.pallas_call(
        _proj_kernel,
        grid=(m // tm, N_MAIN // tn),
        in_specs=[pl.BlockSpec((tm, D_MODEL), lambda i, j: (i, 0)),
                  pl.BlockSpec((pl.Element(tn), pl.Element(D_MODEL)), w_rows),
                  pl.BlockSpec((pl.Element(LANE), pl.Element(D_MODEL)), lambda i, j: (W_SIDE_ROW, 0))],
        out_specs=[pl.BlockSpec((tm, tn), lambda i, j: (i, j)),
                   pl.BlockSpec((tm, LANE), lambda i, j: (i, 0))],
        out_shape=[jax.ShapeDtypeStruct((m, N_MAIN), F32),
                   jax.ShapeDtypeStruct((m, LANE), F32)],
        scratch_shapes=[pltpu.VMEM((tm, D_MODEL), BF16)],
        compiler_params=_cparams(("parallel", "arbitrary")),
        name="in_proj",
    )(x, w_t, w_t)


def _cast_kernel(x_ref, o_ref):
    o_ref[...] = x_ref[0].astype(BF16)


def _cast_rows(w, layer, tr):
    rows = w.shape[1]
    return pl.pallas_call(
        _cast_kernel,
        grid=(pl.cdiv(rows, tr),),
        in_specs=[pl.BlockSpec((1, tr, D_MODEL), lambda i: (layer, i, 0))],
        out_specs=pl.BlockSpec((tr, D_MODEL), lambda i: (i, 0)),
        out_shape=jax.ShapeDtypeStruct((rows, D_MODEL), BF16),
        compiler_params=_cparams(("parallel",)),
        name="w_cast",
    )(w)


def _swap_halves_index(shape, d):
    half = d // ROPE_FRAC // 2
    lane = lax.broadcasted_iota(jnp.int32, shape, 1)
    pos = lane & (d - 1)
    return jnp.where(pos < half, lane + half, jnp.where(pos < 2 * half, lane - half, lane))


def _rope(x, c, s, swap):
    return x * c + jnp.take_along_axis(x, swap, axis=1) * s


def _prep_kernel(aq_ref, ak_ref, av_ref, iq_ref, zs_ref, c1_ref, s1_ref, c2_ref, s2_ref,
                 q_ref, kf_ref, kb_ref, vf_ref, vt_ref, iqw_ref, sgn_ref,
                 ikf_ref, ike_ref, iko_ref):
    c1, s1, c2, s2 = c1_ref[...], s1_ref[...], c2_ref[...], s2_ref[...]
    swap_a = _swap_halves_index(c1.shape, DH_A)
    swap_i = _swap_halves_index(c2.shape, D_IDX)
    k_heads, v_heads = [], []
    for h in range(H_A):
        sl = slice(h * DH_A, (h + 1) * DH_A)
        q_ref[:, sl] = (_rope(aq_ref[:, sl], c1, s1, swap_a) * Q_SCALE).astype(BF16)
        kr = _rope(ak_ref[:, sl], c1, s1, swap_a)
        kb_ref[:, sl] = kr.astype(BF16)
        v = av_ref[:, sl]
        vt_ref[h * VT_ROWS:h * VT_ROWS + DH_A, :] = v.T.astype(BF16)
        vt_ref[h * VT_ROWS + DH_A:(h + 1) * VT_ROWS, :] = jnp.ones((VT_ROWS - DH_A, v.shape[0]), BF16)
        k_heads.append(kr)
        v_heads.append(v)
    kf_ref[...] = jnp.swapaxes(jnp.stack(k_heads, axis=0), 0, 1)
    vf_ref[...] = jnp.swapaxes(jnp.stack(v_heads, axis=0), 0, 1)

    zs = zs_ref[...]
    lane = lax.broadcasted_iota(jnp.int32, zs.shape, 1)
    ikr = _rope(zs, c2, s2, swap_i)
    ikf_ref[...] = ikr[:, :D_IDX]
    ike_ref[...] = jnp.where(lane < D_IDX, ikr, 0.0).astype(BF16)
    iko_ref[...] = jnp.where(lane >= D_IDX, pltpu.roll(ikr, D_IDX, 1), 0.0).astype(BF16)
    sgn_t = jnp.sign(pltpu.roll(zs, LANE - D_IDX, 1)).T
    sgn_ref[...] = sgn_t[:H_IDX, :]
    wscale = (D_IDX ** -0.5) * (H_IDX ** -0.5)
    for p in range(H_IDX // 2):
        sl = slice(p * LANE, (p + 1) * LANE)
        wa = jnp.abs(zs[:, D_IDX + 2 * p:D_IDX + 2 * p + 1])
        wb = jnp.abs(zs[:, D_IDX + 2 * p + 1:D_IDX + 2 * p + 2])
        wpair = jnp.where(lane < D_IDX, wa, wb) * wscale
        iqw_ref[:, sl] = (_rope(iq_ref[:, sl], c2, s2, swap_i) * wpair).astype(BF16)


def _prep(z, zs, tabs, tm):
    m = z.shape[0]
    period = tabs[0].shape[0] // tm
    zblk = lambda c: pl.BlockSpec((tm, 1024), lambda i, c=c: (i, c // 1024))
    tab = pl.BlockSpec((tm, LANE), lambda i: (i % period, 0))
    row = lambda w: pl.BlockSpec((tm, w), lambda i: (i, 0))
    col = lambda w: pl.BlockSpec((w, tm), lambda i: (0, i))
    sds = lambda w, dt: jax.ShapeDtypeStruct((m, w), dt)
    heads =pl.BlockSpec((tm, H_A, DH_A), lambda i: (i, 0, 0))
    heads_sds = jax.ShapeDtypeStruct((m, H_A, DH_A), F32)
    return pl.pallas_call(
        _prep_kernel,
        grid=(m // tm,),
        in_specs=[zblk(C_AQ), zblk(C_AK), zblk(C_AV), zblk(C_IQ), row(LANE)] + [tab] * 4,
        out_specs=[row(W_A), heads, row(W_A), heads, col(H_A * VT_ROWS), row(W_IDX), col(H_IDX),
                   row(D_IDX), row(LANE), row(LANE)],
        out_shape=[sds(W_A, BF16), heads_sds, sds(W_A, BF16), heads_sds,
                   jax.ShapeDtypeStruct((H_A * VT_ROWS, m), BF16), sds(W_IDX, BF16),
                   jax.ShapeDtypeStruct((H_IDX, m), F32),
                   sds(D_IDX, F32), sds(LANE, BF16), sds(LANE, BF16)],
        compiler_params=_cparams(("parallel",)),
        name="rope_prep",
    )(z, z, z, z, zs, *tabs)


def _rope_tables(pos, d):
    r = d // ROPE_FRAC
    half = r // 2
    n = pos.shape[0]
    inv = ROPE_THETA ** (-jnp.arange(half, dtype=F32) / half)
    ang = pos.astype(F32)[:, None] * inv[None, :]
    cos, sin = jnp.cos(ang), jnp.sin(ang)
    c = jnp.concatenate([cos, cos, jnp.ones((n, d - r), F32)], axis=1)
    s = jnp.concatenate([-sin, sin, jnp.zeros((n, d - r), F32)], axis=1)
    reps = LANE // d
    return [jnp.tile(t, (1, reps)) for t in (c, s)]


KEY_NEG_INF = -2139095041
F32_LOWEST = -3.4028234663852886e38


def _key_to_float(key):
    key = jnp.maximum(key, jnp.int32(KEY_NEG_INF))
    b = key ^ (lax.shift_right_arithmetic(key, 31) & jnp.int32(0x7FFFFFFF))
    return lax.bitcast_convert_type(b, F32)


def _lane_total(part):
    return _dot(part.astype(BF16), jnp.ones((LANE, LANE), BF16))


def _radix_select(count_ge, shape, total, nsel):
    def body(bi, carry):
        pu, cacc = carry
        bit = jnp.int32(31) - bi
        cand_u = pu | lax.shift_left(jnp.int32(1), bit)
        cnt = count_ge(_key_to_float(cand_u ^ jnp.int32(INT_MIN)))
        ok = cnt >= nsel
        return jnp.where(ok, cand_u, pu), jnp.where(ok, cnt, cacc)

    pu0 = jnp.zeros(shape, jnp.int32)
    c0 = jnp.full(shape, total, F32)
    pu, cacc = lax.fori_loop(0, 32, body, (pu0, c0))
    return _key_to_float(pu ^ jnp.int32(INT_MIN)), cacc


def _dsa_prompt_kernel(q_ref, k_ref, vt_ref, iq_ref, sgn_ref, ike_ref, iko_ref, o_ref,
                       key_ref, bias_ref, m_ref, acc_ref, *, tq, nsel):
    tk = tq
    i = pl.program_id(1)
    nch = i + 1
    sub = 8

    kidx0 = lax.broadcasted_iota(jnp.int32, (tk, tq), 0)
    qidx = i * tq + lax.broadcasted_iota(jnp.int32, (tk, tq), 1)

    def chunk_slice(c):
        return pl.ds(pl.multiple_of(c * tk, tk), tk)

    def score_body(c, carry):
        ds = chunk_slice(c)
        ke = ike_ref[ds, :]
        ko = iko_ref[ds, :]
        acc = jnp.zeros((tk, tq), F32)
        for h in range(H_IDX):
            rhs = iq_ref[:, (h // 2) * LANE:(h // 2 + 1) * LANE]
            d = _dot_nt(ke if h % 2 == 0 else ko, rhs)
            acc = acc + sgn_ref[h:h + 1, :] * jnp.maximum(d, 0.0)
        key_ref[ds, :] = jnp.where(kidx0 + c * tk <= qidx, acc, -jnp.inf)
        return carry

    lax.fori_loop(0, nch, score_body, 0)

    def count(cmp_fn):
        def hits(c):
            hit = jnp.where(cmp_fn(key_ref[chunk_slice(c), :]), 1.0, 0.0)
            return jnp.sum(hit.reshape(tk // sub, sub, tq), axis=0)

        def body(c2, parts):
            return parts[0] + hits(2 * c2), parts[1] + hits(2 * c2 + 1)

        zero = jnp.zeros((sub, tq), F32)
        p0, p1 = lax.fori_loop(0, nch // 2, body, (zero, zero))
        p0 = lax.cond(nch % 2 == 1, lambda p: p + hits(nch - 1), lambda p: p, p0)
        return jnp.sum(p0 + p1, axis=0, keepdims=True)

    total = (nch * tk).astype(F32)
    tau, c_ge = _radix_select(lambda cand: count(lambda kc: kc >= cand), (1, tq), total, float(nsel))
    tau_eff = jnp.maximum(tau, F32_LOWEST)

    def bias_body(c, carry):
        ds = chunk_slice(c)
        bias_ref[ds, :] = jnp.where(key_ref[ds, :] >= tau_eff, 0.0, NEG)
        return carry

    lax.fori_loop(0, nch, bias_body, 0)

    tie_q = jnp.logical_and(c_ge > float(nsel), tau > -jnp.inf)
    has_tie = jnp.max(jnp.where(tie_q, 1.0, 0.0)) > 0.5

    @pl.when(has_tie)
    def _():
        need = float(nsel) - count(lambda kc: kc > tau)
        lower = jnp.where(lax.broadcasted_iota(jnp.int32, (tk, tk), 1)
                          <= lax.broadcasted_iota(jnp.int32, (tk, tk), 0), 1.0, 0.0).astype(BF16)
        real = tau > -jnp.inf

        def tie_body(c, run):
            ds = chunk_slice(c)
            kc = key_ref[ds, :]
            eqf = jnp.where(kc == tau, 1.0, 0.0)
            pre = _dot(lower, eqf.astype(BF16)) + run
            keep = jnp.where(kc > tau, 1.0, jnp.where(real, eqf * jnp.where(pre <= need, 1.0, 0.0), 0.0))
            bias_ref[ds, :] = jnp.where(keep > 0.5, 0.0, NEG)
            return run + jnp.sum(eqf, axis=0, keepdims=True)

        lax.fori_loop(0, nch, tie_body, jnp.zeros((1, tq), F32))

    m_ref[...] = jnp.full(m_ref.shape, NEG, F32)
    acc_ref[...] = jnp.zeros(acc_ref.shape, F32)

    def attn_body(c, carry):
        ds = chunk_slice(c)
        bias = bias_ref[ds, :]
        for h0 in range(0, H_A, ATT_GROUP):
            group = range(h0, h0 + ATT_GROUP)
            ss = {h: _dot_nt(k_ref[ds, h * DH_A:(h + 1) * DH_A], q_ref[:, h * DH_A:(h + 1) * DH_A])
                  for h in group}
            ps, alphas = {}, {}
            for h in group:
                s = ss[h] + bias
                m_prev = m_ref[h]
                m_new = jnp.maximum(m_prev, jnp.max(s, axis=0, keepdims=True))
                alphas[h] = jnp.exp2(m_prev - m_new)
                ps[h] = jnp.exp2(s - m_new).astype(BF16)
                m_ref[h] = m_new
            for h in group:
                acc_ref[h] = alphas[h] * acc_ref[h] + _dot(vt_ref[h * VT_ROWS:(h + 1) * VT_ROWS, ds], ps[h])
        return carry

    lax.fori_loop(0, nch, attn_body, 0)
    for h in range(H_A):
        acc = acc_ref[h]
        o_ref[:, h * DH_A:(h + 1) * DH_A] = (acc[:DH_A] / acc[DH_A:DH_A + 1]).T


def _dsa_prompt(q, k, vt, iqw, sgn_t, ike, iko, n_b, n_t, tq):
    nsel = min(TOPK_MAX, n_t // 4)
    nq = n_t // tq
    qblk = lambda w: pl.BlockSpec((tq, w), lambda b, i: (b * nq + i, 0))
    full = lambda w: pl.BlockSpec((n_t, w), lambda b, i: (b, 0))
    return pl.pallas_call(
        functools.partial(_dsa_prompt_kernel, tq=tq, nsel=nsel),
        grid=(n_b, nq),
        in_specs=[qblk(W_A), full(W_A), pl.BlockSpec((H_A * VT_ROWS, n_t), lambda b, i: (0, b)), qblk(W_IDX),
                  pl.BlockSpec((H_IDX, tq), lambda b, i: (0, b * nq + i)), full(LANE), full(LANE)],
        out_specs=qblk(W_A),
        out_shape=jax.ShapeDtypeStruct((n_b * n_t, W_A), F32),
        scratch_shapes=[pltpu.VMEM((n_t, tq), F32),
                        pltpu.VMEM((n_t, tq), F32),
                        pltpu.VMEM((H_A, 1, tq), F32),
                        pltpu.VMEM((H_A, VT_ROWS, tq), F32)],
        compiler_params=_cparams(("parallel", "arbitrary")),
        name="dsa_prompt",
    )(q, k, vt, iqw, sgn_t, ike, iko)


HGRN_BLOCK = 8


def _hgrn_levels(chunk):
    levels, m = [], HGRN_BLOCK
    while m < chunk:
        levels.append(m)
        m *= 2
    return levels


def _hgrn_masks(chunk):
    row, col = np.indices((chunk, chunk))
    masks = []
    for m in _hgrn_levels(chunk):
        same_pair = (row // (2 * m)) == (col // (2 * m))
        masks.append(same_pair & (row % (2 * m) >= m) & (col % (2 * m) < m))
    for j in range(HGRN_BLOCK):
        masks.append((row - col == j) & (row % HGRN_BLOCK >= j))
    return np.stack(masks).astype(np.float32)


def _hgrn_kernel(bq_ref, bf_ref, bi_ref, lb_ref, g_ref, s0_ref, mask_ref, o_ref, s_out_ref, st_ref,
                 *, n_t, chunk, heads, mm_dtype):
    c_ = chunk
    g = g_ref[...]
    for hh in range(heads):
        st_ref[hh] = s0_ref[0, 0, hh].T
    row = lax.broadcasted_iota(jnp.int32, (c_, c_), 0)
    col = lax.broadcasted_iota(jnp.int32, (c_, c_), 1)
    lower = jnp.where(col <= row, 1.0, 0.0)
    levels = _hgrn_levels(c_)

    def head_chunk(hh, ds):
        hs = slice(hh * DK_B, (hh + 1) * DK_B)
        lb = lb_ref[:, hs]
        bq = bq_ref[ds, hs]
        qs = bq * _sigmoid(bq)
        f = lb + (1.0 - lb) * _sigmoid(bf_ref[ds, hs])
        kk = 1.0 - f
        v = bi_ref[ds, hs]
        gcum = _dot(lower, jnp.log2(f), precision=lax.Precision.HIGHEST)
        a = jnp.zeros((c_, c_), F32)
        for li, m in enumerate(levels):
            g3 = gcum.reshape(c_ // (2 * m), 2 * m, DK_B)
            ref = jnp.broadcast_to(g3[:, m - 1:m, :], g3.shape).reshape(c_, DK_B)
            qh = qs * jnp.exp2(jnp.minimum(gcum - ref, 0.0))
            kh = kk * jnp.exp2(jnp.minimum(ref - gcum, 0.0))
            a = a + _dot_nt(qh.astype(mm_dtype), kh.astype(mm_dtype)) * mask_ref[li]
        for j in range(HGRN_BLOCK):
            kj = kk if j == 0 else pltpu.roll(kk, j, 0)
            gj = gcum if j == 0 else pltpu.roll(gcum, j, 0)
            e = jnp.exp2(jnp.minimum(gcum - gj, 0.0))
            aj = jnp.sum(qs * kj * e, axis=1, keepdims=True)
            a = a + aj * mask_ref[len(levels) + j]
        st = st_ref[hh]
        o = (_dot_nt((qs * jnp.exp2(gcum)).astype(mm_dtype), st.astype(mm_dtype))
             + _dot(a.astype(mm_dtype), v.astype(mm_dtype)))
        glast = gcum[c_ - 1:c_, :]
        kd = kk * jnp.exp2(glast - gcum)
        st_ref[hh] = jnp.exp2(glast) * st + _dot_tn(v.astype(mm_dtype), kd.astype(mm_dtype))
        o_ref[ds, hs] = o * lax.rsqrt(jnp.mean(o * o, axis=1, keepdims=True) + RMS_EPS) * g

    def chunk_body(ci, carry):
        ds = pl.ds(pl.multiple_of(ci * c_, c_), c_)
        for hh in range(heads):
            head_chunk(hh, ds)
        return carry

    lax.fori_loop(0, n_t // c_, chunk_body, 0)
    for hh in range(heads):
        s_out_ref[0, 0, hh] = st_ref[hh].T


def _hgrn(z, lb, g, s0, layer, n_b, n_t, chunk, heads, mm_dtype):
    w = heads * DK_B
    zcol = lambda c: pl.BlockSpec((n_t, w), lambda b, h, c=c: (b, c // w + h))
    sblk = lambda lyr: pl.BlockSpec((1, 1, heads, DK_B, DV_B), lambda b, h: (lyr, b, h, 0, 0))
    masks = jnp.asarray(_hgrn_masks(chunk))
    return pl.pallas_call(
        functools.partial(_hgrn_kernel, n_t=n_t, chunk=chunk, heads=heads, mm_dtype=mm_dtype),
        grid=(n_b, H_B // heads),
        in_specs=[zcol(C_BQ), zcol(C_BF), zcol(C_BI),
                  pl.BlockSpec((1, w), lambda b, h: (0, h)),
                  pl.BlockSpec((1, DV_B), lambda b, h: (0, 0)),
                  sblk(layer),
                  pl.BlockSpec(masks.shape, lambda b, h: (0, 0, 0))],
        out_specs=[pl.BlockSpec((n_t, w), lambda b, h: (b, h)), sblk(0)],
        out_shape=[jax.ShapeDtypeStruct((n_b * n_t, W_B), F32),
                   jax.ShapeDtypeStruct((1, n_b, H_B, DK_B, DV_B), F32)],
        scratch_shapes=[pltpu.VMEM((heads, DV_B, DK_B), F32)],
        compiler_params=_cparams(("parallel", "arbitrary")),
        name="hgrn2",
    )(z, z, z, lb, g, s0, masks)


def _mem_kv_kernel(x_ref, wk_ref, wv_ref, mk_ref, mv_ref):
    x = x_ref[0].astype(BF16)
    mk = _dot(x, wk_ref[...])
    mv = _dot(x, wv_ref[...])
    heads = lambda a: jnp.stack([a[:, h * DH_M:(h + 1) * DH_M] for h in range(H_M)], axis=0)
    mk_ref[0, 0] = jnp.swapaxes(heads(mk), 0, 1)
    mv_ref[0, 0] = jnp.swapaxes(heads(mv), 0, 1)


def _mem_kv(mem, wk, wv):
    n_b = mem.shape[0]
    wspec = pl.BlockSpec((D_MODEL, W_M), lambda b: (0, 0))
    ospec = pl.BlockSpec((1, 1, N_MEM, H_M, DH_M), lambda b: (0, b, 0, 0, 0))
    sds = jax.ShapeDtypeStruct((1, n_b, N_MEM, H_M, DH_M), F32)
    return pl.pallas_call(
        _mem_kv_kernel,
        grid=(n_b,),
        in_specs=[pl.BlockSpec((1, N_MEM, D_MODEL), lambda b: (b, 0, 0)), wspec, wspec],
        out_specs=[ospec, ospec],
        out_shape=[sds, sds],
        compiler_params=_cparams(("parallel",)),
        name="mem_kv",
    )(mem, wk, wv)


def _mem_kernel(q_ref, mk_ref, mv_ref, o_ref, *, mm_dtype):
    mk = jnp.swapaxes(mk_ref[0, 0], 0, 1).astype(mm_dtype)
    mv = jnp.swapaxes(mv_ref[0, 0], 0, 1).astype(mm_dtype)
    for h in range(H_M):
        hs = slice(h * DH_M, (h + 1) * DH_M)
        q = (q_ref[:, hs] * (DH_M ** -0.5)).astype(mm_dtype)
        s = _dot_nt(q, mk[h])
        p = jnp.exp(s - jnp.max(s, axis=1, keepdims=True))
        l = jnp.sum(p, axis=1, keepdims=True)
        o_ref[:, hs] = _dot(p.astype(mm_dtype), mv[h]) / l


def _mem_attend(z, mk, mv, layer, n_b, n_t, tq, mm_dtype):
    nq = n_t // tq
    kv = pl.BlockSpec((1, 1, N_MEM, H_M, DH_M), lambda b, i: (layer, b, 0, 0, 0))
    return pl.pallas_call(
        functools.partial(_mem_kernel, mm_dtype=mm_dtype),
        grid=(n_b, nq),
        in_specs=[pl.BlockSpec((tq, W_M), lambda b, i: (b * nq + i, C_MQ // W_M)), kv, kv],
        out_specs=pl.BlockSpec((tq, W_M), lambda b, i: (b * nq + i, 0)),
        out_shape=jax.ShapeDtypeStruct((n_b * n_t, W_M), F32),
        compiler_params=_cparams(("parallel", "parallel")),
        name="mem_attend",
    )(z, mk, mv)


def _merge_kernel(a_ref, ag_ref, bn_ref, bg_ref, mo_ref, mg_ref, h_ref, wo_ref, lg_ref, lbias_ref, o_ref):
    a = (a_ref[...] * _silu(ag_ref[...])).astype(BF16)
    b = (bn_ref[...] * _silu(bg_ref[...])).astype(BF16)
    m = (mo_ref[...] * _silu(mg_ref[...])).astype(BF16)
    y = (_dot(a, wo_ref[0:W_A, :]) + _dot(b, wo_ref[W_A:W_A + W_B, :])
         + _dot(m, wo_ref[W_A + W_B:D_MODEL, :]))
    r = ALPHA * h_ref[...] + y
    xc = r - jnp.mean(r, axis=1, keepdims=True)
    var = jnp.mean(xc * xc, axis=1, keepdims=True)
    o_ref[...] = xc * lax.rsqrt(var + LN_EPS) * lg_ref[...] + lbias_ref[...]


def _merge(a_out, z, bn, m_out, h, w_out, ln_g, ln_b, tm):
    m = h.shape[0]
    row = lambda w: pl.BlockSpec((tm, w), lambda i: (i, 0))
    zcol = lambda c, w: pl.BlockSpec((tm, w), lambda i, c=c, w=w: (i, c // w))
    const = lambda s: pl.BlockSpec(s, lambda i: (0, 0), pipeline_mode=pl.Buffered(1))
    return pl.pallas_call(
        _merge_kernel,
        grid=(m // tm,),
        in_specs=[row(W_A), zcol(C_AG, W_A), row(W_B), zcol(C_BG, W_B), row(W_M), zcol(C_MG, W_M),
                  row(D_MODEL), const((D_MODEL, D_MODEL)), const((1, D_MODEL)), const((1, D_MODEL))],
        out_specs=row(D_MODEL),
        out_shape=jax.ShapeDtypeStruct((m, D_MODEL), F32),
        compiler_params=_cparams(("parallel",)),
        name="merge",
    )(a_out, z, bn, z, m_out, z, h, w_out, ln_g, ln_b)


PAGES_PER_STEP = 8


def _dsa_sample_select_kernel(pt_ref, pool_ref, lq_ref, sg_ref, iknew_ref, bias_ref, key_ref, page_buf, sem,
                              *, layer, n_t, n_pages, nsel):
    r = pl.program_id(0)
    slot = r % 2
    n_keys_pad = (n_pages + 1) * PAGE_SIZE

    def page_copy(req, p, s):
        return pltpu.make_async_copy(pool_ref.at[layer, pt_ref[req, p]], page_buf.at[s, p], sem.at[s])

    def for_pages(fn):
        def body(p, carry):
            fn(p)
            return carry
        lax.fori_loop(0, n_pages, body, 0)

    @pl.when(r == 0)
    def _():
        for_pages(lambda p: page_copy(0, p, 0).start())

    @pl.when(r + 1 < pl.num_programs(0))
    def _():
        for_pages(lambda p: page_copy(r + 1, p, 1 - slot).start())

    for_pages(lambda p: page_copy(r, p, slot).wait())

    lq = lq_ref[0]
    sg = sg_ref[0]

    def score(ik_t):
        d = _dot(lq, ik_t.astype(BF16))
        r = jnp.maximum(d, 0.0) * jnp.concatenate([sg] * (ik_t.shape[1] // PAGE_SIZE), axis=1)
        s = r[0:n_t]
        for h in range(1, H_IDX):
            s = s + r[h * n_t:(h + 1) * n_t]
        return s

    width = PAGES_PER_STEP * PAGE_SIZE

    rows = pl.ds(pl.multiple_of(r * n_t, n_t), n_t)

    def group_body(gi, carry):
        base = gi * PAGES_PER_STEP
        ik_step = jnp.concatenate([page_buf[slot, base + u] for u in range(PAGES_PER_STEP)], axis=1)
        key_ref[rows, pl.ds(pl.multiple_of(gi * width, width), width)] = score(ik_step)
        return carry

    lax.fori_loop(0, n_pages // PAGES_PER_STEP, group_body, 0)

    past = n_pages * PAGE_SIZE
    s_new = score(iknew_ref[0])
    t_i = lax.broadcasted_iota(jnp.int32, (n_t, PAGE_SIZE), 0)
    k_i = lax.broadcasted_iota(jnp.int32, (n_t, PAGE_SIZE), 1)
    key_ref[rows, past:n_keys_pad] = jnp.where(k_i <= t_i, s_new, -jnp.inf)

    @pl.when(r == pl.num_programs(0) - 1)
    def _():
        n_rows = key_ref.shape[0]
        n_tiles = n_keys_pad // LANE

        def count_ge(cand):
            part = jnp.zeros((n_rows, LANE), F32)
            for c in range(n_tiles):
                part = part + jnp.where(key_ref[:, c * LANE:(c + 1) * LANE] >= cand, 1.0, 0.0)
            return _lane_total(part)

        tau_r, c_ge_r = _radix_select(count_ge, (n_rows, LANE), float(n_keys_pad), float(nsel))
        tau = tau_r[:, 0:1]
        tau_eff = jnp.maximum(tau, F32_LOWEST)
        bias_ref[...] = jnp.where(key_ref[...] >= tau_eff, 0.0, NEG)

        tie_rows = jnp.logical_and(c_ge_r > float(nsel), tau_r > -jnp.inf)
        has_tie = jnp.max(jnp.where(tie_rows, 1.0, 0.0)) > 0.5

        @pl.when(has_tie)
        def _():
            c_gt = jnp.sum(jnp.where(key_ref[...] > tau, 1.0, 0.0), axis=1, keepdims=True)
            need = float(nsel) - c_gt
            ri = lax.broadcasted_iota(jnp.int32, (PAGE_SIZE, PAGE_SIZE), 0)
            ci = lax.broadcasted_iota(jnp.int32, (PAGE_SIZE, PAGE_SIZE), 1)
            upper = jnp.where(ri <= ci, 1.0, 0.0)
            real = tau > -jnp.inf

            def tie_body(c, run):
                ds = pl.ds(pl.multiple_of(c * PAGE_SIZE, PAGE_SIZE), PAGE_SIZE)
                kc = key_ref[:, ds]
                eqf = jnp.where(kc == tau, 1.0, 0.0)
                pre = _dot(eqf, upper) + run
                keep = jnp.where(kc > tau, 1.0, jnp.where(real, eqf * jnp.where(pre <= need, 1.0, 0.0), 0.0))
                bias_ref[:, ds] = jnp.where(keep > 0.5, 0.0, NEG)
                return run + jnp.sum(eqf, axis=1, keepdims=True)

            lax.fori_loop(0, n_keys_pad // PAGE_SIZE, tie_body, jnp.zeros((n_rows, 1), F32))


def _dsa_sample_select(page_table, idx_pool, layer, lq, sgb, iknew, n_b, n_t, nsel):
    n_pages = page_table.shape[1]
    n_keys_pad = (n_pages + 1) * PAGE_SIZE
    per_req = lambda s: pl.BlockSpec((1,) + s, lambda r, pt: (r, 0, 0))
    return pl.pallas_call(
        functools.partial(_dsa_sample_select_kernel, layer=layer, n_t=n_t, n_pages=n_pages, nsel=nsel),
        grid_spec=pltpu.PrefetchScalarGridSpec(
            num_scalar_prefetch=1,
            grid=(n_b,),
            in_specs=[pl.BlockSpec(memory_space=pl.ANY),
                      per_req((H_IDX * n_t, D_IDX)), per_req((H_IDX * n_t, PAGE_SIZE)), per_req((D_IDX, PAGE_SIZE))],
            out_specs=pl.BlockSpec((n_b * n_t, n_keys_pad), lambda r, pt: (0, 0)),
            scratch_shapes=[pltpu.VMEM((n_b * n_t, n_keys_pad), F32),
                            pltpu.VMEM((2, n_pages, D_IDX, PAGE_SIZE), F32),
                            pltpu.SemaphoreType.DMA((2,))]),
        out_shape=jax.ShapeDtypeStruct((n_b * n_t, n_keys_pad), F32),
        compiler_params=_cparams(("arbitrary",)),
        name="dsa_sample_select",
    )(page_table, idx_pool, lq, sgb, iknew)


ATT_PAGES = 16


def _dsa_sample_attend_kernel(pt_ref, *refs, n_t, steps):
    kp = refs[:ATT_PAGES]
    vp = refs[ATT_PAGES:2 * ATT_PAGES]
    qbd_ref, bias_ref, knew_ref, vnew_ref, o_ref, m_ref, l_ref, acc_ref = refs[2 * ATT_PAGES:]
    j = pl.program_id(1)
    rows = H_A * n_t

    @pl.when(j == 0)
    def _():
        m_ref[...] = jnp.full(m_ref.shape, NEG, F32)
        l_ref[...] = jnp.zeros(l_ref.shape, F32)
        acc_ref[...] = jnp.zeros(acc_ref.shape, F32)

    qbd = qbd_ref[0]

    def update(kblk, vblk, bias):
        n = kblk.shape[0]
        s = _dot_nt(qbd, kblk) + jnp.concatenate([bias] * H_A, axis=0)
        m_prev = m_ref[...]
        m_new = jnp.maximum(m_prev, jnp.max(s, axis=1, keepdims=True))
        alpha = jnp.exp2(m_prev - m_new)
        p = jnp.exp2(s - jnp.concatenate([m_new] * (n // LANE), axis=1))
        l_ref[...] = alpha * l_ref[...] + jnp.sum(p, axis=1, keepdims=True)
        pv = _dot(p.astype(BF16), vblk)
        diag = jnp.concatenate(
            [pv[h * n_t:(h + 1) * n_t, h * DH_A:(h + 1) * DH_A] for h in range(H_A)], axis=0)
        acc_ref[...] = alpha * acc_ref[...] + diag
        m_ref[...] = m_new

    def page2d(ref):
        return jnp.concatenate(
            [ref[0, 0, pl.ds(h, PAGE_SIZE, stride=H_A), :] for h in range(H_A)], axis=1).astype(BF16)

    kblk = jnp.concatenate([page2d(kp[u]) for u in range(ATT_PAGES)], axis=0)
    vblk = jnp.concatenate([page2d(vp[u]) for u in range(ATT_PAGES)], axis=0)
    width = ATT_PAGES * PAGE_SIZE
    off = pl.multiple_of(j * width, width)
    update(kblk, vblk, bias_ref[0, :, pl.ds(off, width)])

    @pl.when(j == steps - 1)
    def _():
        past = steps * width
        update(page2d(knew_ref), page2d(vnew_ref), bias_ref[0, :, past:past + PAGE_SIZE])
        o = acc_ref[...] / l_ref[...]
        o_ref[0] = jnp.concatenate([o[h * n_t:(h + 1) * n_t] for h in range(H_A)], axis=1)


def _dsa_sample_attend(page_table, k_pool, v_pool, layer, qbd, bias, knew, vnew, n_b, n_t):
    n_pages = page_table.shape[1]
    steps = n_pages // ATT_PAGES
    n_keys_pad = (n_pages + 1) * PAGE_SIZE
    page_spec = lambda u: pl.BlockSpec(
        (1, 1, PAGE_SIZE * H_A, DH_A), lambda r, j, pt, u=u: (layer, pt[r, j * ATT_PAGES + u], 0, 0))
    per_req = lambda s: pl.BlockSpec((1,) + s, lambda r, j, pt: (r, 0, 0))
    rows = H_A * n_t
    return pl.pallas_call(
        functools.partial(_dsa_sample_attend_kernel, n_t=n_t, steps=steps),
        grid_spec=pltpu.PrefetchScalarGridSpec(
            num_scalar_prefetch=1,
            grid=(n_b, steps),
            in_specs=[page_spec(u) for u in range(ATT_PAGES)] * 2
            + [per_req((rows, W_A)), per_req((n_t, n_keys_pad))]
            + [pl.BlockSpec((1, 1, PAGE_SIZE * H_A, DH_A), lambda r, j, pt: (r, 0, 0, 0))] * 2,
            out_specs=per_req((n_t, W_A)),
            scratch_shapes=[pltpu.VMEM((rows, LANE), F32), pltpu.VMEM((rows, LANE), F32),
                            pltpu.VMEM((rows, DH_A), F32)]),
        out_shape=jax.ShapeDtypeStruct((n_b, n_t, W_A), F32),
        compiler_params=_cparams(("parallel", "arbitrary")),
        name="dsa_sample_attend",
    )(page_table, *([k_pool] * ATT_PAGES), *([v_pool] * ATT_PAGES), qbd, bias, knew, vnew)


def kernel(x_prompt, x_sample, mem_prompt, cache_k, cache_v, cache_idx_k, state_hgrn, cache_mem_k,
           cache_mem_v, page_table, w_in, lb_logits, hgrn_norm_g, w_mem_k, w_mem_v, w_out, ln_g, ln_b):
    n_bp, t_p, _ = x_prompt.shape
    n_bs, t_s, _ = x_sample.shape
    n_pages = page_table.shape[1]
    past = n_pages * PAGE_SIZE
    l = 0

    lb_all = jnp.cumsum(jax.nn.softmax(lb_logits.astype(F32), axis=0), axis=0)
    lb = lb_all[l][None, :]
    g_norm = hgrn_norm_g[l][None, :]
    w_t = _cast_rows(jnp.swapaxes(w_in, 1, 2), l, tr=1024)
    w_o = w_out[l].astype(BF16)
    lg, lbias = ln_g[l][None, :], ln_b[l][None, :]

    xp = x_prompt.reshape(n_bp * t_p, D_MODEL)
    z, zs = _proj(xp, w_t, tm=1024, tn=1024)
    pos_p = jnp.arange(t_p, dtype=jnp.int32)
    tabs = _rope_tables(pos_p, DH_A) + _rope_tables(pos_p, D_IDX)
    q, kf, kb, vf, vt, iqw, sgn_t, ikf, ike, iko = _prep(z, zs, tabs, tm=256)
    a_out = _dsa_prompt(q, kb, vt, iqw, sgn_t, ike, iko, n_bp, t_p, tq=256)
    s0 = jnp.zeros((1, n_bp, H_B, DK_B, DV_B), F32)
    bn, s_p = _hgrn(z, lb, g_norm, s0, 0, n_bp, t_p, chunk=128, heads=H_B, mm_dtype=BF16)
    mk_p, mv_p = _mem_kv(mem_prompt, w_mem_k[l].astype(BF16), w_mem_v[l].astype(BF16))
    m_out = _mem_attend(z, mk_p, mv_p, 0, n_bp, t_p, tq=512, mm_dtype=BF16)
    y_p = _merge(a_out, z, bn, m_out, xp, w_o, lg, lbias, tm=256)

    n_s = n_bs * t_s
    xs = x_sample.reshape(n_s, D_MODEL)
    z2, zs2 = _proj(xs, w_t, tm=n_s, tn=1024)
    pos_s = past + jnp.arange(t_s, dtype=jnp.int32)
    tabs_s = [jnp.tile(t, (n_bs, 1)) for t in _rope_tables(pos_s, DH_A) + _rope_tables(pos_s, D_IDX)]
    q2, kf2, kb2, vf2, vt2, iqw2, sgn_t2, ikf2, ike2, iko2 = _prep(z2, zs2, tabs_s, tm=n_s)
    nsel_s = min(TOPK_MAX, (past + t_s) // 4)
    lq = iqw2.reshape(n_bs, t_s, H_IDX, D_IDX).transpose(0, 2, 1, 3).reshape(n_bs, H_IDX * t_s, D_IDX)
    sgb = jnp.broadcast_to(
        sgn_t2.reshape(H_IDX, n_bs, t_s).transpose(1, 0, 2).reshape(n_bs, H_IDX * t_s, 1),
        (n_bs, H_IDX * t_s, PAGE_SIZE))
    pad_keys = lambda a: jnp.pad(a.reshape((n_bs, t_s) + a.shape[1:]),
                                 ((0, 0), (0, PAGE_SIZE - t_s)) + ((0, 0),) * (a.ndim - 1))
    idx_pool_t = jnp.swapaxes(cache_idx_k, 2, 3)
    bias = _dsa_sample_select(page_table, idx_pool_t, l, lq, sgb, jnp.swapaxes(pad_keys(ikf2), 1, 2),
                              n_bs, t_s, nsel_s).reshape(n_bs, t_s, -1)
    q4 = q2.reshape(n_bs, t_s, H_A, DH_A)
    eye = jnp.eye(H_A, dtype=q2.dtype)
    qbd = (q4[:, None, :, :, :] * eye[None, :, None, :, None]).reshape(n_bs, H_A * t_s, W_A)
    n_layers, n_phys = cache_k.shape[:2]
    as_rows = lambda a: a.reshape(a.shape[0], a.shape[1], PAGE_SIZE * H_A, DH_A)
    a_out2 = _dsa_sample_attend(page_table, as_rows(cache_k), as_rows(cache_v), l, qbd, bias,
                                as_rows(pad_keys(kf2)[:, None]), as_rows(pad_keys(vf2)[:, None]), n_bs, t_s)
    bn2, s_s = _hgrn(z2, lb, g_norm, state_hgrn, l, n_bs, t_s, chunk=t_s, heads=H_B, mm_dtype=F32)
    m_out2 = _mem_attend(z2, cache_mem_k, cache_mem_v, l, n_bs, t_s, tq=t_s, mm_dtype=F32)
    y_s = _merge(a_out2.reshape(n_s, W_A), z2, bn2, m_out2, xs, w_o, lg, lbias, tm=n_s)

    return (y_p.reshape(n_bp, t_p, D_MODEL), y_s.reshape(n_bs, t_s, D_MODEL),
            kf.reshape(1, n_bp, t_p, H_A, DH_A), vf.reshape(1, n_bp, t_p, H_A, DH_A),
            ikf.reshape(1, n_bp, t_p, D_IDX), s_p, mk_p, mv_p,
            kf2.reshape(1, n_bs, t_s, H_A, DH_A), vf2.reshape(1, n_bs, t_s, H_A, DH_A),
            ikf2.reshape(1, n_bs, t_s, D_IDX), s_s)
```

```python
import functools

import numpy as np
import jax
import jax.numpy as jnp
from jax import lax
from jax.experimental import pallas as pl
from jax.experimental.pallas import tpu as pltpu

D_MODEL = 2048
DH_A = 128
H_A = 8
W_A = H_A * DH_A
H_IDX = 16
D_IDX = 64
W_IDX = H_IDX * D_IDX
TOPK_MAX = 256
DK_B = 128
DV_B = 128
H_B = 4
W_B = H_B * DV_B
DH_M = 128
H_M = 4
W_M = H_M * DH_M
N_MEM = 256
PAGE_SIZE = 128
ROPE_THETA = 500000.0
ROPE_FRAC = 4
LN_EPS = 1e-5
RMS_EPS = 1e-6
DEPTH = 1
ALPHA = (2.0 * DEPTH) ** 0.25

LANE = 128
VMEM_LIMIT = 48 * 1024 * 1024

C_AQ, C_AK, C_AV, C_AG, C_IQ = 0, 1024, 2048, 3072, 4096
C_BQ, C_BF, C_BI, C_BG, C_MQ, C_MG = 5120, 5632, 6144, 6656, 7168, 7680
N_MAIN = 8192

NEG = -1e30
Q_SCALE = DH_A ** -0.5 * 1.4426950408889634
ATT_GROUP = H_A
VT_ROWS = DH_A + 16
INT_MIN = -(2 ** 31)
BF16 = jnp.bfloat16
F32 = jnp.float32


def _dot(a, b, **kw):
    return jnp.dot(a, b, preferred_element_type=F32, **kw)


def _dot_nt(a, b):
    return lax.dot_general(a, b, (((1,), (1,)), ((), ())), preferred_element_type=F32)


def _dot_tn(a, b):
    return lax.dot_general(a, b, (((0,), (0,)), ((), ())), preferred_element_type=F32)


def _sigmoid(x):
    return 1.0 / (1.0 + jnp.exp(-x))


def _silu(x):
    return x * _sigmoid(x)


def _cparams(sem):
    return pltpu.CompilerParams(dimension_semantics=sem, vmem_limit_bytes=VMEM_LIMIT)


def _proj_kernel(x_ref, w_ref, ws_ref, o_ref, os_ref, xb_ref):
    j = pl.program_id(1)

    @pl.when(j == 0)
    def _():
        xb = x_ref[...].astype(BF16)
        xb_ref[...] = xb
        os_ref[...] = _dot_nt(xb, ws_ref[...])

    o_ref[...] = _dot_nt(xb_ref[...], w_ref[...])


W_SIDE_ROW = C_IQ + W_IDX
W_SIDE_ROWS = D_IDX + H_IDX
N_IN = N_MAIN + W_SIDE_ROWS


def _proj(x, w_t, tm, tn):
    m = x.shape[0]
    n_lo = W_SIDE_ROW // tn

    def w_rows(i, j):
        return ((j * (tn // 16) + jnp.where(j < n_lo, 0, W_SIDE_ROWS // 16)) * 16, 0)

    return pl.pallas_call(
        _proj_kernel,
        grid=(m // tm, N_MAIN // tn),
        in_specs=[pl.BlockSpec((tm, D_MODEL), lambda i, j: (i, 0)),
                  pl.BlockSpec((pl.Element(tn), pl.Element(D_MODEL)), w_rows),
                  pl.BlockSpec((pl.Element(LANE), pl.Element(D_MODEL)), lambda i, j: (W_SIDE_ROW, 0))],
        out_specs=[pl.BlockSpec((tm, tn), lambda i, j: (i, j)),
                   pl.BlockSpec((tm, LANE), lambda i, j: (i, 0))],
        out_shape=[jax.ShapeDtypeStruct((m, N_MAIN), F32),
                   jax.ShapeDtypeStruct((m, LANE), F32)],
        scratch_shapes=[pltpu.VMEM((tm, D_MODEL), BF16)],
        compiler_params=_cparams(("parallel", "arbitrary")),
        name="in_proj",
    )(x, w_t, w_t)


def _cast_kernel(x_ref, o_ref):
    o_ref[...] = x_ref[0].astype(BF16)


def _cast_rows(w, layer, tr):
    rows = w.shape[1]
    return pl.pallas_call(
        _cast_kernel,
        grid=(pl.cdiv(rows, tr),),
        in_specs=[pl.BlockSpec((1, tr, D_MODEL), lambda i: (layer, i, 0))],
        out_specs=pl.BlockSpec((tr, D_MODEL), lambda i: (i, 0)),
        out_shape=jax.ShapeDtypeStruct((rows, D_MODEL), BF16),
        compiler_params=_cparams(("parallel",)),
        name="w_cast",
    )(w)


def _swap_halves_index(shape, d):
    half = d // ROPE_FRAC // 2
    lane = lax.broadcasted_iota(jnp.int32, shape, 1)
    pos = lane & (d - 1)
    return jnp.where(pos < half, lane + half, jnp.where(pos < 2 * half, lane - half, lane))


def _rope(x, c, s, swap):
    return x * c + jnp.take_along_axis(x, swap, axis=1) * s


def _prep_kernel(aq_ref, ak_ref, av_ref, iq_ref, zs_ref, c1_ref, s1_ref, c2_ref, s2_ref,
                 q_ref, kf_ref, kb_ref, vf_ref, vt_ref, iqw_ref, sgn_ref,
                 ikf_ref, ike_ref, iko_ref):
    c1, s1, c2, s2 = c1_ref[...], s1_ref[...], c2_ref[...], s2_ref[...]
    swap_a = _swap_halves_index(c1.shape, DH_A)
    swap_i = _swap_halves_index(c2.shape, D_IDX)
    k_heads, v_heads = [], []
    for h in range(H_A):
        sl = slice(h * DH_A, (h + 1) * DH_A)
        q_ref[:, sl] = (_rope(aq_ref[:, sl], c1, s1, swap_a) * Q_SCALE).astype(BF16)
        kr = _rope(ak_ref[:, sl], c1, s1, swap_a)
        kb_ref[:, sl] = kr.astype(BF16)
        v = av_ref[:, sl]
        vt_ref[h * VT_ROWS:h * VT_ROWS + DH_A, :] = v.T.astype(BF16)
        vt_ref[h * VT_ROWS + DH_A:(h + 1) * VT_ROWS, :] = jnp.ones((VT_ROWS - DH_A, v.shape[0]), BF16)
        k_heads.append(kr)
        v_heads.append(v)
    kf_ref[...] = jnp.swapaxes(jnp.stack(k_heads, axis=0), 0, 1)
    vf_ref[...] = jnp.swapaxes(jnp.stack(v_heads, axis=0), 0, 1)

    zs = zs_ref[...]
    lane = lax.broadcasted_iota(jnp.int32, zs.shape, 1)
    ikr = _rope(zs, c2, s2, swap_i)
    ikf_ref[...] = ikr[:, :D_IDX]
    ike_ref[...] = jnp.where(lane < D_IDX, ikr, 0.0).astype(BF16)
    iko_ref[...] = jnp.where(lane >= D_IDX, pltpu.roll(ikr, D_IDX, 1), 0.0).astype(BF16)
    sgn_t = jnp.sign(pltpu.roll(zs, LANE - D_IDX, 1)).T
    sgn_ref[...] = sgn_t[:H_IDX, :]
    wscale = (D_IDX ** -0.5) * (H_IDX ** -0.5)
    for p in range(H_IDX // 2):
        sl = slice(p * LANE, (p + 1) * LANE)
        wa = jnp.abs(zs[:, D_IDX + 2 * p:D_IDX + 2 * p + 1])
        wb = jnp.abs(zs[:, D_IDX + 2 * p + 1:D_IDX + 2 * p + 2])
        wpair = jnp.where(lane < D_IDX, wa, wb) * wscale
        iqw_ref[:, sl] = (_rope(iq_ref[:, sl], c2, s2, swap_i) * wpair).astype(BF16)


def _prep(z, zs, tabs, tm):
    m = z.shape[0]
    period = tabs[0].shape[0] // tm
    zblk = lambda c: pl.BlockSpec((tm, 1024), lambda i, c=c: (i, c // 1024))
    tab = pl.BlockSpec((tm, LANE), lambda i: (i % period, 0))
    row = lambda w: pl.BlockSpec((tm, w), lambda i: (i, 0))
    col = lambda w: pl.BlockSpec((w, tm), lambda i: (0, i))
    sds = lambda w, dt: jax.ShapeDtypeStruct((m, w), dt)
    heads =pl.BlockSpec((tm, H_A, DH_A), lambda i: (i, 0, 0))
    heads_sds = jax.ShapeDtypeStruct((m, H_A, DH_A), F32)
    return pl.pallas_call(
        _prep_kernel,
        grid=(m // tm,),
        in_specs=[zblk(C_AQ), zblk(C_AK), zblk(C_AV), zblk(C_IQ), row(LANE)] + [tab] * 4,
        out_specs=[row(W_A), heads, row(W_A), heads, col(H_A * VT_ROWS), row(W_IDX), col(H_IDX),
                   row(D_IDX), row(LANE), row(LANE)],
        out_shape=[sds(W_A, BF16), heads_sds, sds(W_A, BF16), heads_sds,
                   jax.ShapeDtypeStruct((H_A * VT_ROWS, m), BF16), sds(W_IDX, BF16),
                   jax.ShapeDtypeStruct((H_IDX, m), F32),
                   sds(D_IDX, F32), sds(LANE, BF16), sds(LANE, BF16)],
        compiler_params=_cparams(("parallel",)),
        name="rope_prep",
    )(z, z, z, z, zs, *tabs)


def _rope_tables(pos, d):
    r = d // ROPE_FRAC
    half = r // 2
    n = pos.shape[0]
    inv = ROPE_THETA ** (-jnp.arange(half, dtype=F32) / half)
    ang = pos.astype(F32)[:, None] * inv[None, :]
    cos, sin = jnp.cos(ang), jnp.sin(ang)
    c = jnp.concatenate([cos, cos, jnp.ones((n, d - r), F32)], axis=1)
    s = jnp.concatenate([-sin, sin, jnp.zeros((n, d - r), F32)], axis=1)
    reps = LANE // d
    return [jnp.tile(t, (1, reps)) for t in (c, s)]


KEY_NEG_INF = -2139095041
F32_LOWEST = -3.4028234663852886e38


def _key_to_float(key):
    key = jnp.maximum(key, jnp.int32(KEY_NEG_INF))
    b = key ^ (lax.shift_right_arithmetic(key, 31) & jnp.int32(0x7FFFFFFF))
    return lax.bitcast_convert_type(b, F32)


def _lane_total(part):
    return _dot(part.astype(BF16), jnp.ones((LANE, LANE), BF16))


def _radix_select(count_ge, shape, total, nsel):
    def body(bi, carry):
        pu, cacc = carry
        bit = jnp.int32(31) - bi
        cand_u = pu | lax.shift_left(jnp.int32(1), bit)
        cnt = count_ge(_key_to_float(cand_u ^ jnp.int32(INT_MIN)))
        ok = cnt >= nsel
        return jnp.where(ok, cand_u, pu), jnp.where(ok, cnt, cacc)

    pu0 = jnp.zeros(shape, jnp.int32)
    c0 = jnp.full(shape, total, F32)
    pu, cacc = lax.fori_loop(0, 32, body, (pu0, c0))
    return _key_to_float(pu ^ jnp.int32(INT_MIN)), cacc


def _dsa_prompt_kernel(q_ref, k_ref, vt_ref, iq_ref, sgn_ref, ike_ref, iko_ref, o_ref,
                       key_ref, bias_ref, m_ref, acc_ref, *, tq, nsel):
    tk = tq
    i = pl.program_id(1)
    nch = i + 1
    sub = 8

    kidx0 = lax.broadcasted_iota(jnp.int32, (tk, tq), 0)
    qidx = i * tq + lax.broadcasted_iota(jnp.int32, (tk, tq), 1)

    def chunk_slice(c):
        return pl.ds(pl.multiple_of(c * tk, tk), tk)

    def score_body(c, carry):
        ds = chunk_slice(c)
        ke = ike_ref[ds, :]
        ko = iko_ref[ds, :]
        acc = jnp.zeros((tk, tq), F32)
        for h in range(H_IDX):
            rhs = iq_ref[:, (h // 2) * LANE:(h // 2 + 1) * LANE]
            d = _dot_nt(ke if h % 2 == 0 else ko, rhs)
            acc = acc + sgn_ref[h:h + 1, :] * jnp.maximum(d, 0.0)
        key_ref[ds, :] = jnp.where(kidx0 + c * tk <= qidx, acc, -jnp.inf)
        return carry

    lax.fori_loop(0, nch, score_body, 0)

    def count(cmp_fn):
        def hits(c):
            hit = jnp.where(cmp_fn(key_ref[chunk_slice(c), :]), 1.0, 0.0)
            return jnp.sum(hit.reshape(tk // sub, sub, tq), axis=0)

        def body(c2, parts):
            return parts[0] + hits(2 * c2), parts[1] + hits(2 * c2 + 1)

        zero = jnp.zeros((sub, tq), F32)
        p0, p1 = lax.fori_loop(0, nch // 2, body, (zero, zero))
        p0 = lax.cond(nch % 2 == 1, lambda p: p + hits(nch - 1), lambda p: p, p0)
        return jnp.sum(p0 + p1, axis=0, keepdims=True)

    total = (nch * tk).astype(F32)
    tau, c_ge = _radix_select(lambda cand: count(lambda kc: kc >= cand), (1, tq), total, float(nsel))
    tau_eff = jnp.maximum(tau, F32_LOWEST)

    def bias_body(c, carry):
        ds = chunk_slice(c)
        bias_ref[ds, :] = jnp.where(key_ref[ds, :] >= tau_eff, 0.0, NEG)
        return carry

    lax.fori_loop(0, nch, bias_body, 0)

    tie_q = jnp.logical_and(c_ge > float(nsel), tau > -jnp.inf)
    has_tie = jnp.max(jnp.where(tie_q, 1.0, 0.0)) > 0.5

    @pl.when(has_tie)
    def _():
        need = float(nsel) - count(lambda kc: kc > tau)
        lower = jnp.where(lax.broadcasted_iota(jnp.int32, (tk, tk), 1)
                          <= lax.broadcasted_iota(jnp.int32, (tk, tk), 0), 1.0, 0.0).astype(BF16)
        real = tau > -jnp.inf

        def tie_body(c, run):
            ds = chunk_slice(c)
            kc = key_ref[ds, :]
            eqf = jnp.where(kc == tau, 1.0, 0.0)
            pre = _dot(lower, eqf.astype(BF16)) + run
            keep = jnp.where(kc > tau, 1.0, jnp.where(real, eqf * jnp.where(pre <= need, 1.0, 0.0), 0.0))
            bias_ref[ds, :] = jnp.where(keep > 0.5, 0.0, NEG)
            return run + jnp.sum(eqf, axis=0, keepdims=True)

        lax.fori_loop(0, nch, tie_body, jnp.zeros((1, tq), F32))

    m_ref[...] = jnp.full(m_ref.shape, NEG, F32)
    acc_ref[...] = jnp.zeros(acc_ref.shape, F32)

    def attn_body(c, carry):
        ds = chunk_slice(c)
        bias = bias_ref[ds, :]
        for h0 in range(0, H_A, ATT_GROUP):
            group = range(h0, h0 + ATT_GROUP)
            ss = {h: _dot_nt(k_ref[ds, h * DH_A:(h + 1) * DH_A], q_ref[:, h * DH_A:(h + 1) * DH_A])
                  for h in group}
            ps, alphas = {}, {}
            for h in group:
                s = ss[h] + bias
                m_prev = m_ref[h]
                m_new = jnp.maximum(m_prev, jnp.max(s, axis=0, keepdims=True))
                alphas[h] = jnp.exp2(m_prev - m_new)
                ps[h] = jnp.exp2(s - m_new).astype(BF16)
                m_ref[h] = m_new
            for h in group:
                acc_ref[h] = alphas[h] * acc_ref[h] + _dot(vt_ref[h * VT_ROWS:(h + 1) * VT_ROWS, ds], ps[h])
        return carry

    lax.fori_loop(0, nch, attn_body, 0)
    for h in range(H_A):
        acc = acc_ref[h]
        o_ref[:, h * DH_A:(h + 1) * DH_A] = (acc[:DH_A] / acc[DH_A:DH_A + 1]).T


def _dsa_prompt(q, k, vt, iqw, sgn_t, ike, iko, n_b, n_t, tq):
    nsel = min(TOPK_MAX, n_t // 4)
    nq = n_t // tq
    qblk = lambda w: pl.BlockSpec((tq, w), lambda b, i: (b * nq + i, 0))
    full = lambda w: pl.BlockSpec((n_t, w), lambda b, i: (b, 0))
    return pl.pallas_call(
        functools.partial(_dsa_prompt_kernel, tq=tq, nsel=nsel),
        grid=(n_b, nq),
        in_specs=[qblk(W_A), full(W_A), pl.BlockSpec((H_A * VT_ROWS, n_t), lambda b, i: (0, b)), qblk(W_IDX),
                  pl.BlockSpec((H_IDX, tq), lambda b, i: (0, b * nq + i)), full(LANE), full(LANE)],
        out_specs=qblk(W_A),
        out_shape=jax.ShapeDtypeStruct((n_b * n_t, W_A), F32),
        scratch_shapes=[pltpu.VMEM((n_t, tq), F32),
                        pltpu.VMEM((n_t, tq), F32),
                        pltpu.VMEM((H_A, 1, tq), F32),
                        pltpu.VMEM((H_A, VT_ROWS, tq), F32)],
        compiler_params=_cparams(("parallel", "arbitrary")),
        name="dsa_prompt",
    )(q, k, vt, iqw, sgn_t, ike, iko)


HGRN_BLOCK = 8


def _hgrn_levels(chunk):
    levels, m = [], HGRN_BLOCK
    while m < chunk:
        levels.append(m)
        m *= 2
    return levels


def _hgrn_masks(chunk):
    row, col = np.indices((chunk, chunk))
    masks = []
    for m in _hgrn_levels(chunk):
        same_pair = (row // (2 * m)) == (col // (2 * m))
        masks.append(same_pair & (row % (2 * m) >= m) & (col % (2 * m) < m))
    for j in range(HGRN_BLOCK):
        masks.append((row - col == j) & (row % HGRN_BLOCK >= j))
    return np.stack(masks).astype(np.float32)


def _hgrn_kernel(bq_ref, bf_ref, bi_ref, lb_ref, g_ref, s0_ref, mask_ref, o_ref, s_out_ref, st_ref,
                 *, n_t, chunk, heads, mm_dtype):
    c_ = chunk
    g = g_ref[...]
    for hh in range(heads):
        st_ref[hh] = s0_ref[0, 0, hh].T
    row = lax.broadcasted_iota(jnp.int32, (c_, c_), 0)
    col = lax.broadcasted_iota(jnp.int32, (c_, c_), 1)
    lower = jnp.where(col <= row, 1.0, 0.0)
    levels = _hgrn_levels(c_)

    def head_chunk(hh, ds):
        hs = slice(hh * DK_B, (hh + 1) * DK_B)
        lb = lb_ref[:, hs]
        bq = bq_ref[ds, hs]
        qs = bq * _sigmoid(bq)
        f = lb + (1.0 - lb) * _sigmoid(bf_ref[ds, hs])
        kk = 1.0 - f
        v = bi_ref[ds, hs]
        gcum = _dot(lower, jnp.log2(f), precision=lax.Precision.HIGHEST)
        a = jnp.zeros((c_, c_), F32)
        for li, m in enumerate(levels):
            g3 = gcum.reshape(c_ // (2 * m), 2 * m, DK_B)
            ref = jnp.broadcast_to(g3[:, m - 1:m, :], g3.shape).reshape(c_, DK_B)
            qh = qs * jnp.exp2(jnp.minimum(gcum - ref, 0.0))
            kh = kk * jnp.exp2(jnp.minimum(ref - gcum, 0.0))
            a = a + _dot_nt(qh.astype(mm_dtype), kh.astype(mm_dtype)) * mask_ref[li]
        for j in range(HGRN_BLOCK):
            kj = kk if j == 0 else pltpu.roll(kk, j, 0)
            gj = gcum if j == 0 else pltpu.roll(gcum, j, 0)
            e = jnp.exp2(jnp.minimum(gcum - gj, 0.0))
            aj = jnp.sum(qs * kj * e, axis=1, keepdims=True)
            a = a + aj * mask_ref[len(levels) + j]
        st = st_ref[hh]
        o = (_dot_nt((qs * jnp.exp2(gcum)).astype(mm_dtype), st.astype(mm_dtype))
             + _dot(a.astype(mm_dtype), v.astype(mm_dtype)))
        glast = gcum[c_ - 1:c_, :]
        kd = kk * jnp.exp2(glast - gcum)
        st_ref[hh] = jnp.exp2(glast) * st + _dot_tn(v.astype(mm_dtype), kd.astype(mm_dtype))
        o_ref[ds, hs] = o * lax.rsqrt(jnp.mean(o * o, axis=1, keepdims=True) + RMS_EPS) * g

    def chunk_body(ci, carry):
        ds = pl.ds(pl.multiple_of(ci * c_, c_), c_)
        for hh in range(heads):
            head_chunk(hh, ds)
        return carry

    lax.fori_loop(0, n_t // c_, chunk_body, 0)
    for hh in range(heads):
        s_out_ref[0, 0, hh] = st_ref[hh].T


def _hgrn(z, lb, g, s0, layer, n_b, n_t, chunk, heads, mm_dtype):
    w = heads * DK_B
    zcol = lambda c: pl.BlockSpec((n_t, w), lambda b, h, c=c: (b, c // w + h))
    sblk = lambda lyr: pl.BlockSpec((1, 1, heads, DK_B, DV_B), lambda b, h: (lyr, b, h, 0, 0))
    masks = jnp.asarray(_hgrn_masks(chunk))
    return pl.pallas_call(
        functools.partial(_hgrn_kernel, n_t=n_t, chunk=chunk, heads=heads, mm_dtype=mm_dtype),
        grid=(n_b, H_B // heads),
        in_specs=[zcol(C_BQ), zcol(C_BF), zcol(C_BI),
                  pl.BlockSpec((1, w), lambda b, h: (0, h)),
                  pl.BlockSpec((1, DV_B), lambda b, h: (0, 0)),
                  sblk(layer),
                  pl.BlockSpec(masks.shape, lambda b, h: (0, 0, 0))],
        out_specs=[pl.BlockSpec((n_t, w), lambda b, h: (b, h)), sblk(0)],
        out_shape=[jax.ShapeDtypeStruct((n_b * n_t, W_B), F32),
                   jax.ShapeDtypeStruct((1, n_b, H_B, DK_B, DV_B), F32)],
        scratch_shapes=[pltpu.VMEM((heads, DV_B, DK_B), F32)],
        compiler_params=_cparams(("parallel", "arbitrary")),
        name="hgrn2",
    )(z, z, z, lb, g, s0, masks)


def _mem_kv_kernel(x_ref, wk_ref, wv_ref, mk_ref, mv_ref):
    x = x_ref[0].astype(BF16)
    mk = _dot(x, wk_ref[...])
    mv = _dot(x, wv_ref[...])
    heads = lambda a: jnp.stack([a[:, h * DH_M:(h + 1) * DH_M] for h in range(H_M)], axis=0)
    mk_ref[0, 0] = jnp.swapaxes(heads(mk), 0, 1)
    mv_ref[0, 0] = jnp.swapaxes(heads(mv), 0, 1)


def _mem_kv(mem, wk, wv):
    n_b = mem.shape[0]
    wspec = pl.BlockSpec((D_MODEL, W_M), lambda b: (0, 0))
    ospec = pl.BlockSpec((1, 1, N_MEM, H_M, DH_M), lambda b: (0, b, 0, 0, 0))
    sds = jax.ShapeDtypeStruct((1, n_b, N_MEM, H_M, DH_M), F32)
    return pl.pallas_call(
        _mem_kv_kernel,
        grid=(n_b,),
        in_specs=[pl.BlockSpec((1, N_MEM, D_MODEL), lambda b: (b, 0, 0)), wspec, wspec],
        out_specs=[ospec, ospec],
        out_shape=[sds, sds],
        compiler_params=_cparams(("parallel",)),
        name="mem_kv",
    )(mem, wk, wv)


def _mem_kernel(q_ref, mk_ref, mv_ref, o_ref, *, tq, reqs, mm_dtype):
    for r in range(reqs):
        rows = slice(r * tq, (r + 1) * tq)
        mk = jnp.swapaxes(mk_ref[0, r], 0, 1).astype(mm_dtype)
        mv = jnp.swapaxes(mv_ref[0, r], 0, 1).astype(mm_dtype)
        for h in range(H_M):
            hs = slice(h * DH_M, (h + 1) * DH_M)
            q = (q_ref[rows, hs] * (DH_M ** -0.5)).astype(mm_dtype)
            s = _dot_nt(q, mk[h])
            p = jnp.exp(s - jnp.max(s, axis=1, keepdims=True))
            l = jnp.sum(p, axis=1, keepdims=True)
            o_ref[rows, hs] = _dot(p.astype(mm_dtype), mv[h]) / l


def _mem_attend(z, mk, mv, layer, n_b, n_t, tq, reqs, mm_dtype):
    nq = n_t // tq
    assert reqs == 1 or nq == 1
    kv = pl.BlockSpec((1, reqs, N_MEM, H_M, DH_M), lambda b, i: (layer, b, 0, 0, 0))
    return pl.pallas_call(
        functools.partial(_mem_kernel, tq=tq, reqs=reqs, mm_dtype=mm_dtype),
        grid=(n_b // reqs, nq),
        in_specs=[pl.BlockSpec((reqs * tq, W_M), lambda b, i: (b * nq + i, C_MQ // W_M)), kv, kv],
        out_specs=pl.BlockSpec((reqs * tq, W_M), lambda b, i: (b * nq + i, 0)),
        out_shape=jax.ShapeDtypeStruct((n_b * n_t, W_M), F32),
        compiler_params=_cparams(("parallel", "parallel")),
        name="mem_attend",
    )(z, mk, mv)


def _merge_kernel(a_ref, ag_ref, bn_ref, bg_ref, mo_ref, mg_ref, h_ref, wo_ref, lg_ref, lbias_ref, o_ref):
    a = (a_ref[...] * _silu(ag_ref[...])).astype(BF16)
    b = (bn_ref[...] * _silu(bg_ref[...])).astype(BF16)
    m = (mo_ref[...] * _silu(mg_ref[...])).astype(BF16)
    y = (_dot(a, wo_ref[0:W_A, :]) + _dot(b, wo_ref[W_A:W_A + W_B, :])
         + _dot(m, wo_ref[W_A + W_B:D_MODEL, :]))
    r = ALPHA * h_ref[...] + y
    xc = r - jnp.mean(r, axis=1, keepdims=True)
    var = jnp.mean(xc * xc, axis=1, keepdims=True)
    o_ref[...] = xc * lax.rsqrt(var + LN_EPS) * lg_ref[...] + lbias_ref[...]


def _merge(a_out, z, bn, m_out, h, w_out, ln_g, ln_b, tm):
    m = h.shape[0]
    row = lambda w: pl.BlockSpec((tm, w), lambda i: (i, 0))
    zcol = lambda c, w: pl.BlockSpec((tm, w), lambda i, c=c, w=w: (i, c // w))
    const = lambda s: pl.BlockSpec(s, lambda i: (0, 0), pipeline_mode=pl.Buffered(1))
    return pl.pallas_call(
        _merge_kernel,
        grid=(m // tm,),
        in_specs=[row(W_A), zcol(C_AG, W_A), row(W_B), zcol(C_BG, W_B), row(W_M), zcol(C_MG, W_M),
                  row(D_MODEL), const((D_MODEL, D_MODEL)), const((1, D_MODEL)), const((1, D_MODEL))],
        out_specs=row(D_MODEL),
        out_shape=jax.ShapeDtypeStruct((m, D_MODEL), F32),
        compiler_params=_cparams(("parallel",)),
        name="merge",
    )(a_out, z, bn, z, m_out, z, h, w_out, ln_g, ln_b)


PAGES_PER_STEP = 8


def _dsa_sample_select_kernel(pt_ref, pool_ref, lq_ref, sg_ref, iknew_ref, bias_ref, key_ref, page_buf, sem,
                              *, layer, n_t, n_pages, nsel):
    r = pl.program_id(0)
    slot = r % 2
    n_keys_pad = (n_pages + 1) * PAGE_SIZE

    def page_copy(req, p, s):
        return pltpu.make_async_copy(pool_ref.at[layer, pt_ref[req, p]], page_buf.at[s, p], sem.at[s])

    def for_pages(fn):
        def body(p, carry):
            fn(p)
            return carry
        lax.fori_loop(0, n_pages, body, 0)

    @pl.when(r == 0)
    def _():
        for_pages(lambda p: page_copy(0, p, 0).start())

    @pl.when(r + 1 < pl.num_programs(0))
    def _():
        for_pages(lambda p: page_copy(r + 1, p, 1 - slot).start())

    for_pages(lambda p: page_copy(r, p, slot).wait())

    lq = lq_ref[0]
    sg = sg_ref[0]

    def score(ik_t):
        d = _dot(lq, ik_t.astype(BF16))
        r = jnp.maximum(d, 0.0) * jnp.concatenate([sg] * (ik_t.shape[1] // PAGE_SIZE), axis=1)
        s = r[0:n_t]
        for h in range(1, H_IDX):
            s = s + r[h * n_t:(h + 1) * n_t]
        return s

    width = PAGES_PER_STEP * PAGE_SIZE

    rows = pl.ds(pl.multiple_of(r * n_t, n_t), n_t)

    def group_body(gi, carry):
        base = gi * PAGES_PER_STEP
        ik_step = jnp.concatenate([page_buf[slot, base + u] for u in range(PAGES_PER_STEP)], axis=1)
        key_ref[rows, pl.ds(pl.multiple_of(gi * width, width), width)] = score(ik_step)
        return carry

    lax.fori_loop(0, n_pages // PAGES_PER_STEP, group_body, 0)

    past = n_pages * PAGE_SIZE
    s_new = score(iknew_ref[0])
    t_i = lax.broadcasted_iota(jnp.int32, (n_t, PAGE_SIZE), 0)
    k_i = lax.broadcasted_iota(jnp.int32, (n_t, PAGE_SIZE), 1)
    key_ref[rows, past:n_keys_pad] = jnp.where(k_i <= t_i, s_new, -jnp.inf)

    @pl.when(r == pl.num_programs(0) - 1)
    def _():
        n_rows = key_ref.shape[0]
        n_tiles = n_keys_pad // LANE

        def count_ge(cand):
            part = jnp.zeros((n_rows, LANE), F32)
            for c in range(n_tiles):
                part = part + jnp.where(key_ref[:, c * LANE:(c + 1) * LANE] >= cand, 1.0, 0.0)
            return _lane_total(part)

        tau_r, c_ge_r = _radix_select(count_ge, (n_rows, LANE), float(n_keys_pad), float(nsel))
        tau = tau_r[:, 0:1]
        tau_eff = jnp.maximum(tau, F32_LOWEST)
        bias_ref[...] = jnp.where(key_ref[...] >= tau_eff, 0.0, NEG)

        tie_rows = jnp.logical_and(c_ge_r > float(nsel), tau_r > -jnp.inf)
        has_tie = jnp.max(jnp.where(tie_rows, 1.0, 0.0)) > 0.5

        @pl.when(has_tie)
        def _():
            c_gt = jnp.sum(jnp.where(key_ref[...] > tau, 1.0, 0.0), axis=1, keepdims=True)
            need = float(nsel) - c_gt
            ri = lax.broadcasted_iota(jnp.int32, (PAGE_SIZE, PAGE_SIZE), 0)
            ci = lax.broadcasted_iota(jnp.int32, (PAGE_SIZE, PAGE_SIZE), 1)
            upper = jnp.where(ri <= ci, 1.0, 0.0)
            real = tau > -jnp.inf

            def tie_body(c, run):
                ds = pl.ds(pl.multiple_of(c * PAGE_SIZE, PAGE_SIZE), PAGE_SIZE)
                kc = key_ref[:, ds]
                eqf = jnp.where(kc == tau, 1.0, 0.0)
                pre = _dot(eqf, upper) + run
                keep = jnp.where(kc > tau, 1.0, jnp.where(real, eqf * jnp.where(pre <= need, 1.0, 0.0), 0.0))
                bias_ref[:, ds] = jnp.where(keep > 0.5, 0.0, NEG)
                return run + jnp.sum(eqf, axis=1, keepdims=True)

            lax.fori_loop(0, n_keys_pad // PAGE_SIZE, tie_body, jnp.zeros((n_rows, 1), F32))


def _dsa_sample_select(page_table, idx_pool, layer, lq, sgb, iknew, n_b, n_t, nsel):
    n_pages = page_table.shape[1]
    n_keys_pad = (n_pages + 1) * PAGE_SIZE
    per_req = lambda s: pl.BlockSpec((1,) + s, lambda r, pt: (r, 0, 0))
    return pl.pallas_call(
        functools.partial(_dsa_sample_select_kernel, layer=layer, n_t=n_t, n_pages=n_pages, nsel=nsel),
        grid_spec=pltpu.PrefetchScalarGridSpec(
            num_scalar_prefetch=1,
            grid=(n_b,),
            in_specs=[pl.BlockSpec(memory_space=pl.ANY),
                      per_req((H_IDX * n_t, D_IDX)), per_req((H_IDX * n_t, PAGE_SIZE)), per_req((D_IDX, PAGE_SIZE))],
            out_specs=pl.BlockSpec((n_b * n_t, n_keys_pad), lambda r, pt: (0, 0)),
            scratch_shapes=[pltpu.VMEM((n_b * n_t, n_keys_pad), F32),
                            pltpu.VMEM((2, n_pages, D_IDX, PAGE_SIZE), F32),
                            pltpu.SemaphoreType.DMA((2,))]),
        out_shape=jax.ShapeDtypeStruct((n_b * n_t, n_keys_pad), F32),
        compiler_params=_cparams(("arbitrary",)),
        name="dsa_sample_select",
    )(page_table, idx_pool, lq, sgb, iknew)


ATT_PAGES = 16


def _dsa_sample_attend_kernel(pt_ref, *refs, n_t, steps):
    kp = refs[:ATT_PAGES]
    vp = refs[ATT_PAGES:2 * ATT_PAGES]
    qbd_ref, bias_ref, knew_ref, vnew_ref, o_ref, m_ref, l_ref, acc_ref = refs[2 * ATT_PAGES:]
    j = pl.program_id(1)
    rows = H_A * n_t

    @pl.when(j == 0)
    def _():
        m_ref[...] = jnp.full(m_ref.shape, NEG, F32)
        l_ref[...] = jnp.zeros(l_ref.shape, F32)
        acc_ref[...] = jnp.zeros(acc_ref.shape, F32)

    qbd = qbd_ref[0]

    def update(kblk, vblk, bias):
        n = kblk.shape[0]
        s = _dot_nt(qbd, kblk) + jnp.concatenate([bias] * H_A, axis=0)
        m_prev = m_ref[...]
        m_new = jnp.maximum(m_prev, jnp.max(s, axis=1, keepdims=True))
        alpha = jnp.exp2(m_prev - m_new)
        p = jnp.exp2(s - jnp.concatenate([m_new] * (n // LANE), axis=1))
        l_ref[...] = alpha * l_ref[...] + jnp.sum(p, axis=1, keepdims=True)
        pv = _dot(p.astype(BF16), vblk)
        diag = jnp.concatenate(
            [pv[h * n_t:(h + 1) * n_t, h * DH_A:(h + 1) * DH_A] for h in range(H_A)], axis=0)
        acc_ref[...] = alpha * acc_ref[...] + diag
        m_ref[...] = m_new

    def page2d(ref):
        return jnp.concatenate(
            [ref[0, 0, pl.ds(h, PAGE_SIZE, stride=H_A), :] for h in range(H_A)], axis=1).astype(BF16)

    kblk = jnp.concatenate([page2d(kp[u]) for u in range(ATT_PAGES)], axis=0)
    vblk = jnp.concatenate([page2d(vp[u]) for u in range(ATT_PAGES)], axis=0)
    width = ATT_PAGES * PAGE_SIZE
    off = pl.multiple_of(j * width, width)
    update(kblk, vblk, bias_ref[0, :, pl.ds(off, width)])

    @pl.when(j == steps - 1)
    def _():
        past = steps * width
        update(page2d(knew_ref), page2d(vnew_ref), bias_ref[0, :, past:past + PAGE_SIZE])
        o = acc_ref[...] / l_ref[...]
        o_ref[0] = jnp.concatenate([o[h * n_t:(h + 1) * n_t] for h in range(H_A)], axis=1)


def _dsa_sample_attend(page_table, k_pool, v_pool, layer, qbd, bias, knew, vnew, n_b, n_t):
    n_pages = page_table.shape[1]
    steps = n_pages // ATT_PAGES
    n_keys_pad = (n_pages + 1) * PAGE_SIZE
    page_spec = lambda u: pl.BlockSpec(
        (1, 1, PAGE_SIZE * H_A, DH_A), lambda r, j, pt, u=u: (layer, pt[r, j * ATT_PAGES + u], 0, 0))
    per_req = lambda s: pl.BlockSpec((1,) + s, lambda r, j, pt: (r, 0, 0))
    rows = H_A * n_t
    return pl.pallas_call(
        functools.partial(_dsa_sample_attend_kernel, n_t=n_t, steps=steps),
        grid_spec=pltpu.PrefetchScalarGridSpec(
            num_scalar_prefetch=1,
            grid=(n_b, steps),
            in_specs=[page_spec(u) for u in range(ATT_PAGES)] * 2
            + [per_req((rows, W_A)), per_req((n_t, n_keys_pad))]
            + [pl.BlockSpec((1, 1, PAGE_SIZE * H_A, DH_A), lambda r, j, pt: (r, 0, 0, 0))] * 2,
            out_specs=per_req((n_t, W_A)),
            scratch_shapes=[pltpu.VMEM((rows, LANE), F32), pltpu.VMEM((rows, LANE), F32),
                            pltpu.VMEM((rows, DH_A), F32)]),
        out_shape=jax.ShapeDtypeStruct((n_b, n_t, W_A), F32),
        compiler_params=_cparams(("parallel", "arbitrary")),
        name="dsa_sample_attend",
    )(page_table, *([k_pool] * ATT_PAGES), *([v_pool] * ATT_PAGES), qbd, bias, knew, vnew)


def kernel(x_prompt, x_sample, mem_prompt, cache_k, cache_v, cache_idx_k, state_hgrn, cache_mem_k,
           cache_mem_v, page_table, w_in, lb_logits, hgrn_norm_g, w_mem_k, w_mem_v, w_out, ln_g, ln_b):
    n_bp, t_p, _ = x_prompt.shape
    n_bs, t_s, _ = x_sample.shape
    n_pages = page_table.shape[1]
    past = n_pages * PAGE_SIZE
    l = 0

    lb_all = jnp.cumsum(jax.nn.softmax(lb_logits.astype(F32), axis=0), axis=0)
    lb = lb_all[l][None, :]
    g_norm = hgrn_norm_g[l][None, :]
    w_t = _cast_rows(jnp.swapaxes(w_in, 1, 2), l, tr=1024)
    w_o = w_out[l].astype(BF16)
    lg, lbias = ln_g[l][None, :], ln_b[l][None, :]

    xp = x_prompt.reshape(n_bp * t_p, D_MODEL)
    z, zs = _proj(xp, w_t, tm=1024, tn=1024)
    pos_p = jnp.arange(t_p, dtype=jnp.int32)
    tabs = _rope_tables(pos_p, DH_A) + _rope_tables(pos_p, D_IDX)
    q, kf, kb, vf, vt, iqw, sgn_t, ikf, ike, iko = _prep(z, zs, tabs, tm=512)
    a_out = _dsa_prompt(q, kb, vt, iqw, sgn_t, ike, iko, n_bp, t_p, tq=256)
    s0 = jnp.zeros((1, n_bp, H_B, DK_B, DV_B), F32)
    bn, s_p = _hgrn(z, lb, g_norm, s0, 0, n_bp, t_p, chunk=128, heads=H_B, mm_dtype=BF16)
    mk_p, mv_p = _mem_kv(mem_prompt, w_mem_k[l].astype(BF16), w_mem_v[l].astype(BF16))
    m_out = _mem_attend(z, mk_p, mv_p, 0, n_bp, t_p, tq=512, reqs=1, mm_dtype=BF16)
    y_p = _merge(a_out, z, bn, m_out, xp, w_o, lg, lbias, tm=256)

    n_s = n_bs * t_s
    xs = x_sample.reshape(n_s, D_MODEL)
    z2, zs2 = _proj(xs, w_t, tm=n_s, tn=1024)
    pos_s = past + jnp.arange(t_s, dtype=jnp.int32)
    tabs_s = [jnp.tile(t, (n_bs, 1)) for t in _rope_tables(pos_s, DH_A) + _rope_tables(pos_s, D_IDX)]
    q2, kf2, kb2, vf2, vt2, iqw2, sgn_t2, ikf2, ike2, iko2 = _prep(z2, zs2, tabs_s, tm=n_s)
    nsel_s = min(TOPK_MAX, (past + t_s) // 4)
    lq = iqw2.reshape(n_bs, t_s, H_IDX, D_IDX).transpose(0, 2, 1, 3).reshape(n_bs, H_IDX * t_s, D_IDX)
    sgb = jnp.broadcast_to(
        sgn_t2.reshape(H_IDX, n_bs, t_s).transpose(1, 0, 2).reshape(n_bs, H_IDX * t_s, 1),
        (n_bs, H_IDX * t_s, PAGE_SIZE))
    pad_keys = lambda a: jnp.pad(a.reshape((n_bs, t_s) + a.shape[1:]),
                                 ((0, 0), (0, PAGE_SIZE - t_s)) + ((0, 0),) * (a.ndim - 1))
    idx_pool_t = jnp.swapaxes(cache_idx_k, 2, 3)
    bias = _dsa_sample_select(page_table, idx_pool_t, l, lq, sgb, jnp.swapaxes(pad_keys(ikf2), 1, 2),
                              n_bs, t_s, nsel_s).reshape(n_bs, t_s, -1)
    q4 = q2.reshape(n_bs, t_s, H_A, DH_A)
    eye = jnp.eye(H_A, dtype=q2.dtype)
    qbd = (q4[:, None, :, :, :] * eye[None, :, None, :, None]).reshape(n_bs, H_A * t_s, W_A)
    n_layers, n_phys = cache_k.shape[:2]
    as_rows = lambda a: a.reshape(a.shape[0], a.shape[1], PAGE_SIZE * H_A, DH_A)
    a_out2 = _dsa_sample_attend(page_table, as_rows(cache_k), as_rows(cache_v), l, qbd, bias,
                                as_rows(pad_keys(kf2)[:, None]), as_rows(pad_keys(vf2)[:, None]), n_bs, t_s)
    bn2, s_s = _hgrn(z2, lb, g_norm, state_hgrn, l, n_bs, t_s, chunk=t_s, heads=H_B, mm_dtype=F32)
    m_out2 = _mem_attend(z2, cache_mem_k, cache_mem_v, l, n_bs, t_s, tq=t_s, reqs=n_bs, mm_dtype=F32)
    y_s = _merge(a_out2.reshape(n_s, W_A), z2, bn2, m_out2, xs, w_o, lg, lbias, tm=n_s)

    return (y_p.reshape(n_bp, t_p, D_MODEL), y_s.reshape(n_bs, t_s, D_MODEL),
            kf.reshape(1, n_bp, t_p, H_A, DH_A), vf.reshape(1, n_bp, t_p, H_A, DH_A),
            ikf.reshape(1, n_bp, t_p, D_IDX), s_p, mk_p, mv_p,
            kf2.reshape(1, n_bs, t_s, H_A, DH_A), vf2.reshape(1, n_bs, t_s, H_A, DH_A),
            ikf2.reshape(1, n_bs, t_s, D_IDX), s_s)
```

```python
import functools

import numpy as np
import jax
import jax.numpy as jnp
from jax import lax
from jax.experimental import pallas as pl
from jax.experimental.pallas import tpu as pltpu

D_MODEL = 2048
DH_A = 128
H_A = 8
W_A = H_A * DH_A
H_IDX = 16
D_IDX = 64
W_IDX = H_IDX * D_IDX
TOPK_MAX = 256
DK_B = 128
DV_B = 128
H_B = 4
W_B = H_B * DV_B
DH_M = 128
H_M = 4
W_M = H_M * DH_M
N_MEM = 256
PAGE_SIZE = 128
ROPE_THETA = 500000.0
ROPE_FRAC = 4
LN_EPS = 1e-5
RMS_EPS = 1e-6
DEPTH = 1
ALPHA = (2.0 * DEPTH) ** 0.25

LANE = 128
VMEM_LIMIT = 48 * 1024 * 1024

C_AQ, C_AK, C_AV, C_AG, C_IQ = 0, 1024, 2048, 3072, 4096
C_BQ, C_BF, C_BI, C_BG, C_MQ, C_MG = 5120, 5632, 6144, 6656, 7168, 7680
N_MAIN = 8192

NEG = -1e30
Q_SCALE = DH_A ** -0.5 * 1.4426950408889634
ATT_GROUP = H_A
VT_ROWS = DH_A + 16
INT_MIN = -(2 ** 31)
BF16 = jnp.bfloat16
F32 = jnp.float32


def _dot(a, b, **kw):
    return jnp.dot(a, b, preferred_element_type=F32, **kw)


def _dot_nt(a, b):
    return lax.dot_general(a, b, (((1,), (1,)), ((), ())), preferred_element_type=F32)


def _dot_tn(a, b):
    return lax.dot_general(a, b, (((0,), (0,)), ((), ())), preferred_element_type=F32)


def _sigmoid(x):
    return 1.0 / (1.0 + jnp.exp(-x))


def _silu(x):
    return x * _sigmoid(x)


def _cparams(sem):
    return pltpu.CompilerParams(dimension_semantics=sem, vmem_limit_bytes=VMEM_LIMIT)


def _proj_kernel(x_ref, w_ref, ws_ref, o_ref, os_ref, xb_ref):
    j = pl.program_id(1)

    @pl.when(j == 0)
    def _():
        xb = x_ref[...].astype(BF16)
        xb_ref[...] = xb
        os_ref[...] = _dot_nt(xb, ws_ref[...])

    o_ref[...] = _dot_nt(xb_ref[...], w_ref[...])


W_SIDE_ROW = C_IQ + W_IDX
W_SIDE_ROWS = D_IDX + H_IDX
N_IN = N_MAIN + W_SIDE_ROWS


def _proj(x, w_t, tm, tn):
    m = x.shape[0]
    n_lo = W_SIDE_ROW // tn

    def w_rows(i, j):
        return ((j * (tn // 16) + jnp.where(j < n_lo, 0, W_SIDE_ROWS // 16)) * 16, 0)

    return pl.pallas_call(
        _proj_kernel,
        grid=(m // tm, N_MAIN // tn),
        in_specs=[pl.BlockSpec((tm, D_MODEL), lambda i, j: (i, 0)),
                  pl.BlockSpec((pl.Element(tn), pl.Element(D_MODEL)), w_rows),
                  pl.BlockSpec((pl.Element(LANE), pl.Element(D_MODEL)), lambda i, j: (W_SIDE_ROW, 0))],
        out_specs=[pl.BlockSpec((tm, tn), lambda i, j: (i, j)),
                   pl.BlockSpec((tm, LANE), lambda i, j: (i, 0))],
        out_shape=[jax.ShapeDtypeStruct((m, N_MAIN), F32),
                   jax.ShapeDtypeStruct((m, LANE), F32)],
        scratch_shapes=[pltpu.VMEM((tm, D_MODEL), BF16)],
        compiler_params=_cparams(("parallel", "arbitrary")),
        name="in_proj",
    )(x, w_t, w_t)


def _cast_kernel(x_ref, o_ref):
    o_ref[...] = x_ref[0].astype(BF16)


def _cast_rows(w, layer, tr):
    rows = w.shape[1]
    return pl.pallas_call(
        _cast_kernel,
        grid=(pl.cdiv(rows, tr),),
        in_specs=[pl.BlockSpec((1, tr, D_MODEL), lambda i: (layer, i, 0))],
        out_specs=pl.BlockSpec((tr, D_MODEL), lambda i: (i, 0)),
        out_shape=jax.ShapeDtypeStruct((rows, D_MODEL), BF16),
        compiler_params=_cparams(("parallel",)),
        name="w_cast",
    )(w)


def _swap_halves_index(shape, d):
    half = d // ROPE_FRAC // 2
    lane = lax.broadcasted_iota(jnp.int32, shape, 1)
    pos = lane & (d - 1)
    return jnp.where(pos < half, lane + half, jnp.where(pos < 2 * half, lane - half, lane))


def _rope(x, c, s, swap):
    return x * c + jnp.take_along_axis(x, swap, axis=1) * s


def _prep_kernel(aq_ref, ak_ref, av_ref, iq_ref, zs_ref, c1_ref, s1_ref, c2_ref, s2_ref,
                 q_ref, kf_ref, kb_ref, vf_ref, vt_ref, iqw_ref, sgn_ref,
                 ikf_ref, ike_ref, iko_ref):
    c1, s1, c2, s2 = c1_ref[...], s1_ref[...], c2_ref[...], s2_ref[...]
    swap_a = _swap_halves_index(c1.shape, DH_A)
    swap_i = _swap_halves_index(c2.shape, D_IDX)
    k_heads, v_heads = [], []
    for h in range(H_A):
        sl = slice(h * DH_A, (h + 1) * DH_A)
        q_ref[:, sl] = (_rope(aq_ref[:, sl], c1, s1, swap_a) * Q_SCALE).astype(BF16)
        kr = _rope(ak_ref[:, sl], c1, s1, swap_a)
        kb_ref[:, sl] = kr.astype(BF16)
        v = av_ref[:, sl]
        vt_ref[h * VT_ROWS:h * VT_ROWS + DH_A, :] = v.T.astype(BF16)
        vt_ref[h * VT_ROWS + DH_A:(h + 1) * VT_ROWS, :] = jnp.ones((VT_ROWS - DH_A, v.shape[0]), BF16)
        k_heads.append(kr)
        v_heads.append(v)
    kf_ref[...] = jnp.swapaxes(jnp.stack(k_heads, axis=0), 0, 1)
    vf_ref[...] = jnp.swapaxes(jnp.stack(v_heads, axis=0), 0, 1)

    zs = zs_ref[...]
    lane = lax.broadcasted_iota(jnp.int32, zs.shape, 1)
    ikr = _rope(zs, c2, s2, swap_i)
    ikf_ref[...] = ikr[:, :D_IDX]
    ike_ref[...] = jnp.where(lane < D_IDX, ikr, 0.0).astype(BF16)
    iko_ref[...] = jnp.where(lane >= D_IDX, pltpu.roll(ikr, D_IDX, 1), 0.0).astype(BF16)
    sgn_t = jnp.sign(pltpu.roll(zs, LANE - D_IDX, 1)).T
    sgn_ref[...] = sgn_t[:H_IDX, :]
    wscale = (D_IDX ** -0.5) * (H_IDX ** -0.5)
    for p in range(H_IDX // 2):
        sl = slice(p * LANE, (p + 1) * LANE)
        wa = jnp.abs(zs[:, D_IDX + 2 * p:D_IDX + 2 * p + 1])
        wb = jnp.abs(zs[:, D_IDX + 2 * p + 1:D_IDX + 2 * p + 2])
        wpair = jnp.where(lane < D_IDX, wa, wb) * wscale
        iqw_ref[:, sl] = (_rope(iq_ref[:, sl], c2, s2, swap_i) * wpair).astype(BF16)


def _prep(z, zs, tabs, tm):
    m = z.shape[0]
    period = tabs[0].shape[0] // tm
    zblk = lambda c: pl.BlockSpec((tm, 1024), lambda i, c=c: (i, c // 1024))
    tab = pl.BlockSpec((tm, LANE), lambda i: (i % period, 0))
    row = lambda w: pl.BlockSpec((tm, w), lambda i: (i, 0))
    col = lambda w: pl.BlockSpec((w, tm), lambda i: (0, i))
    sds = lambda w, dt: jax.ShapeDtypeStruct((m, w), dt)
    heads =pl.BlockSpec((tm, H_A, DH_A), lambda i: (i, 0, 0))
    heads_sds = jax.ShapeDtypeStruct((m, H_A, DH_A), F32)
    return pl.pallas_call(
        _prep_kernel,
        grid=(m // tm,),
        in_specs=[zblk(C_AQ), zblk(C_AK), zblk(C_AV), zblk(C_IQ), row(LANE)] + [tab] * 4,
        out_specs=[row(W_A), heads, row(W_A), heads, col(H_A * VT_ROWS), row(W_IDX), col(H_IDX),
                   row(D_IDX), row(LANE), row(LANE)],
        out_shape=[sds(W_A, BF16), heads_sds, sds(W_A, BF16), heads_sds,
                   jax.ShapeDtypeStruct((H_A * VT_ROWS, m), BF16), sds(W_IDX, BF16),
                   jax.ShapeDtypeStruct((H_IDX, m), F32),
                   sds(D_IDX, F32), sds(LANE, BF16), sds(LANE, BF16)],
        compiler_params=_cparams(("parallel",)),
        name="rope_prep",
    )(z, z, z, z, zs, *tabs)


def _rope_tables(pos, d):
    r = d // ROPE_FRAC
    half = r // 2
    n = pos.shape[0]
    inv = ROPE_THETA ** (-jnp.arange(half, dtype=F32) / half)
    ang = pos.astype(F32)[:, None] * inv[None, :]
    cos, sin = jnp.cos(ang), jnp.sin(ang)
    c = jnp.concatenate([cos, cos, jnp.ones((n, d - r), F32)], axis=1)
    s = jnp.concatenate([-sin, sin, jnp.zeros((n, d - r), F32)], axis=1)
    reps = LANE // d
    return [jnp.tile(t, (1, reps)) for t in (c, s)]


KEY_NEG_INF = -2139095041
F32_LOWEST = -3.4028234663852886e38


def _key_to_float(key):
    key = jnp.maximum(key, jnp.int32(KEY_NEG_INF))
    b = key ^ (lax.shift_right_arithmetic(key, 31) & jnp.int32(0x7FFFFFFF))
    return lax.bitcast_convert_type(b, F32)


def _lane_total(part):
    return _dot(part.astype(BF16), jnp.ones((LANE, LANE), BF16))


def _radix_select(count_ge, shape, total, nsel):
    def body(bi, carry):
        pu, cacc = carry
        bit = jnp.int32(31) - bi
        cand_u = pu | lax.shift_left(jnp.int32(1), bit)
        cnt = count_ge(_key_to_float(cand_u ^ jnp.int32(INT_MIN)))
        ok = cnt >= nsel
        return jnp.where(ok, cand_u, pu), jnp.where(ok, cnt, cacc)

    pu0 = jnp.zeros(shape, jnp.int32)
    c0 = jnp.full(shape, total, F32)
    pu, cacc = lax.fori_loop(0, 32, body, (pu0, c0))
    return _key_to_float(pu ^ jnp.int32(INT_MIN)), cacc


def _dsa_prompt_kernel(q_ref, k_ref, vt_ref, iq_ref, sgn_ref, ike_ref, iko_ref, o_ref,
                       key_ref, bias_ref, m_ref, acc_ref, *, tq, nsel):
    tk = tq
    i = pl.program_id(1)
    nch = i + 1
    sub = 8

    kidx0 = lax.broadcasted_iota(jnp.int32, (tk, tq), 0)
    qidx = i * tq + lax.broadcasted_iota(jnp.int32, (tk, tq), 1)

    def chunk_slice(c):
        return pl.ds(pl.multiple_of(c * tk, tk), tk)

    def score_body(c, carry):
        ds = chunk_slice(c)
        ke = ike_ref[ds, :]
        ko = iko_ref[ds, :]
        acc = jnp.zeros((tk, tq), F32)
        for h in range(H_IDX):
            rhs = iq_ref[:, (h // 2) * LANE:(h // 2 + 1) * LANE]
            d = _dot_nt(ke if h % 2 == 0 else ko, rhs)
            acc = acc + sgn_ref[h:h + 1, :] * jnp.maximum(d, 0.0)
        key_ref[ds, :] = jnp.where(kidx0 + c * tk <= qidx, acc, -jnp.inf)
        return carry

    lax.fori_loop(0, nch, score_body, 0)

    def count(cmp_fn):
        def hits(c):
            hit = jnp.where(cmp_fn(key_ref[chunk_slice(c), :]), 1.0, 0.0)
            return jnp.sum(hit.reshape(tk // sub, sub, tq), axis=0)

        def body(c2, parts):
            return parts[0] + hits(2 * c2), parts[1] + hits(2 * c2 + 1)

        zero = jnp.zeros((sub, tq), F32)
        p0, p1 = lax.fori_loop(0, nch // 2, body, (zero, zero))
        p0 = lax.cond(nch % 2 == 1, lambda p: p + hits(nch - 1), lambda p: p, p0)
        return jnp.sum(p0 + p1, axis=0, keepdims=True)

    total = (nch * tk).astype(F32)
    tau, c_ge = _radix_select(lambda cand: count(lambda kc: kc >= cand), (1, tq), total, float(nsel))
    tau_eff = jnp.maximum(tau, F32_LOWEST)

    def bias_body(c, carry):
        ds = chunk_slice(c)
        bias_ref[ds, :] = jnp.where(key_ref[ds, :] >= tau_eff, 0.0, NEG)
        return carry

    lax.fori_loop(0, nch, bias_body, 0)

    tie_q = jnp.logical_and(c_ge > float(nsel), tau > -jnp.inf)
    has_tie = jnp.max(jnp.where(tie_q, 1.0, 0.0)) > 0.5

    @pl.when(has_tie)
    def _():
        need = float(nsel) - count(lambda kc: kc > tau)
        lower = jnp.where(lax.broadcasted_iota(jnp.int32, (tk, tk), 1)
                          <= lax.broadcasted_iota(jnp.int32, (tk, tk), 0), 1.0, 0.0).astype(BF16)
        real = tau > -jnp.inf

        def tie_body(c, run):
            ds = chunk_slice(c)
            kc = key_ref[ds, :]
            eqf = jnp.where(kc == tau, 1.0, 0.0)
            pre = _dot(lower, eqf.astype(BF16)) + run
            keep = jnp.where(kc > tau, 1.0, jnp.where(real, eqf * jnp.where(pre <= need, 1.0, 0.0), 0.0))
            bias_ref[ds, :] = jnp.where(keep > 0.5, 0.0, NEG)
            return run + jnp.sum(eqf, axis=0, keepdims=True)

        lax.fori_loop(0, nch, tie_body, jnp.zeros((1, tq), F32))

    m_ref[...] = jnp.full(m_ref.shape, NEG, F32)
    acc_ref[...] = jnp.zeros(acc_ref.shape, F32)

    def attn_body(c, carry):
        ds = chunk_slice(c)
        bias = bias_ref[ds, :]
        for h0 in range(0, H_A, ATT_GROUP):
            group = range(h0, h0 + ATT_GROUP)
            ss = {h: _dot_nt(k_ref[ds, h * DH_A:(h + 1) * DH_A], q_ref[:, h * DH_A:(h + 1) * DH_A])
                  for h in group}
            ps, alphas = {}, {}
            for h in group:
                s = ss[h] + bias
                m_prev = m_ref[h]
                m_new = jnp.maximum(m_prev, jnp.max(s, axis=0, keepdims=True))
                alphas[h] = jnp.exp2(m_prev - m_new)
                ps[h] = jnp.exp2(s - m_new).astype(BF16)
                m_ref[h] = m_new
            for h in group:
                acc_ref[h] = alphas[h] * acc_ref[h] + _dot(vt_ref[h * VT_ROWS:(h + 1) * VT_ROWS, ds], ps[h])
        return carry

    lax.fori_loop(0, nch, attn_body, 0)
    for h in range(H_A):
        acc = acc_ref[h]
        o_ref[:, h * DH_A:(h + 1) * DH_A] = (acc[:DH_A] / acc[DH_A:DH_A + 1]).T


def _dsa_prompt(q, k, vt, iqw, sgn_t, ike, iko, n_b, n_t, tq):
    nsel = min(TOPK_MAX, n_t // 4)
    nq = n_t // tq
    qblk = lambda w: pl.BlockSpec((tq, w), lambda b, i: (b * nq + i, 0))
    full = lambda w: pl.BlockSpec((n_t, w), lambda b, i: (b, 0))
    return pl.pallas_call(
        functools.partial(_dsa_prompt_kernel, tq=tq, nsel=nsel),
        grid=(n_b, nq),
        in_specs=[qblk(W_A), full(W_A), pl.BlockSpec((H_A * VT_ROWS, n_t), lambda b, i: (0, b)), qblk(W_IDX),
                  pl.BlockSpec((H_IDX, tq), lambda b, i: (0, b * nq + i)), full(LANE), full(LANE)],
        out_specs=qblk(W_A),
        out_shape=jax.ShapeDtypeStruct((n_b * n_t, W_A), F32),
        scratch_shapes=[pltpu.VMEM((n_t, tq), F32),
                        pltpu.VMEM((n_t, tq), F32),
                        pltpu.VMEM((H_A, 1, tq), F32),
                        pltpu.VMEM((H_A, VT_ROWS, tq), F32)],
        compiler_params=_cparams(("parallel", "arbitrary")),
        name="dsa_prompt",
    )(q, k, vt, iqw, sgn_t, ike, iko)


HGRN_BLOCK = 8


def _hgrn_levels(chunk):
    levels, m = [], HGRN_BLOCK
    while m < chunk:
        levels.append(m)
        m *= 2
    return levels


def _hgrn_masks(chunk):
    row, col = np.indices((chunk, chunk))
    masks = []
    for m in _hgrn_levels(chunk):
        same_pair = (row // (2 * m)) == (col // (2 * m))
        masks.append(same_pair & (row % (2 * m) >= m) & (col % (2 * m) < m))
    for j in range(HGRN_BLOCK):
        masks.append((row - col == j) & (row % HGRN_BLOCK >= j))
    masks.append((row // HGRN_BLOCK == col // HGRN_BLOCK) & (col <= row))
    return np.stack(masks).astype(np.float32)


HGRN_SAFE_BLOCK_DECAY = 100.0


def _cumsum_rows(lower, x, mm_dtype):
    if mm_dtype == F32:
        return _dot(lower, x, precision=lax.Precision.HIGHEST)
    hi = x.astype(BF16)
    r1 = x - hi.astype(F32)
    mid = r1.astype(BF16)
    lo = (r1 - mid.astype(F32)).astype(BF16)
    lb16 = lower.astype(BF16)
    return _dot(lb16, hi) + _dot(lb16, mid) + _dot(lb16, lo)


def _hgrn_kernel(bq_ref, bf_ref, bi_ref, lb_ref, g_ref, s0_ref, mask_ref, o_ref, s_out_ref, st_ref,
                 *, t_blk, chunk, n_seq, heads, mm_dtype):
    c_ = chunk
    tt = pl.program_id(1)
    g = g_ref[...]

    @pl.when(tt == 0)
    def _():
        for bb in range(n_seq):
            for hh in range(heads):
                st_ref[bb * heads + hh] = s0_ref[0, bb, hh].T

    row = lax.broadcasted_iota(jnp.int32, (c_, c_), 0)
    col = lax.broadcasted_iota(jnp.int32, (c_, c_), 1)
    lower = jnp.where(col <= row, 1.0, 0.0)
    levels = _hgrn_levels(c_)
    n_blk = c_ // HGRN_BLOCK

    def decay_body(ci, worst):
        ds = pl.ds(pl.multiple_of(ci * c_, c_), c_)
        for bb in range(n_seq):
            for hh in range(heads):
                hs = slice(hh * DK_B, (hh + 1) * DK_B)
                lb = lb_ref[:, hs]
                lg = jnp.log2(lb + (1.0 - lb) * _sigmoid(bf_ref[bb, ds, hs]))
                blk = jnp.sum(lg.reshape(n_blk, HGRN_BLOCK, DK_B), axis=1)
                worst = jnp.minimum(worst, jnp.min(blk, axis=0, keepdims=True))
        return worst

    worst = lax.fori_loop(0, t_blk // c_, decay_body, jnp.zeros((1, DK_B), F32))
    blocks_safe = jnp.min(worst) > -HGRN_SAFE_BLOCK_DECAY

    def head_chunk(bb, hh, ds, one_matmul_blocks):
        hs = slice(hh * DK_B, (hh + 1) * DK_B)
        si = bb * heads + hh
        lb = lb_ref[:, hs]
        bq = bq_ref[bb, ds, hs]
        qs = bq * _sigmoid(bq)
        f = lb + (1.0 - lb) * _sigmoid(bf_ref[bb, ds, hs])
        kk = 1.0 - f
        v = bi_ref[bb, ds, hs]
        gcum = _cumsum_rows(lower, jnp.log2(f), mm_dtype)
        a = jnp.zeros((c_, c_), F32)
        for li, m in enumerate(levels):
            g3 = gcum.reshape(c_ // (2 * m), 2 * m, DK_B)
            ref = jnp.broadcast_to(g3[:, m - 1:m, :], g3.shape).reshape(c_, DK_B)
            qh = qs * jnp.exp2(jnp.minimum(gcum - ref, 0.0))
            kh = kk * jnp.exp2(jnp.minimum(ref - gcum, 0.0))
            a = a + _dot_nt(qh.astype(mm_dtype), kh.astype(mm_dtype)) * mask_ref[li]
        if one_matmul_blocks:
            g3 = gcum.reshape(n_blk, HGRN_BLOCK, DK_B)
            first = jnp.broadcast_to(g3[:, 0:1, :], g3.shape).reshape(c_, DK_B)
            qh = qs * jnp.exp2(gcum - first)
            kh = kk * jnp.exp2(first - gcum)
            a = a + _dot_nt(qh.astype(mm_dtype), kh.astype(mm_dtype)) * mask_ref[len(levels) + HGRN_BLOCK]
        else:
            for j in range(HGRN_BLOCK):
                kj = kk if j == 0 else pltpu.roll(kk, j, 0)
                gj = gcum if j == 0 else pltpu.roll(gcum, j, 0)
                e = jnp.exp2(jnp.minimum(gcum - gj, 0.0))
                aj = jnp.sum(qs * kj * e, axis=1, keepdims=True)
                a = a + aj * mask_ref[len(levels) + j]
        st = st_ref[si]
        o = (_dot_nt((qs * jnp.exp2(gcum)).astype(mm_dtype), st.astype(mm_dtype))
             + _dot(a.astype(mm_dtype), v.astype(mm_dtype)))
        glast = gcum[c_ - 1:c_, :]
        kd = kk * jnp.exp2(glast - gcum)
        st_ref[si] = jnp.exp2(glast) * st + _dot_tn(v.astype(mm_dtype), kd.astype(mm_dtype))
        o_ref[bb, ds, hs] = o * lax.rsqrt(jnp.mean(o * o, axis=1, keepdims=True) + RMS_EPS) * g

    def run(one_matmul_blocks):
        def chunk_body(ci, carry):
            ds = pl.ds(pl.multiple_of(ci * c_, c_), c_)
            for bb in range(n_seq):
                for hh in range(heads):
                    head_chunk(bb, hh, ds, one_matmul_blocks)
            return carry
        lax.fori_loop(0, t_blk // c_, chunk_body, 0)

    @pl.when(blocks_safe)
    def _():
        run(True)

    @pl.when(jnp.logical_not(blocks_safe))
    def _():
        run(False)

    @pl.when(tt == pl.num_programs(1) - 1)
    def _():
        for bb in range(n_seq):
            for hh in range(heads):
                s_out_ref[0, bb, hh] = st_ref[bb * heads + hh].T


def _hgrn(z, lb, g, s0, layer, n_b, n_t, chunk, n_seq, t_blk, mm_dtype):
    heads = H_B
    z3 = z.reshape(n_b, n_t, z.shape[1])
    zcol = lambda c: pl.BlockSpec((n_seq, t_blk, W_B), lambda i, tt, c=c: (i, tt, c // W_B))
    sblk = lambda lyr: pl.BlockSpec((1, n_seq, heads, DK_B, DV_B), lambda i, tt: (lyr, i, 0, 0, 0))
    masks = jnp.asarray(_hgrn_masks(chunk))
    bn, s_new = pl.pallas_call(
        functools.partial(_hgrn_kernel, t_blk=t_blk, chunk=chunk, n_seq=n_seq, heads=heads, mm_dtype=mm_dtype),
        grid=(n_b // n_seq, n_t // t_blk),
        in_specs=[zcol(C_BQ), zcol(C_BF), zcol(C_BI),
                  pl.BlockSpec((1, W_B), lambda i, tt: (0, 0)),
                  pl.BlockSpec((1, DV_B), lambda i, tt: (0, 0)),
                  sblk(layer),
                  pl.BlockSpec(masks.shape, lambda i, tt: (0, 0, 0))],
        out_specs=[pl.BlockSpec((n_seq, t_blk, W_B), lambda i, tt: (i, tt, 0)), sblk(0)],
        out_shape=[jax.ShapeDtypeStruct((n_b, n_t, W_B), F32),
                   jax.ShapeDtypeStruct((1, n_b, H_B, DK_B, DV_B), F32)],
        scratch_shapes=[pltpu.VMEM((n_seq * heads, DV_B, DK_B), F32)],
        compiler_params=_cparams(("parallel", "arbitrary")),
        name="hgrn2",
    )(z3, z3, z3, lb, g, s0, masks)
    return bn.reshape(n_b * n_t, W_B), s_new


def _mem_kv_kernel(x_ref, wk_ref, wv_ref, mk_ref, mv_ref):
    x = x_ref[0].astype(BF16)
    mk = _dot(x, wk_ref[...])
    mv = _dot(x, wv_ref[...])
    heads = lambda a: jnp.stack([a[:, h * DH_M:(h + 1) * DH_M] for h in range(H_M)], axis=0)
    mk_ref[0, 0] = jnp.swapaxes(heads(mk), 0, 1)
    mv_ref[0, 0] = jnp.swapaxes(heads(mv), 0, 1)


def _mem_kv(mem, wk, wv):
    n_b = mem.shape[0]
    wspec = pl.BlockSpec((D_MODEL, W_M), lambda b: (0, 0))
    ospec = pl.BlockSpec((1, 1, N_MEM, H_M, DH_M), lambda b: (0, b, 0, 0, 0))
    sds = jax.ShapeDtypeStruct((1, n_b, N_MEM, H_M, DH_M), F32)
    return pl.pallas_call(
        _mem_kv_kernel,
        grid=(n_b,),
        in_specs=[pl.BlockSpec((1, N_MEM, D_MODEL), lambda b: (b, 0, 0)), wspec, wspec],
        out_specs=[ospec, ospec],
        out_shape=[sds, sds],
        compiler_params=_cparams(("parallel",)),
        name="mem_kv",
    )(mem, wk, wv)


def _mem_kernel(q_ref, mk_ref, mv_ref, o_ref, *, mm_dtype):
    mk = jnp.swapaxes(mk_ref[0, 0], 0, 1).astype(mm_dtype)
    mv = jnp.swapaxes(mv_ref[0, 0], 0, 1).astype(mm_dtype)
    for h in range(H_M):
        hs = slice(h * DH_M, (h + 1) * DH_M)
        q = (q_ref[:, hs] * (DH_M ** -0.5)).astype(mm_dtype)
        s = _dot_nt(q, mk[h])
        p = jnp.exp(s - jnp.max(s, axis=1, keepdims=True))
        l = jnp.sum(p, axis=1, keepdims=True)
        o_ref[:, hs] = _dot(p.astype(mm_dtype), mv[h]) / l


def _mem_attend(z, mk, mv, layer, n_b, n_t, tq, mm_dtype):
    nq = n_t // tq
    kv = pl.BlockSpec((1, 1, N_MEM, H_M, DH_M), lambda b, i: (layer, b, 0, 0, 0))
    return pl.pallas_call(
        functools.partial(_mem_kernel, mm_dtype=mm_dtype),
        grid=(n_b, nq),
        in_specs=[pl.BlockSpec((tq, W_M), lambda b, i: (b * nq + i, C_MQ // W_M)), kv, kv],
        out_specs=pl.BlockSpec((tq, W_M), lambda b, i: (b * nq + i, 0)),
        out_shape=jax.ShapeDtypeStruct((n_b * n_t, W_M), F32),
        compiler_params=_cparams(("parallel", "parallel")),
        name="mem_attend",
    )(z, mk, mv)


def _merge_kernel(a_ref, ag_ref, bn_ref, bg_ref, mo_ref, mg_ref, h_ref, wo_ref, lg_ref, lbias_ref, o_ref):
    a = (a_ref[...] * _silu(ag_ref[...])).astype(BF16)
    b = (bn_ref[...] * _silu(bg_ref[...])).astype(BF16)
    m = (mo_ref[...] * _silu(mg_ref[...])).astype(BF16)
    y = (_dot(a, wo_ref[0:W_A, :]) + _dot(b, wo_ref[W_A:W_A + W_B, :])
         + _dot(m, wo_ref[W_A + W_B:D_MODEL, :]))
    r = ALPHA * h_ref[...] + y
    xc = r - jnp.mean(r, axis=1, keepdims=True)
    var = jnp.mean(xc * xc, axis=1, keepdims=True)
    o_ref[...] = xc * lax.rsqrt(var + LN_EPS) * lg_ref[...] + lbias_ref[...]


def _merge(a_out, z, bn, m_out, h, w_out, ln_g, ln_b, tm):
    m = h.shape[0]
    row = lambda w: pl.BlockSpec((tm, w), lambda i: (i, 0))
    zcol = lambda c, w: pl.BlockSpec((tm, w), lambda i, c=c, w=w: (i, c // w))
    const = lambda s: pl.BlockSpec(s, lambda i: (0, 0), pipeline_mode=pl.Buffered(1))
    return pl.pallas_call(
        _merge_kernel,
        grid=(m // tm,),
        in_specs=[row(W_A), zcol(C_AG, W_A), row(W_B), zcol(C_BG, W_B), row(W_M), zcol(C_MG, W_M),
                  row(D_MODEL), const((D_MODEL, D_MODEL)), const((1, D_MODEL)), const((1, D_MODEL))],
        out_specs=row(D_MODEL),
        out_shape=jax.ShapeDtypeStruct((m, D_MODEL), F32),
        compiler_params=_cparams(("parallel",)),
        name="merge",
    )(a_out, z, bn, z, m_out, z, h, w_out, ln_g, ln_b)


PAGES_PER_STEP = 8


def _dsa_sample_select_kernel(pt_ref, pool_ref, lq_ref, sg_ref, iknew_ref, bias_ref, key_ref, page_buf, sem,
                              *, layer, n_t, n_pages, nsel):
    r = pl.program_id(0)
    slot = r % 2
    n_keys_pad = (n_pages + 1) * PAGE_SIZE

    def page_copy(req, p, s):
        return pltpu.make_async_copy(pool_ref.at[layer, pt_ref[req, p]], page_buf.at[s, p], sem.at[s])

    def for_pages(fn):
        def body(p, carry):
            fn(p)
            return carry
        lax.fori_loop(0, n_pages, body, 0)

    @pl.when(r == 0)
    def _():
        for_pages(lambda p: page_copy(0, p, 0).start())

    @pl.when(r + 1 < pl.num_programs(0))
    def _():
        for_pages(lambda p: page_copy(r + 1, p, 1 - slot).start())

    for_pages(lambda p: page_copy(r, p, slot).wait())

    lq = lq_ref[0]
    sg = sg_ref[0]

    def score(ik_t):
        d = _dot(lq, ik_t.astype(BF16))
        r = jnp.maximum(d, 0.0) * jnp.concatenate([sg] * (ik_t.shape[1] // PAGE_SIZE), axis=1)
        s = r[0:n_t]
        for h in range(1, H_IDX):
            s = s + r[h * n_t:(h + 1) * n_t]
        return s

    width = PAGES_PER_STEP * PAGE_SIZE

    rows = pl.ds(pl.multiple_of(r * n_t, n_t), n_t)

    def group_body(gi, carry):
        base = gi * PAGES_PER_STEP
        ik_step = jnp.concatenate([page_buf[slot, base + u] for u in range(PAGES_PER_STEP)], axis=1)
        key_ref[rows, pl.ds(pl.multiple_of(gi * width, width), width)] = score(ik_step)
        return carry

    lax.fori_loop(0, n_pages // PAGES_PER_STEP, group_body, 0)

    past = n_pages * PAGE_SIZE
    s_new = score(iknew_ref[0])
    t_i = lax.broadcasted_iota(jnp.int32, (n_t, PAGE_SIZE), 0)
    k_i = lax.broadcasted_iota(jnp.int32, (n_t, PAGE_SIZE), 1)
    key_ref[rows, past:n_keys_pad] = jnp.where(k_i <= t_i, s_new, -jnp.inf)

    @pl.when(r == pl.num_programs(0) - 1)
    def _():
        n_rows = key_ref.shape[0]
        n_tiles = n_keys_pad // LANE

        def count_ge(cand):
            part = jnp.zeros((n_rows, LANE), F32)
            for c in range(n_tiles):
                part = part + jnp.where(key_ref[:, c * LANE:(c + 1) * LANE] >= cand, 1.0, 0.0)
            return _lane_total(part)

        tau_r, c_ge_r = _radix_select(count_ge, (n_rows, LANE), float(n_keys_pad), float(nsel))
        tau = tau_r[:, 0:1]
        tau_eff = jnp.maximum(tau, F32_LOWEST)
        bias_ref[...] = jnp.where(key_ref[...] >= tau_eff, 0.0, NEG)

        tie_rows = jnp.logical_and(c_ge_r > float(nsel), tau_r > -jnp.inf)
        has_tie = jnp.max(jnp.where(tie_rows, 1.0, 0.0)) > 0.5

        @pl.when(has_tie)
        def _():
            c_gt = jnp.sum(jnp.where(key_ref[...] > tau, 1.0, 0.0), axis=1, keepdims=True)
            need = float(nsel) - c_gt
            ri = lax.broadcasted_iota(jnp.int32, (PAGE_SIZE, PAGE_SIZE), 0)
            ci = lax.broadcasted_iota(jnp.int32, (PAGE_SIZE, PAGE_SIZE), 1)
            upper = jnp.where(ri <= ci, 1.0, 0.0)
            real = tau > -jnp.inf

            def tie_body(c, run):
                ds = pl.ds(pl.multiple_of(c * PAGE_SIZE, PAGE_SIZE), PAGE_SIZE)
                kc = key_ref[:, ds]
                eqf = jnp.where(kc == tau, 1.0, 0.0)
                pre = _dot(eqf, upper) + run
                keep = jnp.where(kc > tau, 1.0, jnp.where(real, eqf * jnp.where(pre <= need, 1.0, 0.0), 0.0))
                bias_ref[:, ds] = jnp.where(keep > 0.5, 0.0, NEG)
                return run + jnp.sum(eqf, axis=1, keepdims=True)

            lax.fori_loop(0, n_keys_pad // PAGE_SIZE, tie_body, jnp.zeros((n_rows, 1), F32))


def _dsa_sample_select(page_table, idx_pool, layer, lq, sgb, iknew, n_b, n_t, nsel):
    n_pages = page_table.shape[1]
    n_keys_pad = (n_pages + 1) * PAGE_SIZE
    per_req = lambda s: pl.BlockSpec((1,) + s, lambda r, pt: (r, 0, 0))
    return pl.pallas_call(
        functools.partial(_dsa_sample_select_kernel, layer=layer, n_t=n_t, n_pages=n_pages, nsel=nsel),
        grid_spec=pltpu.PrefetchScalarGridSpec(
            num_scalar_prefetch=1,
            grid=(n_b,),
            in_specs=[pl.BlockSpec(memory_space=pl.ANY),
                      per_req((H_IDX * n_t, D_IDX)), per_req((H_IDX * n_t, PAGE_SIZE)), per_req((D_IDX, PAGE_SIZE))],
            out_specs=pl.BlockSpec((n_b * n_t, n_keys_pad), lambda r, pt: (0, 0)),
            scratch_shapes=[pltpu.VMEM((n_b * n_t, n_keys_pad), F32),
                            pltpu.VMEM((2, n_pages, D_IDX, PAGE_SIZE), F32),
                            pltpu.SemaphoreType.DMA((2,))]),
        out_shape=jax.ShapeDtypeStruct((n_b * n_t, n_keys_pad), F32),
        compiler_params=_cparams(("arbitrary",)),
        name="dsa_sample_select",
    )(page_table, idx_pool, lq, sgb, iknew)


ATT_PAGES = 16


def _dsa_sample_attend_kernel(pt_ref, *refs, n_t, steps):
    kp = refs[:ATT_PAGES]
    vp = refs[ATT_PAGES:2 * ATT_PAGES]
    qbd_ref, bias_ref, knew_ref, vnew_ref, o_ref, m_ref, l_ref, acc_ref = refs[2 * ATT_PAGES:]
    j = pl.program_id(1)
    rows = H_A * n_t

    @pl.when(j == 0)
    def _():
        m_ref[...] = jnp.full(m_ref.shape, NEG, F32)
        l_ref[...] = jnp.zeros(l_ref.shape, F32)
        acc_ref[...] = jnp.zeros(acc_ref.shape, F32)

    qbd = qbd_ref[0]

    def update(kblk, vblk, bias):
        n = kblk.shape[0]
        s = _dot_nt(qbd, kblk) + jnp.concatenate([bias] * H_A, axis=0)
        m_prev = m_ref[...]
        m_new = jnp.maximum(m_prev, jnp.max(s, axis=1, keepdims=True))
        alpha = jnp.exp2(m_prev - m_new)
        p = jnp.exp2(s - jnp.concatenate([m_new] * (n // LANE), axis=1))
        l_ref[...] = alpha * l_ref[...] + jnp.sum(p, axis=1, keepdims=True)
        pv = _dot(p.astype(BF16), vblk)
        diag = jnp.concatenate(
            [pv[h * n_t:(h + 1) * n_t, h * DH_A:(h + 1) * DH_A] for h in range(H_A)], axis=0)
        acc_ref[...] = alpha * acc_ref[...] + diag
        m_ref[...] = m_new

    def page2d(ref):
        return jnp.concatenate(
            [ref[0, 0, pl.ds(h, PAGE_SIZE, stride=H_A), :] for h in range(H_A)], axis=1).astype(BF16)

    kblk = jnp.concatenate([page2d(kp[u]) for u in range(ATT_PAGES)], axis=0)
    vblk = jnp.concatenate([page2d(vp[u]) for u in range(ATT_PAGES)], axis=0)
    width = ATT_PAGES * PAGE_SIZE
    off = pl.multiple_of(j * width, width)
    update(kblk, vblk, bias_ref[0, :, pl.ds(off, width)])

    @pl.when(j == steps - 1)
    def _():
        past = steps * width
        update(page2d(knew_ref), page2d(vnew_ref), bias_ref[0, :, past:past + PAGE_SIZE])
        o = acc_ref[...] / l_ref[...]
        o_ref[0] = jnp.concatenate([o[h * n_t:(h + 1) * n_t] for h in range(H_A)], axis=1)


def _dsa_sample_attend(page_table, k_pool, v_pool, layer, qbd, bias, knew, vnew, n_b, n_t):
    n_pages = page_table.shape[1]
    steps = n_pages // ATT_PAGES
    n_keys_pad = (n_pages + 1) * PAGE_SIZE
    page_spec = lambda u: pl.BlockSpec(
        (1, 1, PAGE_SIZE * H_A, DH_A), lambda r, j, pt, u=u: (layer, pt[r, j * ATT_PAGES + u], 0, 0))
    per_req = lambda s: pl.BlockSpec((1,) + s, lambda r, j, pt: (r, 0, 0))
    rows = H_A * n_t
    return pl.pallas_call(
        functools.partial(_dsa_sample_attend_kernel, n_t=n_t, steps=steps),
        grid_spec=pltpu.PrefetchScalarGridSpec(
            num_scalar_prefetch=1,
            grid=(n_b, steps),
            in_specs=[page_spec(u) for u in range(ATT_PAGES)] * 2
            + [per_req((rows, W_A)), per_req((n_t, n_keys_pad))]
            + [pl.BlockSpec((1, 1, PAGE_SIZE * H_A, DH_A), lambda r, j, pt: (r, 0, 0, 0))] * 2,
            out_specs=per_req((n_t, W_A)),
            scratch_shapes=[pltpu.VMEM((rows, LANE), F32), pltpu.VMEM((rows, LANE), F32),
                            pltpu.VMEM((rows, DH_A), F32)]),
        out_shape=jax.ShapeDtypeStruct((n_b, n_t, W_A), F32),
        compiler_params=_cparams(("parallel", "arbitrary")),
        name="dsa_sample_attend",
    )(page_table, *([k_pool] * ATT_PAGES), *([v_pool] * ATT_PAGES), qbd, bias, knew, vnew)


def kernel(x_prompt, x_sample, mem_prompt, cache_k, cache_v, cache_idx_k, state_hgrn, cache_mem_k,
           cache_mem_v, page_table, w_in, lb_logits, hgrn_norm_g, w_mem_k, w_mem_v, w_out, ln_g, ln_b):
    n_bp, t_p, _ = x_prompt.shape
    n_bs, t_s, _ = x_sample.shape
    n_pages = page_table.shape[1]
    past = n_pages * PAGE_SIZE
    l = 0

    lb_all = jnp.cumsum(jax.nn.softmax(lb_logits.astype(F32), axis=0), axis=0)
    lb = lb_all[l][None, :]
    g_norm = hgrn_norm_g[l][None, :]
    w_t = _cast_rows(jnp.swapaxes(w_in, 1, 2), l, tr=1024)
    w_o = w_out[l].astype(BF16)
    lg, lbias = ln_g[l][None, :], ln_b[l][None, :]

    xp = x_prompt.reshape(n_bp * t_p, D_MODEL)
    z, zs = _proj(xp, w_t, tm=1024, tn=1024)
    pos_p = jnp.arange(t_p, dtype=jnp.int32)
    tabs = _rope_tables(pos_p, DH_A) + _rope_tables(pos_p, D_IDX)
    q, kf, kb, vf, vt, iqw, sgn_t, ikf, ike, iko = _prep(z, zs, tabs, tm=256)
    a_out = _dsa_prompt(q, kb, vt, iqw, sgn_t, ike, iko, n_bp, t_p, tq=256)
    s0 = jnp.zeros((1, n_bp, H_B, DK_B, DV_B), F32)
    bn, s_p = _hgrn(z, lb, g_norm, s0, 0, n_bp, t_p, chunk=128, n_seq=2, t_blk=t_p // 2, mm_dtype=BF16)
    mk_p, mv_p = _mem_kv(mem_prompt, w_mem_k[l].astype(BF16), w_mem_v[l].astype(BF16))
    m_out = _mem_attend(z, mk_p, mv_p, 0, n_bp, t_p, tq=512, mm_dtype=BF16)
    y_p = _merge(a_out, z, bn, m_out, xp, w_o, lg, lbias, tm=256)

    n_s = n_bs * t_s
    xs = x_sample.reshape(n_s, D_MODEL)
    z2, zs2 = _proj(xs, w_t, tm=n_s, tn=1024)
    pos_s = past + jnp.arange(t_s, dtype=jnp.int32)
    tabs_s = [jnp.tile(t, (n_bs, 1)) for t in _rope_tables(pos_s, DH_A) + _rope_tables(pos_s, D_IDX)]
    q2, kf2, kb2, vf2, vt2, iqw2, sgn_t2, ikf2, ike2, iko2 = _prep(z2, zs2, tabs_s, tm=n_s)
    nsel_s = min(TOPK_MAX, (past + t_s) // 4)
    lq = iqw2.reshape(n_bs, t_s, H_IDX, D_IDX).transpose(0, 2, 1, 3).reshape(n_bs, H_IDX * t_s, D_IDX)
    sgb = jnp.broadcast_to(
        sgn_t2.reshape(H_IDX, n_bs, t_s).transpose(1, 0, 2).reshape(n_bs, H_IDX * t_s, 1),
        (n_bs, H_IDX * t_s, PAGE_SIZE))
    pad_keys = lambda a: jnp.pad(a.reshape((n_bs, t_s) + a.shape[1:]),
                                 ((0, 0), (0, PAGE_SIZE - t_s)) + ((0, 0),) * (a.ndim - 1))
    idx_pool_t = jnp.swapaxes(cache_idx_k, 2, 3)
    bias = _dsa_sample_select(page_table, idx_pool_t, l, lq, sgb, jnp.swapaxes(pad_keys(ikf2), 1, 2),
                              n_bs, t_s, nsel_s).reshape(n_bs, t_s, -1)
    q4 = q2.reshape(n_bs, t_s, H_A, DH_A)
    eye = jnp.eye(H_A, dtype=q2.dtype)
    qbd = (q4[:, None, :, :, :] * eye[None, :, None, :, None]).reshape(n_bs, H_A * t_s, W_A)
    n_layers, n_phys = cache_k.shape[:2]
    as_rows = lambda a: a.reshape(a.shape[0], a.shape[1], PAGE_SIZE * H_A, DH_A)
    a_out2 = _dsa_sample_attend(page_table, as_rows(cache_k), as_rows(cache_v), l, qbd, bias,
                                as_rows(pad_keys(kf2)[:, None]), as_rows(pad_keys(vf2)[:, None]), n_bs, t_s)
    bn2, s_s = _hgrn(z2, lb, g_norm, state_hgrn, l, n_bs, t_s, chunk=t_s, n_seq=2, t_blk=t_s, mm_dtype=F32)
    m_out2 = _mem_attend(z2, cache_mem_k, cache_mem_v, l, n_bs, t_s, tq=t_s, mm_dtype=F32)
    y_s = _merge(a_out2.reshape(n_s, W_A), z2, bn2, m_out2, xs, w_o, lg, lbias, tm=n_s)

    return (y_p.reshape(n_bp, t_p, D_MODEL), y_s.reshape(n_bs, t_s, D_MODEL),
            kf.reshape(1, n_bp, t_p, H_A, DH_A), vf.reshape(1, n_bp, t_p, H_A, DH_A),
            ikf.reshape(1, n_bp, t_p, D_IDX), s_p, mk_p, mv_p,
            kf2.reshape(1, n_bs, t_s, H_A, DH_A), vf2.reshape(1, n_bs, t_s, H_A, DH_A),
            ikf2.reshape(1, n_bs, t_s, D_IDX), s_s)
```

```python
import functools

import numpy as np
import jax
import jax.numpy as jnp
from jax import lax
from jax.experimental import pallas as pl
from jax.experimental.pallas import tpu as pltpu

D_MODEL = 2048
DH_A = 128
H_A = 8
W_A = H_A * DH_A
H_IDX = 16
D_IDX = 64
W_IDX = H_IDX * D_IDX
TOPK_MAX = 256
DK_B = 128
DV_B = 128
H_B = 4
W_B = H_B * DV_B
DH_M = 128
H_M = 4
W_M = H_M * DH_M
N_MEM = 256
PAGE_SIZE = 128
ROPE_THETA = 500000.0
ROPE_FRAC = 4
LN_EPS = 1e-5
RMS_EPS = 1e-6
DEPTH = 1
ALPHA = (2.0 * DEPTH) ** 0.25

LANE = 128
VMEM_LIMIT = 48 * 1024 * 1024

C_AQ, C_AK, C_AV, C_AG, C_IQ = 0, 1024, 2048, 3072, 4096
C_BQ, C_BF, C_BI, C_BG, C_MQ, C_MG = 5120, 5632, 6144, 6656, 7168, 7680
N_MAIN = 8192

NEG = -1e30
Q_SCALE = DH_A ** -0.5 * 1.4426950408889634
ATT_GROUP = H_A
VT_ROWS = DH_A + 16
INT_MIN = -(2 ** 31)
BF16 = jnp.bfloat16
F32 = jnp.float32


def _dot(a, b, **kw):
    return jnp.dot(a, b, preferred_element_type=F32, **kw)


def _dot_nt(a, b):
    return lax.dot_general(a, b, (((1,), (1,)), ((), ())), preferred_element_type=F32)


def _dot_tn(a, b):
    return lax.dot_general(a, b, (((0,), (0,)), ((), ())), preferred_element_type=F32)


def _sigmoid(x):
    return 1.0 / (1.0 + jnp.exp(-x))


def _silu(x):
    return x * _sigmoid(x)


def _cparams(sem):
    return pltpu.CompilerParams(dimension_semantics=sem, vmem_limit_bytes=VMEM_LIMIT)


def _proj_kernel(x_ref, w_ref, ws_ref, o_ref, os_ref, xb_ref):
    j = pl.program_id(1)

    @pl.when(j == 0)
    def _():
        xb = x_ref[...].astype(BF16)
        xb_ref[...] = xb
        os_ref[...] = _dot_nt(xb, ws_ref[...])

    o_ref[...] = _dot_nt(xb_ref[...], w_ref[...])


W_SIDE_ROW = C_IQ + W_IDX
W_SIDE_ROWS = D_IDX + H_IDX
N_IN = N_MAIN + W_SIDE_ROWS


def _proj(x, w_t, tm, tn):
    m = x.shape[0]
    n_lo = W_SIDE_ROW // tn

    def w_rows(i, j):
        return ((j * (tn // 16) + jnp.where(j < n_lo, 0, W_SIDE_ROWS // 16)) * 16, 0)

    return pl.pallas_call(
        _proj_kernel,
        grid=(m // tm, N_MAIN // tn),
        in_specs=[pl.BlockSpec((tm, D_MODEL), lambda i, j: (i, 0)),
                  pl.BlockSpec((pl.Element(tn), pl.Element(D_MODEL)), w_rows),
                  pl.BlockSpec((pl.Element(LANE), pl.Element(D_MODEL)), lambda i, j: (W_SIDE_ROW, 0))],
        out_specs=[pl.BlockSpec((tm, tn), lambda i, j: (i, j)),
                   pl.BlockSpec((tm, LANE), lambda i, j: (i, 0))],
        out_shape=[jax.ShapeDtypeStruct((m, N_MAIN), F32),
                   jax.ShapeDtypeStruct((m, LANE), F32)],
        scratch_shapes=[pltpu.VMEM((tm, D_MODEL), BF16)],
        compiler_params=_cparams(("parallel", "arbitrary")),
        name="in_proj",
    )(x, w_t, w_t)


def _cast_kernel(x_ref, o_ref):
    o_ref[...] = x_ref[0].astype(BF16)


def _cast_rows(w, layer, tr):
    rows = w.shape[1]
    return pl.pallas_call(
        _cast_kernel,
        grid=(pl.cdiv(rows, tr),),
        in_specs=[pl.BlockSpec((1, tr, D_MODEL), lambda i: (layer, i, 0))],
        out_specs=pl.BlockSpec((tr, D_MODEL), lambda i: (i, 0)),
        out_shape=jax.ShapeDtypeStruct((rows, D_MODEL), BF16),
        compiler_params=_cparams(("parallel",)),
        name="w_cast",
    )(w)


def _swap_halves_index(shape, d):
    half = d // ROPE_FRAC // 2
    lane = lax.broadcasted_iota(jnp.int32, shape, 1)
    pos = lane & (d - 1)
    return jnp.where(pos < half, lane + half, jnp.where(pos < 2 * half, lane - half, lane))


def _rope(x, c, s, swap):
    return x * c + jnp.take_along_axis(x, swap, axis=1) * s


def _prep_kernel(aq_ref, ak_ref, av_ref, iq_ref, zs_ref, c1_ref, s1_ref, c2_ref, s2_ref,
                 q_ref, kf_ref, kb_ref, vf_ref, vt_ref, iqw_ref, sgn_ref,
                 ikf_ref, ike_ref, iko_ref):
    c1, s1, c2, s2 = c1_ref[...], s1_ref[...], c2_ref[...], s2_ref[...]
    swap_a = _swap_halves_index(c1.shape, DH_A)
    swap_i = _swap_halves_index(c2.shape, D_IDX)
    k_heads, v_heads = [], []
    for h in range(H_A):
        sl = slice(h * DH_A, (h + 1) * DH_A)
        q_ref[:, sl] = (_rope(aq_ref[:, sl], c1, s1, swap_a) * Q_SCALE).astype(BF16)
        kr = _rope(ak_ref[:, sl], c1, s1, swap_a)
        kb_ref[:, sl] = kr.astype(BF16)
        v = av_ref[:, sl]
        vt_ref[h * VT_ROWS:h * VT_ROWS + DH_A, :] = v.T.astype(BF16)
        vt_ref[h * VT_ROWS + DH_A:(h + 1) * VT_ROWS, :] = jnp.ones((VT_ROWS - DH_A, v.shape[0]), BF16)
        k_heads.append(kr)
        v_heads.append(v)
    kf_ref[...] = jnp.swapaxes(jnp.stack(k_heads, axis=0), 0, 1)
    vf_ref[...] = jnp.swapaxes(jnp.stack(v_heads, axis=0), 0, 1)

    zs = zs_ref[...]
    lane = lax.broadcasted_iota(jnp.int32, zs.shape, 1)
    ikr = _rope(zs, c2, s2, swap_i)
    ikf_ref[...] = ikr[:, :D_IDX]
    ike_ref[...] = jnp.where(lane < D_IDX, ikr, 0.0).astype(BF16)
    iko_ref[...] = jnp.where(lane >= D_IDX, pltpu.roll(ikr, D_IDX, 1), 0.0).astype(BF16)
    sgn_t = jnp.sign(pltpu.roll(zs, LANE - D_IDX, 1)).T
    sgn_ref[...] = sgn_t[:H_IDX, :]
    wscale = (D_IDX ** -0.5) * (H_IDX ** -0.5)
    for p in range(H_IDX // 2):
        sl = slice(p * LANE, (p + 1) * LANE)
        wa = jnp.abs(zs[:, D_IDX + 2 * p:D_IDX + 2 * p + 1])
        wb = jnp.abs(zs[:, D_IDX + 2 * p + 1:D_IDX + 2 * p + 2])
        wpair = jnp.where(lane < D_IDX, wa, wb) * wscale
        iqw_ref[:, sl] = (_rope(iq_ref[:, sl], c2, s2, swap_i) * wpair).astype(BF16)


def _prep(z, zs, tabs, tm):
    m = z.shape[0]
    period = tabs[0].shape[0] // tm
    zblk = lambda c: pl.BlockSpec((tm, 1024), lambda i, c=c: (i, c // 1024))
    tab = pl.BlockSpec((tm, LANE), lambda i: (i % period, 0))
    row = lambda w: pl.BlockSpec((tm, w), lambda i: (i, 0))
    col = lambda w: pl.BlockSpec((w, tm), lambda i: (0, i))
    sds = lambda w, dt: jax.ShapeDtypeStruct((m, w), dt)
    heads =pl.BlockSpec((tm, H_A, DH_A), lambda i: (i, 0, 0))
    heads_sds = jax.ShapeDtypeStruct((m, H_A, DH_A), F32)
    return pl.pallas_call(
        _prep_kernel,
        grid=(m // tm,),
        in_specs=[zblk(C_AQ), zblk(C_AK), zblk(C_AV), zblk(C_IQ), row(LANE)] + [tab] * 4,
        out_specs=[row(W_A), heads, row(W_A), heads, col(H_A * VT_ROWS), row(W_IDX), col(H_IDX),
                   row(D_IDX), row(LANE), row(LANE)],
        out_shape=[sds(W_A, BF16), heads_sds, sds(W_A, BF16), heads_sds,
                   jax.ShapeDtypeStruct((H_A * VT_ROWS, m), BF16), sds(W_IDX, BF16),
                   jax.ShapeDtypeStruct((H_IDX, m), F32),
                   sds(D_IDX, F32), sds(LANE, BF16), sds(LANE, BF16)],
        compiler_params=_cparams(("parallel",)),
        name="rope_prep",
    )(z, z, z, z, zs, *tabs)


def _rope_tables(pos, d):
    r = d // ROPE_FRAC
    half = r // 2
    n = pos.shape[0]
    inv = ROPE_THETA ** (-jnp.arange(half, dtype=F32) / half)
    ang = pos.astype(F32)[:, None] * inv[None, :]
    cos, sin = jnp.cos(ang), jnp.sin(ang)
    c = jnp.concatenate([cos, cos, jnp.ones((n, d - r), F32)], axis=1)
    s = jnp.concatenate([-sin, sin, jnp.zeros((n, d - r), F32)], axis=1)
    reps = LANE // d
    return [jnp.tile(t, (1, reps)) for t in (c, s)]


KEY_NEG_INF = -2139095041
F32_LOWEST = -3.4028234663852886e38


def _key_to_float(key):
    key = jnp.maximum(key, jnp.int32(KEY_NEG_INF))
    b = key ^ (lax.shift_right_arithmetic(key, 31) & jnp.int32(0x7FFFFFFF))
    return lax.bitcast_convert_type(b, F32)


def _lane_total(part):
    return _dot(part.astype(BF16), jnp.ones((LANE, LANE), BF16))


def _radix_select(count_ge, shape, total, nsel):
    def body(bi, carry):
        pu, cacc = carry
        bit = jnp.int32(31) - bi
        cand_u = pu | lax.shift_left(jnp.int32(1), bit)
        cnt = count_ge(_key_to_float(cand_u ^ jnp.int32(INT_MIN)))
        ok = cnt >= nsel
        return jnp.where(ok, cand_u, pu), jnp.where(ok, cnt, cacc)

    pu0 = jnp.zeros(shape, jnp.int32)
    c0 = jnp.full(shape, total, F32)
    pu, cacc = lax.fori_loop(0, 32, body, (pu0, c0))
    return _key_to_float(pu ^ jnp.int32(INT_MIN)), cacc


def _dsa_prompt_kernel(q_ref, k_ref, vt_ref, iq_ref, sgn_ref, ike_ref, iko_ref, o_ref,
                       key_ref, bias_ref, m_ref, acc_ref, *, tq, nsel):
    tk = tq
    i = pl.program_id(1)
    nch = i + 1
    sub = 8

    kidx0 = lax.broadcasted_iota(jnp.int32, (tk, tq), 0)
    qidx = i * tq + lax.broadcasted_iota(jnp.int32, (tk, tq), 1)

    def chunk_slice(c):
        return pl.ds(pl.multiple_of(c * tk, tk), tk)

    def score_body(c, carry):
        ds = chunk_slice(c)
        ke = ike_ref[ds, :]
        ko = iko_ref[ds, :]
        acc = jnp.zeros((tk, tq), F32)
        for h in range(H_IDX):
            rhs = iq_ref[:, (h // 2) * LANE:(h // 2 + 1) * LANE]
            d = _dot_nt(ke if h % 2 == 0 else ko, rhs)
            acc = acc + sgn_ref[h:h + 1, :] * jnp.maximum(d, 0.0)
        key_ref[ds, :] = jnp.where(kidx0 + c * tk <= qidx, acc, -jnp.inf)
        return carry

    lax.fori_loop(0, nch, score_body, 0)

    def count(cmp_fn):
        def hits(c):
            hit = jnp.where(cmp_fn(key_ref[chunk_slice(c), :]), 1.0, 0.0)
            return jnp.sum(hit.reshape(tk // sub, sub, tq), axis=0)

        def body(c2, parts):
            return parts[0] + hits(2 * c2), parts[1] + hits(2 * c2 + 1)

        zero = jnp.zeros((sub, tq), F32)
        p0, p1 = lax.fori_loop(0, nch // 2, body, (zero, zero))
        p0 = lax.cond(nch % 2 == 1, lambda p: p + hits(nch - 1), lambda p: p, p0)
        return jnp.sum(p0 + p1, axis=0, keepdims=True)

    total = (nch * tk).astype(F32)
    tau, c_ge = _radix_select(lambda cand: count(lambda kc: kc >= cand), (1, tq), total, float(nsel))
    tau_eff = jnp.maximum(tau, F32_LOWEST)

    def bias_body(c, carry):
        ds = chunk_slice(c)
        bias_ref[ds, :] = jnp.where(key_ref[ds, :] >= tau_eff, 0.0, NEG)
        return carry

    lax.fori_loop(0, nch, bias_body, 0)

    tie_q = jnp.logical_and(c_ge > float(nsel), tau > -jnp.inf)
    has_tie = jnp.max(jnp.where(tie_q, 1.0, 0.0)) > 0.5

    @pl.when(has_tie)
    def _():
        need = float(nsel) - count(lambda kc: kc > tau)
        lower = jnp.where(lax.broadcasted_iota(jnp.int32, (tk, tk), 1)
                          <= lax.broadcasted_iota(jnp.int32, (tk, tk), 0), 1.0, 0.0).astype(BF16)
        real = tau > -jnp.inf

        def tie_body(c, run):
            ds = chunk_slice(c)
            kc = key_ref[ds, :]
            eqf = jnp.where(kc == tau, 1.0, 0.0)
            pre = _dot(lower, eqf.astype(BF16)) + run
            keep = jnp.where(kc > tau, 1.0, jnp.where(real, eqf * jnp.where(pre <= need, 1.0, 0.0), 0.0))
            bias_ref[ds, :] = jnp.where(keep > 0.5, 0.0, NEG)
            return run + jnp.sum(eqf, axis=0, keepdims=True)

        lax.fori_loop(0, nch, tie_body, jnp.zeros((1, tq), F32))

    m_ref[...] = jnp.full(m_ref.shape, NEG, F32)
    acc_ref[...] = jnp.zeros(acc_ref.shape, F32)

    def attn_body(c, carry):
        ds = chunk_slice(c)
        bias = bias_ref[ds, :]
        for h0 in range(0, H_A, ATT_GROUP):
            group = range(h0, h0 + ATT_GROUP)
            ss = {h: _dot_nt(k_ref[ds, h * DH_A:(h + 1) * DH_A], q_ref[:, h * DH_A:(h + 1) * DH_A])
                  for h in group}
            ps, alphas = {}, {}
            for h in group:
                s = ss[h] + bias
                m_prev = m_ref[h]
                m_new = jnp.maximum(m_prev, jnp.max(s, axis=0, keepdims=True))
                alphas[h] = jnp.exp2(m_prev - m_new)
                ps[h] = jnp.exp2(s - m_new).astype(BF16)
                m_ref[h] = m_new
            for h in group:
                acc_ref[h] = alphas[h] * acc_ref[h] + _dot(vt_ref[h * VT_ROWS:(h + 1) * VT_ROWS, ds], ps[h])
        return carry

    lax.fori_loop(0, nch, attn_body, 0)
    for h in range(H_A):
        acc = acc_ref[h]
        o_ref[:, h * DH_A:(h + 1) * DH_A] = (acc[:DH_A] / acc[DH_A:DH_A + 1]).T


def _dsa_prompt(q, k, vt, iqw, sgn_t, ike, iko, n_b, n_t, tq):
    nsel = min(TOPK_MAX, n_t // 4)
    nq = n_t // tq
    qblk = lambda w: pl.BlockSpec((tq, w), lambda b, i: (b * nq + i, 0))
    full = lambda w: pl.BlockSpec((n_t, w), lambda b, i: (b, 0))
    return pl.pallas_call(
        functools.partial(_dsa_prompt_kernel, tq=tq, nsel=nsel),
        grid=(n_b, nq),
        in_specs=[qblk(W_A), full(W_A), pl.BlockSpec((H_A * VT_ROWS, n_t), lambda b, i: (0, b)), qblk(W_IDX),
                  pl.BlockSpec((H_IDX, tq), lambda b, i: (0, b * nq + i)), full(LANE), full(LANE)],
        out_specs=qblk(W_A),
        out_shape=jax.ShapeDtypeStruct((n_b * n_t, W_A), F32),
        scratch_shapes=[pltpu.VMEM((n_t, tq), F32),
                        pltpu.VMEM((n_t, tq), F32),
                        pltpu.VMEM((H_A, 1, tq), F32),
                        pltpu.VMEM((H_A, VT_ROWS, tq), F32)],
        compiler_params=_cparams(("parallel", "arbitrary")),
        name="dsa_prompt",
    )(q, k, vt, iqw, sgn_t, ike, iko)


HGRN_BLOCK = 8


def _hgrn_levels(chunk):
    levels, m = [], HGRN_BLOCK
    while m < chunk:
        levels.append(m)
        m *= 2
    return levels


def _hgrn_masks(chunk):
    row, col = np.indices((chunk, chunk))
    masks = []
    for m in _hgrn_levels(chunk):
        same_pair = (row // (2 * m)) == (col // (2 * m))
        masks.append(same_pair & (row % (2 * m) >= m) & (col % (2 * m) < m))
    for j in range(HGRN_BLOCK):
        masks.append((row - col == j) & (row % HGRN_BLOCK >= j))
    masks.append((row // HGRN_BLOCK == col // HGRN_BLOCK) & (col <= row))
    return np.stack(masks).astype(np.float32)


HGRN_SAFE_BLOCK_DECAY = 100.0


def _cumsum_rows(lower, x, mm_dtype):
    if mm_dtype == F32:
        return _dot(lower, x, precision=lax.Precision.HIGHEST)
    hi = x.astype(BF16)
    r1 = x - hi.astype(F32)
    mid = r1.astype(BF16)
    lo = (r1 - mid.astype(F32)).astype(BF16)
    lb16 = lower.astype(BF16)
    return _dot(lb16, hi) + _dot(lb16, mid) + _dot(lb16, lo)


def _hgrn_kernel(bq_ref, bf_ref, bi_ref, lb_ref, g_ref, s0_ref, mask_ref, o_ref, s_out_ref, st_ref,
                 *, t_blk, chunk, n_seq, heads, mm_dtype):
    c_ = chunk
    tt = pl.program_id(1)
    g = g_ref[...]

    @pl.when(tt == 0)
    def _():
        for bb in range(n_seq):
            for hh in range(heads):
                st_ref[bb * heads + hh] = s0_ref[0, bb, hh].T

    row = lax.broadcasted_iota(jnp.int32, (c_, c_), 0)
    col = lax.broadcasted_iota(jnp.int32, (c_, c_), 1)
    lower = jnp.where(col <= row, 1.0, 0.0)
    levels = _hgrn_levels(c_)
    n_blk = c_ // HGRN_BLOCK

    def decay_body(ci, worst):
        ds = pl.ds(pl.multiple_of(ci * c_, c_), c_)
        for bb in range(n_seq):
            for hh in range(heads):
                hs = slice(hh * DK_B, (hh + 1) * DK_B)
                lb = lb_ref[:, hs]
                lg = jnp.log2(lb + (1.0 - lb) * _sigmoid(bf_ref[bb, ds, hs]))
                blk = jnp.sum(lg.reshape(n_blk, HGRN_BLOCK, DK_B), axis=1)
                worst = jnp.minimum(worst, jnp.min(blk, axis=0, keepdims=True))
        return worst

    worst = lax.fori_loop(0, t_blk // c_, decay_body, jnp.zeros((1, DK_B), F32))
    blocks_safe = jnp.min(worst) > -HGRN_SAFE_BLOCK_DECAY

    def head_chunk(bb, hh, ds, one_matmul_blocks):
        hs = slice(hh * DK_B, (hh + 1) * DK_B)
        si = bb * heads + hh
        lb = lb_ref[:, hs]
        bq = bq_ref[bb, ds, hs]
        qs = bq * _sigmoid(bq)
        f = lb + (1.0 - lb) * _sigmoid(bf_ref[bb, ds, hs])
        kk = 1.0 - f
        v = bi_ref[bb, ds, hs]
        gcum = _cumsum_rows(lower, jnp.log2(f), mm_dtype)
        a = jnp.zeros((c_, c_), F32)
        for li, m in enumerate(levels):
            g3 = gcum.reshape(c_ // (2 * m), 2 * m, DK_B)
            ref = jnp.broadcast_to(g3[:, m - 1:m, :], g3.shape).reshape(c_, DK_B)
            qh = qs * jnp.exp2(jnp.minimum(gcum - ref, 0.0))
            kh = kk * jnp.exp2(jnp.minimum(ref - gcum, 0.0))
            a = a + _dot_nt(qh.astype(mm_dtype), kh.astype(mm_dtype)) * mask_ref[li]
        if one_matmul_blocks:
            g3 = gcum.reshape(n_blk, HGRN_BLOCK, DK_B)
            first = jnp.broadcast_to(g3[:, 0:1, :], g3.shape).reshape(c_, DK_B)
            qh = qs * jnp.exp2(gcum - first)
            kh = kk * jnp.exp2(first - gcum)
            a = a + _dot_nt(qh.astype(mm_dtype), kh.astype(mm_dtype)) * mask_ref[len(levels) + HGRN_BLOCK]
        else:
            for j in range(HGRN_BLOCK):
                kj = kk if j == 0 else pltpu.roll(kk, j, 0)
                gj = gcum if j == 0 else pltpu.roll(gcum, j, 0)
                e = jnp.exp2(jnp.minimum(gcum - gj, 0.0))
                aj = jnp.sum(qs * kj * e, axis=1, keepdims=True)
                a = a + aj * mask_ref[len(levels) + j]
        st = st_ref[si]
        o = (_dot_nt((qs * jnp.exp2(gcum)).astype(mm_dtype), st.astype(mm_dtype))
             + _dot(a.astype(mm_dtype), v.astype(mm_dtype)))
        glast = gcum[c_ - 1:c_, :]
        kd = kk * jnp.exp2(glast - gcum)
        st_ref[si] = jnp.exp2(glast) * st + _dot_tn(v.astype(mm_dtype), kd.astype(mm_dtype))
        o_ref[bb, ds, hs] = o * lax.rsqrt(jnp.mean(o * o, axis=1, keepdims=True) + RMS_EPS) * g

    def run(one_matmul_blocks):
        def chunk_body(ci, carry):
            ds = pl.ds(pl.multiple_of(ci * c_, c_), c_)
            for bb in range(n_seq):
                for hh in range(heads):
                    head_chunk(bb, hh, ds, one_matmul_blocks)
            return carry
        lax.fori_loop(0, t_blk // c_, chunk_body, 0)

    @pl.when(blocks_safe)
    def _():
        run(True)

    @pl.when(jnp.logical_not(blocks_safe))
    def _():
        run(False)

    @pl.when(tt == pl.num_programs(1) - 1)
    def _():
        for bb in range(n_seq):
            for hh in range(heads):
                s_out_ref[0, bb, hh] = st_ref[bb * heads + hh].T


def _hgrn(z, lb, g, s0, layer, n_b, n_t, chunk, n_seq, t_blk, mm_dtype):
    heads = H_B
    z3 = z.reshape(n_b, n_t, z.shape[1])
    zcol = lambda c: pl.BlockSpec((n_seq, t_blk, W_B), lambda i, tt, c=c: (i, tt, c // W_B))
    sblk = lambda lyr: pl.BlockSpec((1, n_seq, heads, DK_B, DV_B), lambda i, tt: (lyr, i, 0, 0, 0))
    masks = jnp.asarray(_hgrn_masks(chunk))
    bn, s_new = pl.pallas_call(
        functools.partial(_hgrn_kernel, t_blk=t_blk, chunk=chunk, n_seq=n_seq, heads=heads, mm_dtype=mm_dtype),
        grid=(n_b // n_seq, n_t // t_blk),
        in_specs=[zcol(C_BQ), zcol(C_BF), zcol(C_BI),
                  pl.BlockSpec((1, W_B), lambda i, tt: (0, 0)),
                  pl.BlockSpec((1, DV_B), lambda i, tt: (0, 0)),
                  sblk(layer),
                  pl.BlockSpec(masks.shape, lambda i, tt: (0, 0, 0))],
        out_specs=[pl.BlockSpec((n_seq, t_blk, W_B), lambda i, tt: (i, tt, 0)), sblk(0)],
        out_shape=[jax.ShapeDtypeStruct((n_b, n_t, W_B), F32),
                   jax.ShapeDtypeStruct((1, n_b, H_B, DK_B, DV_B), F32)],
        scratch_shapes=[pltpu.VMEM((n_seq * heads, DV_B, DK_B), F32)],
        compiler_params=_cparams(("parallel", "arbitrary")),
        name="hgrn2",
    )(z3, z3, z3, lb, g, s0, masks)
    return bn.reshape(n_b * n_t, W_B), s_new


def _mem_kv_kernel(x_ref, wk_ref, wv_ref, mk_ref, mv_ref):
    x = x_ref[0].astype(BF16)
    mk = _dot(x, wk_ref[...])
    mv = _dot(x, wv_ref[...])
    heads = lambda a: jnp.stack([a[:, h * DH_M:(h + 1) * DH_M] for h in range(H_M)], axis=0)
    mk_ref[0, 0] = jnp.swapaxes(heads(mk), 0, 1)
    mv_ref[0, 0] = jnp.swapaxes(heads(mv), 0, 1)


def _mem_kv(mem, wk, wv):
    n_b = mem.shape[0]
    wspec = pl.BlockSpec((D_MODEL, W_M), lambda b: (0, 0))
    ospec = pl.BlockSpec((1, 1, N_MEM, H_M, DH_M), lambda b: (0, b, 0, 0, 0))
    sds = jax.ShapeDtypeStruct((1, n_b, N_MEM, H_M, DH_M), F32)
    return pl.pallas_call(
        _mem_kv_kernel,
        grid=(n_b,),
        in_specs=[pl.BlockSpec((1, N_MEM, D_MODEL), lambda b: (b, 0, 0)), wspec, wspec],
        out_specs=[ospec, ospec],
        out_shape=[sds, sds],
        compiler_params=_cparams(("parallel",)),
        name="mem_kv",
    )(mem, wk, wv)


def _mem_kernel(q_ref, mk_ref, mv_ref, o_ref, *, mm_dtype):
    mk = jnp.swapaxes(mk_ref[0, 0], 0, 1).astype(mm_dtype)
    mv = jnp.swapaxes(mv_ref[0, 0], 0, 1).astype(mm_dtype)
    for h in range(H_M):
        hs = slice(h * DH_M, (h + 1) * DH_M)
        q = (q_ref[:, hs] * (DH_M ** -0.5)).astype(mm_dtype)
        s = _dot_nt(q, mk[h])
        p = jnp.exp(s - jnp.max(s, axis=1, keepdims=True))
        l = jnp.sum(p, axis=1, keepdims=True)
        o_ref[:, hs] = _dot(p.astype(mm_dtype), mv[h]) / l


def _mem_attend(z, mk, mv, layer, n_b, n_t, tq, mm_dtype):
    nq = n_t // tq
    kv = pl.BlockSpec((1, 1, N_MEM, H_M, DH_M), lambda b, i: (layer, b, 0, 0, 0))
    return pl.pallas_call(
        functools.partial(_mem_kernel, mm_dtype=mm_dtype),
        grid=(n_b, nq),
        in_specs=[pl.BlockSpec((tq, W_M), lambda b, i: (b * nq + i, C_MQ // W_M)), kv, kv],
        out_specs=pl.BlockSpec((tq, W_M), lambda b, i: (b * nq + i, 0)),
        out_shape=jax.ShapeDtypeStruct((n_b * n_t, W_M), F32),
        compiler_params=_cparams(("parallel", "parallel")),
        name="mem_attend",
    )(z, mk, mv)


def _merge_kernel(a_ref, ag_ref, bn_ref, bg_ref, mo_ref, mg_ref, h_ref, wo_ref, lg_ref, lbias_ref, o_ref):
    a = (a_ref[...] * _silu(ag_ref[...])).astype(BF16)
    b = (bn_ref[...] * _silu(bg_ref[...])).astype(BF16)
    m = (mo_ref[...] * _silu(mg_ref[...])).astype(BF16)
    y = (_dot(a, wo_ref[0:W_A, :]) + _dot(b, wo_ref[W_A:W_A + W_B, :])
         + _dot(m, wo_ref[W_A + W_B:D_MODEL, :]))
    r = ALPHA * h_ref[...] + y
    xc = r - jnp.mean(r, axis=1, keepdims=True)
    var = jnp.mean(xc * xc, axis=1, keepdims=True)
    o_ref[...] = xc * lax.rsqrt(var + LN_EPS) * lg_ref[...] + lbias_ref[...]


def _merge(a_out, z, bn, m_out, h, w_out, ln_g, ln_b, tm):
    m = h.shape[0]
    row = lambda w: pl.BlockSpec((tm, w), lambda i: (i, 0))
    zcol = lambda c, w: pl.BlockSpec((tm, w), lambda i, c=c, w=w: (i, c // w))
    const = lambda s: pl.BlockSpec(s, lambda i: (0, 0), pipeline_mode=pl.Buffered(1))
    return pl.pallas_call(
        _merge_kernel,
        grid=(m // tm,),
        in_specs=[row(W_A), zcol(C_AG, W_A), row(W_B), zcol(C_BG, W_B), row(W_M), zcol(C_MG, W_M),
                  row(D_MODEL), const((D_MODEL, D_MODEL)), const((1, D_MODEL)), const((1, D_MODEL))],
        out_specs=row(D_MODEL),
        out_shape=jax.ShapeDtypeStruct((m, D_MODEL), F32),
        compiler_params=_cparams(("parallel",)),
        name="merge",
    )(a_out, z, bn, z, m_out, z, h, w_out, ln_g, ln_b)


PAGES_PER_STEP = 8


def _dsa_sample_select_kernel(pt_ref, pool_ref, lq_ref, sg_ref, iknew_ref, bias_ref, key_ref, page_buf, sem,
                              *, layer, n_t, n_pages, nsel):
    r = pl.program_id(0)
    slot = r % 2
    n_keys_pad = (n_pages + 1) * PAGE_SIZE

    def page_copy(req, p, s):
        return pltpu.make_async_copy(pool_ref.at[layer, pt_ref[req, p]], page_buf.at[s, p], sem.at[s])

    def for_pages(fn):
        def body(p, carry):
            fn(p)
            return carry
        lax.fori_loop(0, n_pages, body, 0)

    @pl.when(r == 0)
    def _():
        for_pages(lambda p: page_copy(0, p, 0).start())

    @pl.when(r + 1 < pl.num_programs(0))
    def _():
        for_pages(lambda p: page_copy(r + 1, p, 1 - slot).start())

    for_pages(lambda p: page_copy(r, p, slot).wait())

    lq = lq_ref[0]
    sg = sg_ref[0]

    def score(ik_t):
        d = _dot(lq, ik_t.astype(BF16))
        r = jnp.maximum(d, 0.0) * jnp.concatenate([sg] * (ik_t.shape[1] // PAGE_SIZE), axis=1)
        s = r[0:n_t]
        for h in range(1, H_IDX):
            s = s + r[h * n_t:(h + 1) * n_t]
        return s

    width = PAGES_PER_STEP * PAGE_SIZE

    rows = pl.ds(pl.multiple_of(r * n_t, n_t), n_t)

    def group_body(gi, carry):
        base = gi * PAGES_PER_STEP
        ik_step = jnp.concatenate([page_buf[slot, base + u] for u in range(PAGES_PER_STEP)], axis=1)
        key_ref[rows, pl.ds(pl.multiple_of(gi * width, width), width)] = score(ik_step)
        return carry

    lax.fori_loop(0, n_pages // PAGES_PER_STEP, group_body, 0)

    past = n_pages * PAGE_SIZE
    s_new = score(iknew_ref[0])
    t_i = lax.broadcasted_iota(jnp.int32, (n_t, PAGE_SIZE), 0)
    k_i = lax.broadcasted_iota(jnp.int32, (n_t, PAGE_SIZE), 1)
    key_ref[rows, past:n_keys_pad] = jnp.where(k_i <= t_i, s_new, -jnp.inf)

    @pl.when(r == pl.num_programs(0) - 1)
    def _():
        n_rows = key_ref.shape[0]
        n_tiles = n_keys_pad // LANE

        def count_ge(cand):
            part = jnp.zeros((n_rows, LANE), F32)
            for c in range(n_tiles):
                part = part + jnp.where(key_ref[:, c * LANE:(c + 1) * LANE] >= cand, 1.0, 0.0)
            return _lane_total(part)

        tau_r, c_ge_r = _radix_select(count_ge, (n_rows, LANE), float(n_keys_pad), float(nsel))
        tau = tau_r[:, 0:1]
        tau_eff = jnp.maximum(tau, F32_LOWEST)
        bias_ref[...] = jnp.where(key_ref[...] >= tau_eff, 0.0, NEG)

        tie_rows = jnp.logical_and(c_ge_r > float(nsel), tau_r > -jnp.inf)
        has_tie = jnp.max(jnp.where(tie_rows, 1.0, 0.0)) > 0.5

        @pl.when(has_tie)
        def _():
            c_gt = jnp.sum(jnp.where(key_ref[...] > tau, 1.0, 0.0), axis=1, keepdims=True)
            need = float(nsel) - c_gt
            ri = lax.broadcasted_iota(jnp.int32, (PAGE_SIZE, PAGE_SIZE), 0)
            ci = lax.broadcasted_iota(jnp.int32, (PAGE_SIZE, PAGE_SIZE), 1)
            upper = jnp.where(ri <= ci, 1.0, 0.0)
            real = tau > -jnp.inf

            def tie_body(c, run):
                ds = pl.ds(pl.multiple_of(c * PAGE_SIZE, PAGE_SIZE), PAGE_SIZE)
                kc = key_ref[:, ds]
                eqf = jnp.where(kc == tau, 1.0, 0.0)
                pre = _dot(eqf, upper) + run
                keep = jnp.where(kc > tau, 1.0, jnp.where(real, eqf * jnp.where(pre <= need, 1.0, 0.0), 0.0))
                bias_ref[:, ds] = jnp.where(keep > 0.5, 0.0, NEG)
                return run + jnp.sum(eqf, axis=1, keepdims=True)

            lax.fori_loop(0, n_keys_pad // PAGE_SIZE, tie_body, jnp.zeros((n_rows, 1), F32))


def _dsa_sample_select(page_table, idx_pool, layer, lq, sgb, iknew, n_b, n_t, nsel):
    n_pages = page_table.shape[1]
    n_keys_pad = (n_pages + 1) * PAGE_SIZE
    per_req = lambda s: pl.BlockSpec((1,) + s, lambda r, pt: (r, 0, 0))
    return pl.pallas_call(
        functools.partial(_dsa_sample_select_kernel, layer=layer, n_t=n_t, n_pages=n_pages, nsel=nsel),
        grid_spec=pltpu.PrefetchScalarGridSpec(
            num_scalar_prefetch=1,
            grid=(n_b,),
            in_specs=[pl.BlockSpec(memory_space=pl.ANY),
                      per_req((H_IDX * n_t, D_IDX)), per_req((H_IDX * n_t, PAGE_SIZE)), per_req((D_IDX, PAGE_SIZE))],
            out_specs=pl.BlockSpec((n_b * n_t, n_keys_pad), lambda r, pt: (0, 0)),
            scratch_shapes=[pltpu.VMEM((n_b * n_t, n_keys_pad), F32),
                            pltpu.VMEM((2, n_pages, D_IDX, PAGE_SIZE), F32),
                            pltpu.SemaphoreType.DMA((2,))]),
        out_shape=jax.ShapeDtypeStruct((n_b * n_t, n_keys_pad), F32),
        compiler_params=_cparams(("arbitrary",)),
        name="dsa_sample_select",
    )(page_table, idx_pool, lq, sgb, iknew)


ATT_PAGES = 16


def _dsa_sample_attend_kernel(pt_ref, *refs, n_t, steps):
    kp = refs[:ATT_PAGES]
    vp = refs[ATT_PAGES:2 * ATT_PAGES]
    qbd_ref, bias_ref, knew_ref, vnew_ref, o_ref, m_ref, l_ref, acc_ref = refs[2 * ATT_PAGES:]
    j = pl.program_id(1)
    rows = H_A * n_t

    @pl.when(j == 0)
    def _():
        m_ref[...] = jnp.full(m_ref.shape, NEG, F32)
        l_ref[...] = jnp.zeros(l_ref.shape, F32)
        acc_ref[...] = jnp.zeros(acc_ref.shape, F32)

    qbd = qbd_ref[0]

    def update(kblk, vblk, bias):
        n = kblk.shape[0]
        s = _dot_nt(qbd, kblk) + jnp.concatenate([bias] * H_A, axis=0)
        m_prev = m_ref[...]
        m_new = jnp.maximum(m_prev, jnp.max(s, axis=1, keepdims=True))
        alpha = jnp.exp2(m_prev - m_new)
        p = jnp.exp2(s - jnp.concatenate([m_new] * (n // LANE), axis=1))
        l_ref[...] = alpha * l_ref[...] + jnp.sum(p, axis=1, keepdims=True)
        pv = _dot(p.astype(BF16), vblk)
        diag = jnp.concatenate(
            [pv[h * n_t:(h + 1) * n_t, h * DH_A:(h + 1) * DH_A] for h in range(H_A)], axis=0)
        acc_ref[...] = alpha * acc_ref[...] + diag
        m_ref[...] = m_new

    def page2d(ref):
        return jnp.concatenate(
            [ref[0, 0, pl.ds(h, PAGE_SIZE, stride=H_A), :] for h in range(H_A)], axis=1).astype(BF16)

    kblk = jnp.concatenate([page2d(kp[u]) for u in range(ATT_PAGES)], axis=0)
    vblk = jnp.concatenate([page2d(vp[u]) for u in range(ATT_PAGES)], axis=0)
    width = ATT_PAGES * PAGE_SIZE
    off = pl.multiple_of(j * width, width)
    update(kblk, vblk, bias_ref[0, :, pl.ds(off, width)])

    @pl.when(j == steps - 1)
    def _():
        past = steps * width
        update(page2d(knew_ref), page2d(vnew_ref), bias_ref[0, :, past:past + PAGE_SIZE])
        o = acc_ref[...] / l_ref[...]
        o_ref[0] = jnp.concatenate([o[h * n_t:(h + 1) * n_t] for h in range(H_A)], axis=1)


def _dsa_sample_attend(page_table, k_pool, v_pool, layer, qbd, bias, knew, vnew, n_b, n_t):
    n_pages = page_table.shape[1]
    steps = n_pages // ATT_PAGES
    n_keys_pad = (n_pages + 1) * PAGE_SIZE
    page_spec = lambda u: pl.BlockSpec(
        (1, 1, PAGE_SIZE * H_A, DH_A), lambda r, j, pt, u=u: (layer, pt[r, j * ATT_PAGES + u], 0, 0))
    per_req = lambda s: pl.BlockSpec((1,) + s, lambda r, j, pt: (r, 0, 0))
    rows = H_A * n_t
    return pl.pallas_call(
        functools.partial(_dsa_sample_attend_kernel, n_t=n_t, steps=steps),
        grid_spec=pltpu.PrefetchScalarGridSpec(
            num_scalar_prefetch=1,
            grid=(n_b, steps),
            in_specs=[page_spec(u) for u in range(ATT_PAGES)] * 2
            + [per_req((rows, W_A)), per_req((n_t, n_keys_pad))]
            + [pl.BlockSpec((1, 1, PAGE_SIZE * H_A, DH_A), lambda r, j, pt: (r, 0, 0, 0))] * 2,
            out_specs=per_req((n_t, W_A)),
            scratch_shapes=[pltpu.VMEM((rows, LANE), F32), pltpu.VMEM((rows, LANE), F32),
                            pltpu.VMEM((rows, DH_A), F32)]),
        out_shape=jax.ShapeDtypeStruct((n_b, n_t, W_A), F32),
        compiler_params=_cparams(("parallel", "arbitrary")),
        name="dsa_sample_attend",
    )(page_table, *([k_pool] * ATT_PAGES), *([v_pool] * ATT_PAGES), qbd, bias, knew, vnew)


def kernel(x_prompt, x_sample, mem_prompt, cache_k, cache_v, cache_idx_k, state_hgrn, cache_mem_k,
           cache_mem_v, page_table, w_in, lb_logits, hgrn_norm_g, w_mem_k, w_mem_v, w_out, ln_g, ln_b):
    n_bp, t_p, _ = x_prompt.shape
    n_bs, t_s, _ = x_sample.shape
    n_pages = page_table.shape[1]
    past = n_pages * PAGE_SIZE
    l = 0

    lb_all = jnp.cumsum(jax.nn.softmax(lb_logits.astype(F32), axis=0), axis=0)
    lb = lb_all[l][None, :]
    g_norm = hgrn_norm_g[l][None, :]
    w_t = _cast_rows(jnp.swapaxes(w_in, 1, 2), l, tr=1024)
    w_o = w_out[l].astype(BF16)
    lg, lbias = ln_g[l][None, :], ln_b[l][None, :]

    xp = x_prompt.reshape(n_bp * t_p, D_MODEL)
    z, zs = _proj(xp, w_t, tm=1024, tn=1024)
    pos_p = jnp.arange(t_p, dtype=jnp.int32)
    tabs = _rope_tables(pos_p, DH_A) + _rope_tables(pos_p, D_IDX)
    q, kf, kb, vf, vt, iqw, sgn_t, ikf, ike, iko = _prep(z, zs, tabs, tm=256)
    a_out = _dsa_prompt(q, kb, vt, iqw, sgn_t, ike, iko, n_bp, t_p, tq=256)
    s0 = jnp.zeros((1, n_bp, H_B, DK_B, DV_B), F32)
    bn, s_p = _hgrn(z, lb, g_norm, s0, 0, n_bp, t_p, chunk=128, n_seq=4, t_blk=t_p // 4, mm_dtype=BF16)
    mk_p, mv_p = _mem_kv(mem_prompt, w_mem_k[l].astype(BF16), w_mem_v[l].astype(BF16))
    m_out = _mem_attend(z, mk_p, mv_p, 0, n_bp, t_p, tq=512, mm_dtype=BF16)
    y_p = _merge(a_out, z, bn, m_out, xp, w_o, lg, lbias, tm=256)

    n_s = n_bs * t_s
    xs = x_sample.reshape(n_s, D_MODEL)
    z2, zs2 = _proj(xs, w_t, tm=n_s, tn=1024)
    pos_s = past + jnp.arange(t_s, dtype=jnp.int32)
    tabs_s = [jnp.tile(t, (n_bs, 1)) for t in _rope_tables(pos_s, DH_A) + _rope_tables(pos_s, D_IDX)]
    q2, kf2, kb2, vf2, vt2, iqw2, sgn_t2, ikf2, ike2, iko2 = _prep(z2, zs2, tabs_s, tm=n_s)
    nsel_s = min(TOPK_MAX, (past + t_s) // 4)
    lq = iqw2.reshape(n_bs, t_s, H_IDX, D_IDX).transpose(0, 2, 1, 3).reshape(n_bs, H_IDX * t_s, D_IDX)
    sgb = jnp.broadcast_to(
        sgn_t2.reshape(H_IDX, n_bs, t_s).transpose(1, 0, 2).reshape(n_bs, H_IDX * t_s, 1),
        (n_bs, H_IDX * t_s, PAGE_SIZE))
    pad_keys = lambda a: jnp.pad(a.reshape((n_bs, t_s) + a.shape[1:]),
                                 ((0, 0), (0, PAGE_SIZE - t_s)) + ((0, 0),) * (a.ndim - 1))
    idx_pool_t = jnp.swapaxes(cache_idx_k, 2, 3)
    bias = _dsa_sample_select(page_table, idx_pool_t, l, lq, sgb, jnp.swapaxes(pad_keys(ikf2), 1, 2),
                              n_bs, t_s, nsel_s).reshape(n_bs, t_s, -1)
    q4 = q2.reshape(n_bs, t_s, H_A, DH_A)
    eye = jnp.eye(H_A, dtype=q2.dtype)
    qbd = (q4[:, None, :, :, :] * eye[None, :, None, :, None]).reshape(n_bs, H_A * t_s, W_A)
    n_layers, n_phys = cache_k.shape[:2]
    as_rows = lambda a: a.reshape(a.shape[0], a.shape[1], PAGE_SIZE * H_A, DH_A)
    a_out2 = _dsa_sample_attend(page_table, as_rows(cache_k), as_rows(cache_v), l, qbd, bias,
                                as_rows(pad_keys(kf2)[:, None]), as_rows(pad_keys(vf2)[:, None]), n_bs, t_s)
    bn2, s_s = _hgrn(z2, lb, g_norm, state_hgrn, l, n_bs, t_s, chunk=t_s, n_seq=2, t_blk=t_s, mm_dtype=F32)
    m_out2 = _mem_attend(z2, cache_mem_k, cache_mem_v, l, n_bs, t_s, tq=t_s, mm_dtype=F32)
    y_s = _merge(a_out2.reshape(n_s, W_A), z2, bn2, m_out2, xs, w_o, lg, lbias, tm=n_s)

    return (y_p.reshape(n_bp, t_p, D_MODEL), y_s.reshape(n_bs, t_s, D_MODEL),
            kf.reshape(1, n_bp, t_p, H_A, DH_A), vf.reshape(1, n_bp, t_p, H_A, DH_A),
            ikf.reshape(1, n_bp, t_p, D_IDX), s_p, mk_p, mv_p,
            kf2.reshape(1, n_bs, t_s, H_A, DH_A), vf2.reshape(1, n_bs, t_s, H_A, DH_A),
            ikf2.reshape(1, n_bs, t_s, D_IDX), s_s)
```
